```python
import math
import jax, jax.numpy as jnp
from jax import lax
import numpy as np

D_MODEL = 1024
BATCH = 8
SEQ = 4096
DEPTH = 4

HEAD_DIM = 64
SWA_Q_HEADS = 8
SWA_KV_HEADS = 2
SWA_GROUP = SWA_Q_HEADS // SWA_KV_HEADS
WINDOW = 128
FOX_HEADS = 4
MLA_HEADS = 4
MLA_Q_RANK = 256
MLA_KV_RANK = 128
MLA_NOPE_DIM = 64
MLA_ROPE_DIM = 32
MLA_V_DIM = 64
ROPE_THETA = 10000.0
REL_BUCKETS = 32
REL_MAX_DIST = 128
D_FF = 2816
CONV_WIDTH = 3
Q_BLOCK = 128
EPS = 1e-6
NEG_INF = -1e30

SWA_WIDTH = SWA_Q_HEADS * HEAD_DIM
FOX_WIDTH = FOX_HEADS * HEAD_DIM
MLA_WIDTH = MLA_HEADS * MLA_V_DIM
MIX_WIDTH = SWA_WIDTH + FOX_WIDTH + MLA_WIDTH
SWA_COLS = (SWA_Q_HEADS + 2 * SWA_KV_HEADS) * HEAD_DIM
FOX_COLS = 3 * FOX_HEADS * HEAD_DIM + FOX_HEADS
MLA_COLS = MLA_Q_RANK + MLA_KV_RANK + MLA_ROPE_DIM
IN_COLS = SWA_COLS + FOX_COLS + MLA_COLS
MLA_QK_DIM = MLA_NOPE_DIM + MLA_ROPE_DIM

kernel_name = "hymba_swa_fox_mla_convffn_trunk"


def rmsnorm(x, g):
    xf = x.astype(jnp.float32)
    y = xf * lax.rsqrt(jnp.mean(xf * xf, axis=-1, keepdims=True) + EPS) * g.astype(jnp.float32)
    return y.astype(x.dtype)


def t5_causal_bucket(dist):
    max_exact = REL_BUCKETS // 2
    d = jnp.maximum(dist, 0)
    log_ratio = jnp.log(jnp.maximum(d, 1).astype(jnp.float32) / max_exact) / math.log(REL_MAX_DIST / max_exact)
    large = max_exact + (log_ratio * (REL_BUCKETS - max_exact)).astype(jnp.int32)
    large = jnp.minimum(large, REL_BUCKETS - 1)
    return jnp.where(d < max_exact, d, large)


def apply_rope(t, cos, sin):
    t1, t2 = jnp.split(t, 2, axis=-1)
    return jnp.concatenate([t1 * cos - t2 * sin, t1 * sin + t2 * cos], axis=-1)


def swa_sink_attention(q, k, v, sinks, rel_bias):
    B, S = q.shape[0], q.shape[1]
    nb = S // WINDOW
    qb = q.reshape(B, nb, WINDOW, SWA_KV_HEADS, SWA_GROUP, HEAD_DIM)

    def band(t):
        tb = t.reshape(B, nb, WINDOW, SWA_KV_HEADS, HEAD_DIM)
        prev = jnp.pad(tb, ((0, 0), (1, 0), (0, 0), (0, 0), (0, 0)))[:, :-1]
        return jnp.concatenate([prev, tb], axis=2)

    kb, vb = band(k), band(v)
    qi = jnp.arange(WINDOW, dtype=jnp.int32)[:, None] + WINDOW
    kj = jnp.arange(2 * WINDOW, dtype=jnp.int32)[None, :]
    dist = qi - kj
    in_band = (dist >= 0) & (dist < WINDOW)
    valid_key = (jnp.arange(nb)[:, None, None] > 0) | (kj >= WINDOW)[None]
    mask = in_band[None] & valid_key
    bias = rel_bias.astype(jnp.float32)[t5_causal_bucket(dist)]
    bias = bias.transpose(2, 0, 1).reshape(SWA_KV_HEADS, SWA_GROUP, WINDOW, 2 * WINDOW)
    s = jnp.einsum('bnqhgd,bnkhd->bnhgqk', qb, kb, preferred_element_type=jnp.float32)
    s = s * (HEAD_DIM ** -0.5) + bias
    s = jnp.where(mask[None, :, None, None], s, NEG_INF)
    sink = sinks.astype(jnp.float32).reshape(1, 1, SWA_KV_HEADS, SWA_GROUP, 1, 1)
    sink = jnp.broadcast_to(sink, s.shape[:-1] + (1,))
    p = jax.nn.softmax(jnp.concatenate([s, sink], axis=-1), axis=-1)[..., :-1]
    o = jnp.einsum('bnhgqk,bnkhd->bnqhgd', p.astype(v.dtype), vb)
    return o.reshape(B, S, SWA_Q_HEADS * HEAD_DIM)


def blocked_causal_attention(q, k, v, scale, log_forget_cum=None):
    B, S, H = q.shape[0], q.shape[1], q.shape[2]
    nb = S // Q_BLOCK
    q_blocks = q.reshape(B, nb, Q_BLOCK, H, q.shape[-1]).swapaxes(0, 1)
    k_pos = jnp.arange(S, dtype=jnp.int32)
    if log_forget_cum is None:
        f_blocks, f_keys = None, None
    else:
        f_t = log_forget_cum.transpose(0, 2, 1)
        f_blocks = f_t.reshape(B, H, nb, Q_BLOCK).transpose(2, 0, 1, 3)
        f_keys = f_t

    def one_block(args):
        qb, fb, i = args
        s = jnp.einsum('bqhd,bkhd->bhqk', qb, k, preferred_element_type=jnp.float32) * scale
        if fb is not None:
            s = s + (fb[..., :, None] - f_keys[..., None, :])
        q_pos = i * Q_BLOCK + jnp.arange(Q_BLOCK, dtype=jnp.int32)
        s = jnp.where(k_pos[None, :] <= q_pos[:, None], s, NEG_INF)
        p = jax.nn.softmax(s, axis=-1)
        return jnp.einsum('bhqk,bkhd->bqhd', p.astype(v.dtype), v)

    out = lax.map(one_block, (q_blocks, f_blocks, jnp.arange(nb, dtype=jnp.int32)))
    return out.swapaxes(0, 1).reshape(B, S, H * v.shape[-1])


def causal_depthwise_conv(u, w, b):
    S = u.shape[1]
    up = jnp.pad(u, ((0, 0), (CONV_WIDTH - 1, 0), (0, 0)))
    y = b.astype(u.dtype)
    for tap in range(CONV_WIDTH):
        y = y + w[tap].astype(u.dtype) * up[:, tap:tap + S]
    return y


def setup_inputs(seed: int = 0) -> dict:
    key = jax.random.key(seed)
    ks = jax.random.split(key, 20)
    L, D = DEPTH, D_MODEL
    f32 = jnp.float32

    def nrm(k, shape, scale):
        return jax.random.normal(k, shape, f32) * scale

    def gain(k, shape):
        return 1.0 + 0.05 * jax.random.normal(k, shape, f32)

    return {
        "x": jax.random.normal(ks[0], (BATCH, SEQ, D), f32),
        "attn_pre_norm": gain(ks[1], (L, D)),
        "w_in": nrm(ks[2], (L, D, IN_COLS), D ** -0.5),
        "forget_bias": 2.0 + 0.5 * jax.random.normal(ks[3], (L, FOX_HEADS), f32),
        "swa_sinks": nrm(ks[4], (L, SWA_Q_HEADS), 0.5),
        "rel_bias": nrm(ks[5], (REL_BUCKETS, SWA_Q_HEADS), 0.5),
        "q_latent_norm": gain(ks[6], (L, MLA_Q_RANK)),
        "w_uq": nrm(ks[7], (L, MLA_Q_RANK, MLA_HEADS * MLA_QK_DIM), MLA_Q_RANK ** -0.5),
        "kv_latent_norm": gain(ks[8], (L, MLA_KV_RANK)),
        "w_ukv": nrm(ks[9], (L, MLA_KV_RANK, MLA_HEADS * (MLA_NOPE_DIM + MLA_V_DIM)), MLA_KV_RANK ** -0.5),
        "group_norm": gain(ks[10], (L, MIX_WIDTH)),
        "w_out": nrm(ks[11], (L, MIX_WIDTH, D), MIX_WIDTH ** -0.5),
        "attn_post_norm": gain(ks[12], (L, D)),
        "ffn_pre_norm": gain(ks[13], (L, D)),
        "w_up": nrm(ks[14], (L, D, 2 * D_FF), D ** -0.5),
        "conv_w": nrm(ks[15], (L, CONV_WIDTH, 2 * D_FF), CONV_WIDTH ** -0.5),
        "conv_b": nrm(ks[16], (L, 2 * D_FF), 0.02),
        "w_down": nrm(ks[17], (L, D_FF, D), D_FF ** -0.5),
        "ffn_post_norm": gain(ks[18], (L, D)),
    }


def reference(x, attn_pre_norm, w_in, forget_bias, swa_sinks, rel_bias, q_latent_norm, w_uq,
              kv_latent_norm, w_ukv, group_norm, w_out, attn_post_norm, ffn_pre_norm, w_up,
              conv_w, conv_b, w_down, ffn_post_norm):
    B, S, _ = x.shape
    pos = jnp.arange(S, dtype=jnp.float32)
    inv_freq = ROPE_THETA ** (-(jnp.arange(MLA_ROPE_DIM // 2, dtype=jnp.float32) * 2.0 / MLA_ROPE_DIM))
    ang = pos[:, None] * inv_freq[None, :]
    cos = jnp.cos(ang)[:, None, :].astype(x.dtype)
    sin = jnp.sin(ang)[:, None, :].astype(x.dtype)

    for l in range(DEPTH):
        h = rmsnorm(x, attn_pre_norm[l])
        proj = h @ w_in[l]
        a_cols, f_cols, m_cols = jnp.split(proj, [SWA_COLS, SWA_COLS + FOX_COLS], axis=-1)

        qa, ka, va = jnp.split(a_cols, [SWA_WIDTH, SWA_WIDTH + SWA_KV_HEADS * HEAD_DIM], axis=-1)
        out_a = swa_sink_attention(qa.reshape(B, S, SWA_Q_HEADS, HEAD_DIM),
                                   ka.reshape(B, S, SWA_KV_HEADS, HEAD_DIM),
                                   va.reshape(B, S, SWA_KV_HEADS, HEAD_DIM),
                                   swa_sinks[l], rel_bias)

        qf, kf, vf, f_logit = jnp.split(f_cols, [FOX_WIDTH, 2 * FOX_WIDTH, 3 * FOX_WIDTH], axis=-1)
        log_f = jax.nn.log_sigmoid(f_logit.astype(jnp.float32) + forget_bias[l].astype(jnp.float32))
        F = jnp.cumsum(log_f, axis=1)
        out_b = blocked_causal_attention(qf.reshape(B, S, FOX_HEADS, HEAD_DIM),
                                         kf.reshape(B, S, FOX_HEADS, HEAD_DIM),
                                         vf.reshape(B, S, FOX_HEADS, HEAD_DIM),
                                         HEAD_DIM ** -0.5, log_forget_cum=F)

        c_q, c_kv, k_rope = jnp.split(m_cols, [MLA_Q_RANK, MLA_Q_RANK + MLA_KV_RANK], axis=-1)
        qm = (rmsnorm(c_q, q_latent_norm[l]) @ w_uq[l]).reshape(B, S, MLA_HEADS, MLA_QK_DIM)
        q_nope, q_rope = jnp.split(qm, [MLA_NOPE_DIM], axis=-1)
        kv = (rmsnorm(c_kv, kv_latent_norm[l]) @ w_ukv[l]).reshape(B, S, MLA_HEADS, MLA_NOPE_DIM + MLA_V_DIM)
        k_nope, vm = jnp.split(kv, [MLA_NOPE_DIM], axis=-1)
        q_rope = apply_rope(q_rope, cos, sin)
        k_rope = jnp.broadcast_to(apply_rope(k_rope[:, :, None, :], cos, sin), (B, S, MLA_HEADS, MLA_ROPE_DIM))
        out_c = blocked_causal_attention(jnp.concatenate([q_nope, q_rope], axis=-1),
                                         jnp.concatenate([k_nope, k_rope], axis=-1),
                                         vm, MLA_QK_DIM ** -0.5)

        g_a, g_b, g_c = jnp.split(group_norm[l], [SWA_WIDTH, SWA_WIDTH + FOX_WIDTH])
        mixed = jnp.concatenate([rmsnorm(out_a, g_a), rmsnorm(out_b, g_b), rmsnorm(out_c, g_c)], axis=-1)
        x = x + rmsnorm(mixed @ w_out[l], attn_post_norm[l])

        h = rmsnorm(x, ffn_pre_norm[l])
        u = causal_depthwise_conv(h @ w_up[l], conv_w[l], conv_b[l])
        gate, up = jnp.split(u, 2, axis=-1)
        y = (jax.nn.gelu(gate, approximate=True) * up) @ w_down[l]
        x = x + rmsnorm(y, ffn_post_norm[l])
    return x
```

```python
import functools
import math

import jax
import jax.numpy as jnp
from jax import lax
from jax.experimental import pallas as pl
from jax.experimental.pallas import tpu as pltpu

D_MODEL = 1024
HEAD_DIM = 64
SWA_Q_HEADS = 8
SWA_KV_HEADS = 2
WINDOW = 128
FOX_HEADS = 4
MLA_HEADS = 4
MLA_Q_RANK = 256
MLA_KV_RANK = 128
MLA_NOPE_DIM = 64
MLA_ROPE_DIM = 32
MLA_V_DIM = 64
MLA_QK_DIM = MLA_NOPE_DIM + MLA_ROPE_DIM
ROPE_THETA = 10000.0
REL_BUCKETS = 32
REL_MAX_DIST = 128
D_FF = 2816
EPS = 1e-6
NEG_INF = -1e30

SWA_WIDTH = SWA_Q_HEADS * HEAD_DIM
FOX_WIDTH = FOX_HEADS * HEAD_DIM
MLA_WIDTH = MLA_HEADS * MLA_V_DIM
SWA_COLS = (SWA_Q_HEADS + 2 * SWA_KV_HEADS) * HEAD_DIM
FOX_COLS = 3 * FOX_WIDTH + FOX_HEADS

LANES = 128
ROW_TILE = 512
ATTN_TILE = 512
FF_CHUNK = 256
N_FF_CHUNKS = D_FF // FF_CHUNK
VMEM_LIMIT = 56 * 1024 * 1024

SWA_HEAD_ORDER = (0, 4, 1, 5, 2, 6, 3, 7)

F32 = jnp.float32
BF16 = jnp.bfloat16
NT_DIMS = (((1,), (1,)), ((), ()))


def _rms(x, g):
    ms = jnp.mean(x * x, axis=-1, keepdims=True)
    return x * lax.rsqrt(ms + EPS) * g


def _params(*sem):
    return pltpu.CompilerParams(dimension_semantics=sem, vmem_limit_bytes=VMEM_LIMIT)


def _full(shape):
    nd = len(shape)
    return pl.BlockSpec(shape, lambda *_: (0,) * nd)


def _in_proj_kernel(x_ref, g_ref, wmain_ref, wside_ref, fb_ref, qg_ref, wuq_ref, kvg_ref, wukv_ref,
                    cos_ref, sin_ref,
                    swaq_ref, swak_ref, swav_ref, foxq_ref, foxk_ref, foxv_ref, fcol_ref, frow_ref,
                    mlaq_ref, mlak_ref, mlav_ref, carry_ref):
    tm = x_ref.shape[0]
    h = _rms(x_ref[...], g_ref[...]).astype(BF16)

    main = jnp.dot(h, wmain_ref[...], preferred_element_type=F32)
    scale = HEAD_DIM ** -0.5
    swaq_ref[...] = (main[:, 0:512] * scale).astype(BF16)
    swak_ref[...] = main[:, 512:640].astype(BF16)
    swav_ref[...] = main[:, 640:768].astype(BF16)
    foxq_ref[...] = (main[:, 768:1024] * scale).astype(BF16)
    foxk_ref[...] = main[:, 1024:1280].astype(BF16)
    foxv_ref[...] = main[:, 1280:1536].astype(BF16)

    side = jnp.dot(h, wside_ref[...], preferred_element_type=F32)
    cos = cos_ref[...]
    sin = sin_ref[...]

    cqn = _rms(side[:, 0:256], qg_ref[...]).astype(BF16)
    qa = jnp.dot(cqn, wuq_ref[...], preferred_element_type=F32)
    for hh in range(MLA_HEADS):
        lo = hh * LANES
        a = qa[:, lo:lo + LANES]
        b = qa[:, 512 + lo:512 + lo + LANES]
        mlaq_ref[:, lo:lo + LANES] = (a * cos + b * sin).astype(BF16)

    ckvn = _rms(side[:, 256:384], kvg_ref[...]).astype(BF16)
    kv = jnp.dot(ckvn, wukv_ref[...], preferred_element_type=F32)
    krot = side[:, 384:512] * cos + side[:, 512:640] * sin
    for hh in range(MLA_HEADS):
        lo = hh * LANES
        mlak_ref[:, lo:lo + LANES] = (kv[:, lo:lo + LANES] + krot).astype(BF16)
    mlav_ref[...] = kv[:, 512:768].astype(BF16)

    lane = lax.broadcasted_iota(jnp.int32, (tm, LANES), 1)
    row = lax.broadcasted_iota(jnp.int32, (tm, LANES), 0)
    c = jnp.where(lane < FOX_HEADS, jax.nn.log_sigmoid(side[:, 640:768] + fb_ref[...]), 0.0)
    shift = 1
    while shift < tm:
        c = c + jnp.where(row >= shift, pltpu.roll(c, shift, axis=0), 0.0)
        shift *= 2

    @pl.when(pl.program_id(1) == 0)
    def _():
        carry_ref[...] = jnp.zeros_like(carry_ref)

    c = c + carry_ref[0:1, :]
    carry_ref[...] = jnp.broadcast_to(c[tm - 1:tm, :], carry_ref.shape)
    fcol_ref[...] = c
    frow_ref[0] = jnp.transpose(c)[0:8, :]


def _in_proj(x2, g, wmain, wside, fb, qg, wuq, kvg, wukv, cos_t, sin_t, batch, seq):
    T = x2.shape[0]
    tm = ROW_TILE
    ns = seq // tm
    row = lambda b, s: (b * ns + s, 0)
    seqrow = lambda b, s: (s, 0)
    out_shape = [
        jax.ShapeDtypeStruct((T, 512), BF16), jax.ShapeDtypeStruct((T, 128), BF16),
        jax.ShapeDtypeStruct((T, 128), BF16), jax.ShapeDtypeStruct((T, 256), BF16),
        jax.ShapeDtypeStruct((T, 256), BF16), jax.ShapeDtypeStruct((T, 256), BF16),
        jax.ShapeDtypeStruct((T, LANES), F32), jax.ShapeDtypeStruct((batch, 8, seq), F32),
        jax.ShapeDtypeStruct((T, 512), BF16), jax.ShapeDtypeStruct((T, 512), BF16),
        jax.ShapeDtypeStruct((T, 256), BF16),
    ]
    out_specs = [
        pl.BlockSpec((tm, 512), row), pl.BlockSpec((tm, 128), row), pl.BlockSpec((tm, 128), row),
        pl.BlockSpec((tm, 256), row), pl.BlockSpec((tm, 256), row), pl.BlockSpec((tm, 256), row),
        pl.BlockSpec((tm, LANES), row), pl.BlockSpec((1, 8, tm), lambda b, s: (b, 0, s)),
        pl.BlockSpec((tm, 512), row), pl.BlockSpec((tm, 512), row), pl.BlockSpec((tm, 256), row),
    ]
    in_specs = [
        pl.BlockSpec((tm, D_MODEL), row), _full(g.shape), _full(wmain.shape), _full(wside.shape),
        _full(fb.shape), _full(qg.shape), _full(wuq.shape), _full(kvg.shape), _full(wukv.shape),
        pl.BlockSpec((tm, LANES), seqrow), pl.BlockSpec((tm, LANES), seqrow),
    ]
    return pl.pallas_call(
        _in_proj_kernel,
        grid=(batch, ns),
        in_specs=in_specs,
        out_specs=out_specs,
        out_shape=out_shape,
        scratch_shapes=[pltpu.VMEM((8, LANES), F32)],
        compiler_params=_params("arbitrary", "arbitrary"),
        name="in_proj",
    )(x2, g, wmain, wside, fb, qg, wuq, kvg, wukv, cos_t, sin_t)


def _swa_kernel(q_ref, kc_ref, kp_ref, vc_ref, vp_ref, bias_ref, sink_ref, o_ref, kbuf, vbuf):
    first_tile = pl.program_id(1) == 0
    nwin = q_ref.shape[0] // WINDOW
    kbuf[0:WINDOW, :] = kp_ref[...]
    kbuf[WINDOW:, :] = kc_ref[...]
    vbuf[0:WINDOW, :] = vp_ref[...]
    vbuf[WINDOW:, :] = vc_ref[...]
    lane = lax.broadcasted_iota(jnp.int32, (WINDOW, LANES), 1)
    kj = lax.broadcasted_iota(jnp.int32, (1, 2 * WINDOW), 1)
    no_prev = jnp.where(jnp.logical_and(kj < WINDOW, first_tile), NEG_INF, 0.0)
    for w in range(nwin):
        kw = kbuf[w * WINDOW:(w + 2) * WINDOW, :]
        vw = vbuf[w * WINDOW:(w + 2) * WINDOW, :]
        rows = slice(w * WINDOW, (w + 1) * WINDOW)
        for p in range(SWA_Q_HEADS // 2):
            q2 = q_ref[rows, p * LANES:(p + 1) * LANES]
            outs = []
            for hh in range(2):
                head = 2 * p + hh
                mine = (lane < HEAD_DIM) if hh == 0 else (lane >= HEAD_DIM)
                qh = jnp.where(mine, q2, jnp.zeros_like(q2))
                s = lax.dot_general(qh, kw, NT_DIMS, preferred_element_type=F32)
                s = s + bias_ref[head]
                if w == 0:
                    s = s + no_prev
                sink = sink_ref[head:head + 1, 0:1]
                m = jnp.maximum(jnp.max(s, axis=1, keepdims=True), sink)
                e = jnp.exp(s - m)
                den = jnp.sum(e, axis=1, keepdims=True) + jnp.exp(sink - m)
                pv = jnp.dot(e.astype(BF16), vw, preferred_element_type=F32)
                outs.append(pv / den)
            o_ref[rows, p * LANES:(p + 1) * LANES] = jnp.where(lane < HEAD_DIM, outs[0], outs[1])


def _swa(q, k, v, bias, sinks, batch, seq):
    T = q.shape[0]
    tm = ROW_TILE
    ns = seq // tm
    per = tm // WINDOW
    row = lambda b, s: (b * ns + s, 0)
    prev = lambda b, s: (jnp.maximum((b * ns + s) * per - 1, 0), 0)
    return pl.pallas_call(
        _swa_kernel,
        grid=(batch, ns),
        in_specs=[
            pl.BlockSpec((tm, SWA_WIDTH), row),
            pl.BlockSpec((tm, LANES), row), pl.BlockSpec((WINDOW, LANES), prev),
            pl.BlockSpec((tm, LANES), row), pl.BlockSpec((WINDOW, LANES), prev),
            _full(bias.shape), _full(sinks.shape),
        ],
        out_specs=pl.BlockSpec((tm, SWA_WIDTH), row),
        out_shape=jax.ShapeDtypeStruct((T, SWA_WIDTH), F32),
        scratch_shapes=[pltpu.VMEM((tm + WINDOW, LANES), BF16), pltpu.VMEM((tm + WINDOW, LANES), BF16)],
        compiler_params=_params("parallel", "arbitrary"),
        name="swa_attn",
    )(q, k, k, v, v, bias, sinks)


def _online_softmax_step(s, v2, hh, m_sc, l_sc, acc_sc):
    m_prev = m_sc[hh]
    m_new = jnp.maximum(m_prev, jnp.max(s, axis=1, keepdims=True))
    alpha = jnp.exp(m_prev - m_new)
    p = jnp.exp(s - m_new)
    l_sc[hh] = alpha * l_sc[hh] + jnp.sum(p, axis=1, keepdims=True)
    acc_sc[hh] = alpha * acc_sc[hh] + jnp.dot(p.astype(BF16), v2, preferred_element_type=F32)
    m_sc[hh] = m_new


def _flash_init(m_sc, l_sc, acc_sc):
    m_sc[...] = jnp.full(m_sc.shape, NEG_INF, F32)
    l_sc[...] = jnp.zeros(l_sc.shape, F32)
    acc_sc[...] = jnp.zeros(acc_sc.shape, F32)


def _flash_finish(o_ref, l_sc, acc_sc):
    lane = lax.broadcasted_iota(jnp.int32, o_ref.shape, 1)
    o_ref[...] = jnp.where(lane < HEAD_DIM, acc_sc[0] / l_sc[0], acc_sc[1] / l_sc[1])


def _causal_mask(qi, ki, tq, tk):
    qpos = qi * tq + lax.broadcasted_iota(jnp.int32, (tq, tk), 0)
    kpos = ki * tk + lax.broadcasted_iota(jnp.int32, (tq, tk), 1)
    return kpos <= qpos


def _fox_kernel(qi_ref, ki_ref, q_ref, k_ref, v_ref, fcol_ref, frow_ref, o_ref, m_sc, l_sc, acc_sc, fq_sc):
    pair = pl.program_id(1)
    t = pl.program_id(2)
    qi = qi_ref[t]
    ki = ki_ref[t]
    tq, tk = q_ref.shape[0], k_ref.shape[0]

    @pl.when(ki == 0)
    def _():
        _flash_init(m_sc, l_sc, acc_sc)
        fc = fcol_ref[...]
        lane = lax.broadcasted_iota(jnp.int32, fc.shape, 1)
        for hh in range(2):
            fq_sc[hh] = jnp.sum(jnp.where(lane == 2 * pair + hh, fc, 0.0), axis=1, keepdims=True)

    q2 = q_ref[...]
    k2 = k_ref[...]
    v2 = v_ref[...]
    fr = frow_ref[0]
    head_row = lax.broadcasted_iota(jnp.int32, fr.shape, 0)
    lane = lax.broadcasted_iota(jnp.int32, q2.shape, 1)
    causal = _causal_mask(qi, ki, tq, tk)
    for hh in range(2):
        mine = (lane < HEAD_DIM) if hh == 0 else (lane >= HEAD_DIM)
        qh = jnp.where(mine, q2, jnp.zeros_like(q2))
        s = lax.dot_general(qh, k2, NT_DIMS, preferred_element_type=F32)
        fk = jnp.sum(jnp.where(head_row == 2 * pair + hh, fr, 0.0), axis=0, keepdims=True)
        s = (s + fq_sc[hh]) - fk
        s = jnp.where(causal, s, NEG_INF)
        _online_softmax_step(s, v2, hh, m_sc, l_sc, acc_sc)

    @pl.when(ki == qi)
    def _():
        _flash_finish(o_ref, l_sc, acc_sc)


def _mla_kernel(qi_ref, ki_ref, q_ref, k_ref, v_ref, o_ref, m_sc, l_sc, acc_sc):
    t = pl.program_id(2)
    qi = qi_ref[t]
    ki = ki_ref[t]
    tq, tk = q_ref.shape[0], k_ref.shape[0]

    @pl.when(ki == 0)
    def _():
        _flash_init(m_sc, l_sc, acc_sc)

    v2 = v_ref[...]
    causal = _causal_mask(qi, ki, tq, tk)
    for hh in range(2):
        qh = q_ref[:, hh * LANES:(hh + 1) * LANES]
        kh = k_ref[:, hh * LANES:(hh + 1) * LANES]
        s = lax.dot_general(qh, kh, NT_DIMS, preferred_element_type=F32) * (MLA_QK_DIM ** -0.5)
        s = jnp.where(causal, s, NEG_INF)
        _online_softmax_step(s, v2, hh, m_sc, l_sc, acc_sc)

    @pl.when(ki == qi)
    def _():
        _flash_finish(o_ref, l_sc, acc_sc)


def _tri_tables(nq):
    qi = [q for q in range(nq) for _ in range(q + 1)]
    ki = [k for q in range(nq) for k in range(q + 1)]
    return jnp.asarray(qi, jnp.int32), jnp.asarray(ki, jnp.int32)


def _flash_scratch(tq):
    return [pltpu.VMEM((2, tq, 1), F32), pltpu.VMEM((2, tq, 1), F32), pltpu.VMEM((2, tq, LANES), F32)]


def _fox(q, k, v, fcol, frow, batch, seq):
    T = q.shape[0]
    tq = ATTN_TILE
    nq = seq // tq
    qi_tab, ki_tab = _tri_tables(nq)
    qrow = lambda b, p, t, qi, ki: (b * nq + qi[t], p)
    krow = lambda b, p, t, qi, ki: (b * nq + ki[t], p)
    grid_spec = pltpu.PrefetchScalarGridSpec(
        num_scalar_prefetch=2,
        grid=(batch, FOX_HEADS // 2, qi_tab.shape[0]),
        in_specs=[
            pl.BlockSpec((tq, LANES), qrow), pl.BlockSpec((tq, LANES), krow), pl.BlockSpec((tq, LANES), krow),
            pl.BlockSpec((tq, LANES), lambda b, p, t, qi, ki: (b * nq + qi[t], 0)),
            pl.BlockSpec((1, 8, tq), lambda b, p, t, qi, ki: (b, 0, ki[t])),
        ],
        out_specs=pl.BlockSpec((tq, LANES), qrow),
        scratch_shapes=_flash_scratch(tq) + [pltpu.VMEM((2, tq, 1), F32)],
    )
    return pl.pallas_call(
        _fox_kernel,
        grid_spec=grid_spec,
        out_shape=jax.ShapeDtypeStruct((T, FOX_WIDTH), F32),
        compiler_params=_params("parallel", "parallel", "arbitrary"),
        name="fox_attn",
    )(qi_tab, ki_tab, q, k, v, fcol, frow)


def _mla(q, k, v, batch, seq):
    T = q.shape[0]
    tq = ATTN_TILE
    nq = seq // tq
    qi_tab, ki_tab = _tri_tables(nq)
    qrow = lambda b, p, t, qi, ki: (b * nq + qi[t], p)
    krow = lambda b, p, t, qi, ki: (b * nq + ki[t], p)
    grid_spec = pltpu.PrefetchScalarGridSpec(
        num_scalar_prefetch=2,
        grid=(batch, MLA_HEADS // 2, qi_tab.shape[0]),
        in_specs=[
            pl.BlockSpec((tq, 2 * LANES), qrow), pl.BlockSpec((tq, 2 * LANES), krow),
            pl.BlockSpec((tq, LANES), krow),
        ],
        out_specs=pl.BlockSpec((tq, LANES), qrow),
        scratch_shapes=_flash_scratch(tq),
    )
    return pl.pallas_call(
        _mla_kernel,
        grid_spec=grid_spec,
        out_shape=jax.ShapeDtypeStruct((T, MLA_WIDTH), F32),
        compiler_params=_params("parallel", "parallel", "arbitrary"),
        name="mla_attn",
    )(qi_tab, ki_tab, q, k, v)


def _out_proj_kernel(oa_ref, ob_ref, oc_ref, gg_ref, wout_ref, pg_ref, x_ref, o_ref):
    gg = gg_ref[...]
    a = _rms(oa_ref[...], gg[:, 0:SWA_WIDTH]).astype(BF16)
    b = _rms(ob_ref[...], gg[:, SWA_WIDTH:SWA_WIDTH + FOX_WIDTH]).astype(BF16)
    c = _rms(oc_ref[...], gg[:, SWA_WIDTH + FOX_WIDTH:]).astype(BF16)
    y = jnp.dot(a, wout_ref[0:SWA_WIDTH, :], preferred_element_type=F32)
    y = y + jnp.dot(b, wout_ref[SWA_WIDTH:SWA_WIDTH + FOX_WIDTH, :], preferred_element_type=F32)
    y = y + jnp.dot(c, wout_ref[SWA_WIDTH + FOX_WIDTH:, :], preferred_element_type=F32)
    o_ref[...] = x_ref[...] + _rms(y, pg_ref[...])


def _out_proj(oa, ob, oc, gg, wout, pg, x2):
    T = x2.shape[0]
    tm = ROW_TILE
    row = lambda i: (i, 0)
    return pl.pallas_call(
        _out_proj_kernel,
        grid=(T // tm,),
        in_specs=[
            pl.BlockSpec((tm, SWA_WIDTH), row), pl.BlockSpec((tm, FOX_WIDTH), row),
            pl.BlockSpec((tm, MLA_WIDTH), row), _full(gg.shape), _full(wout.shape), _full(pg.shape),
            pl.BlockSpec((tm, D_MODEL), row),
        ],
        out_specs=pl.BlockSpec((tm, D_MODEL), row),
        out_shape=jax.ShapeDtypeStruct((T, D_MODEL), F32),
        compiler_params=_params("parallel"),
        name="out_proj",
    )(oa, ob, oc, gg, wout, pg, x2)


def _ffn_kernel(x_ref, g_ref, wup_ref, cw_ref, cb_ref, wdown_ref, pg_ref, o_ref, acc_ref, carry_ref):
    tm = x_ref.shape[0]
    x = x_ref[...]
    h = _rms(x, g_ref[...]).astype(BF16)

    @pl.when(pl.program_id(1) == 0)
    def _():
        carry_ref[...] = jnp.zeros_like(carry_ref)

    acc_ref[...] = jnp.zeros_like(acc_ref)
    row8 = lax.broadcasted_iota(jnp.int32, (8, 2 * FF_CHUNK), 0)

    def chunk(c, _):
        u = jnp.dot(h, wup_ref[c], preferred_element_type=F32)
        prev = carry_ref[c]
        carry_ref[c] = u[tm - 8:tm, :]
        w = cw_ref[c]
        y = cb_ref[c] + w[2:3, :] * u
        for d in (1, 2):
            r = pltpu.roll(u, d, axis=0)
            head = jnp.where(row8 < d, pltpu.roll(prev, d, axis=0), r[0:8, :])
            r = jnp.concatenate([head, r[8:, :]], axis=0)
            y = y + w[2 - d:3 - d, :] * r
        act = (jax.nn.gelu(y[:, 0:FF_CHUNK], approximate=True) * y[:, FF_CHUNK:]).astype(BF16)
        acc_ref[...] += jnp.dot(act, wdown_ref[c], preferred_element_type=F32)
        return 0

    lax.fori_loop(0, N_FF_CHUNKS, chunk, 0)
    o_ref[...] = x + _rms(acc_ref[...], pg_ref[...])


def _ffn(x2, g, wup, cw, cb, wdown, pg, batch, seq):
    T = x2.shape[0]
    tm = ROW_TILE
    ns = seq // tm
    row = lambda b, s: (b * ns + s, 0)
    return pl.pallas_call(
        _ffn_kernel,
        grid=(batch, ns),
        in_specs=[
            pl.BlockSpec((tm, D_MODEL), row), _full(g.shape), _full(wup.shape), _full(cw.shape),
            _full(cb.shape), _full(wdown.shape), _full(pg.shape),
        ],
        out_specs=pl.BlockSpec((tm, D_MODEL), row),
        out_shape=jax.ShapeDtypeStruct((T, D_MODEL), F32),
        scratch_shapes=[pltpu.VMEM((tm, D_MODEL), F32), pltpu.VMEM((N_FF_CHUNKS, 8, 2 * FF_CHUNK), F32)],
        compiler_params=_params("parallel", "arbitrary"),
        name="conv_ffn",
    )(x2, g, wup, cw, cb, wdown, pg)


def _t5_causal_bucket(dist):
    max_exact = REL_BUCKETS // 2
    d = jnp.maximum(dist, 0)
    log_ratio = jnp.log(jnp.maximum(d, 1).astype(F32) / max_exact) / math.log(REL_MAX_DIST / max_exact)
    large = max_exact + (log_ratio * (REL_BUCKETS - max_exact)).astype(jnp.int32)
    large = jnp.minimum(large, REL_BUCKETS - 1)
    return jnp.where(d < max_exact, d, large)


def _swa_bias_table(rel_bias):
    qi = jnp.arange(WINDOW, dtype=jnp.int32)[:, None] + WINDOW
    kj = jnp.arange(2 * WINDOW, dtype=jnp.int32)[None, :]
    dist = qi - kj
    in_band = (dist >= 0) & (dist < WINDOW)
    bias = rel_bias.astype(F32)[_t5_causal_bucket(dist)]
    bias = jnp.where(in_band[:, :, None], bias, NEG_INF).transpose(2, 0, 1)
    return bias[jnp.asarray(SWA_HEAD_ORDER)]


def _rope_tables(seq):
    pos = jnp.arange(seq, dtype=F32)
    inv_freq = ROPE_THETA ** (-(jnp.arange(MLA_ROPE_DIM // 2, dtype=F32) * 2.0 / MLA_ROPE_DIM))
    ang = pos[:, None] * inv_freq[None, :]
    cos, sin = jnp.cos(ang), jnp.sin(ang)
    pad = jnp.zeros((seq, LANES - MLA_QK_DIM), F32)
    cos_t = jnp.concatenate([jnp.ones((seq, MLA_NOPE_DIM), F32), cos, cos, pad], axis=1)
    sin_t = jnp.concatenate([jnp.zeros((seq, MLA_NOPE_DIM), F32), -sin, sin, pad], axis=1)
    return cos_t, sin_t


def _swap_halves(w):
    half = w.shape[1] // 2
    return jnp.concatenate([w[:, half:], w[:, :half]], axis=1)


def _layer_params(l, w_in, forget_bias, w_uq, w_ukv, group_norm, w_out, w_up, conv_w, conv_b, w_down):
    wi = w_in[l]
    d = wi.shape[0]
    order = jnp.asarray(SWA_HEAD_ORDER)
    swa_q = wi[:, 0:SWA_WIDTH].reshape(d, SWA_Q_HEADS, HEAD_DIM)[:, order].reshape(d, SWA_WIDTH)
    wmain = jnp.concatenate([swa_q, wi[:, SWA_WIDTH:SWA_COLS + 3 * FOX_WIDTH]], axis=1).astype(BF16)

    f0 = SWA_COLS + 3 * FOX_WIDTH
    m0 = SWA_COLS + FOX_COLS
    z = lambda n: jnp.zeros((d, n), F32)
    w_kr = wi[:, m0 + MLA_Q_RANK + MLA_KV_RANK:]
    rope_pad = LANES - MLA_QK_DIM
    wside = jnp.concatenate([
        wi[:, m0:m0 + MLA_Q_RANK + MLA_KV_RANK],
        z(MLA_NOPE_DIM), w_kr, z(rope_pad),
        z(MLA_NOPE_DIM), _swap_halves(w_kr), z(rope_pad),
        wi[:, f0:f0 + FOX_HEADS], z(LANES - FOX_HEADS),
    ], axis=1).astype(BF16)
    fb = jnp.pad(forget_bias[l], (0, LANES - FOX_HEADS))[None, :]

    uq = w_uq[l].reshape(MLA_Q_RANK, MLA_HEADS, MLA_QK_DIM)
    zq = jnp.zeros((MLA_Q_RANK, MLA_HEADS, 1), F32)
    plain = jnp.concatenate([uq, jnp.tile(zq, (1, 1, rope_pad))], axis=2)
    rope = uq[:, :, MLA_NOPE_DIM:]
    swapped = jnp.concatenate([jnp.tile(zq, (1, 1, MLA_NOPE_DIM)), rope[:, :, MLA_ROPE_DIM // 2:],
                               rope[:, :, :MLA_ROPE_DIM // 2], jnp.tile(zq, (1, 1, rope_pad))], axis=2)
    wuq = jnp.concatenate([plain.reshape(MLA_Q_RANK, -1), swapped.reshape(MLA_Q_RANK, -1)], axis=1).astype(BF16)

    ukv = w_ukv[l].reshape(MLA_KV_RANK, MLA_HEADS, MLA_NOPE_DIM + MLA_V_DIM)
    k_slabs = jnp.concatenate([ukv[:, :, :MLA_NOPE_DIM],
                               jnp.zeros((MLA_KV_RANK, MLA_HEADS, LANES - MLA_NOPE_DIM), F32)], axis=2)
    wukv = jnp.concatenate([k_slabs.reshape(MLA_KV_RANK, -1),
                            ukv[:, :, MLA_NOPE_DIM:].reshape(MLA_KV_RANK, -1)], axis=1).astype(BF16)

    gg = group_norm[l]
    gg = jnp.concatenate([gg[:SWA_WIDTH].reshape(SWA_Q_HEADS, HEAD_DIM)[order].reshape(-1), gg[SWA_WIDTH:]])[None, :]
    wo = w_out[l]
    wo = jnp.concatenate([wo[:SWA_WIDTH].reshape(SWA_Q_HEADS, HEAD_DIM, -1)[order].reshape(SWA_WIDTH, -1),
                          wo[SWA_WIDTH:]], axis=0).astype(BF16)

    def gate_up(a):
        lead = a.shape[:-1]
        a = a.reshape(lead + (2, N_FF_CHUNKS, FF_CHUNK))
        a = jnp.moveaxis(a, -2, 0)
        return a.reshape((N_FF_CHUNKS,) + lead + (2 * FF_CHUNK,))

    wup = gate_up(w_up[l]).astype(BF16)
    cw = gate_up(conv_w[l])
    cb = gate_up(conv_b[l][None, :])
    wdown = w_down[l].reshape(N_FF_CHUNKS, FF_CHUNK, -1).astype(BF16)
    return wmain, wside, fb, wuq, wukv, gg, wo, wup, cw, cb, wdown


def kernel(x, attn_pre_norm, w_in, forget_bias, swa_sinks, rel_bias, q_latent_norm, w_uq, kv_latent_norm, w_ukv,
           group_norm, w_out, attn_post_norm, ffn_pre_norm, w_up, conv_w, conv_b, w_down, ffn_post_norm):
    batch, seq, d = x.shape
    assert d == D_MODEL and seq % ROW_TILE == 0 and seq % ATTN_TILE == 0
    depth = w_in.shape[0]
    cos_t, sin_t = _rope_tables(seq)
    bias = _swa_bias_table(rel_bias)
    order = jnp.asarray(SWA_HEAD_ORDER)
    x2 = x.reshape(batch * seq, d)
    for l in range(depth):
        wmain, wside, fb, wuq, wukv, gg, wo, wup, cw, cb, wdown = _layer_params(
            l, w_in, forget_bias, w_uq, w_ukv, group_norm, w_out, w_up, conv_w, conv_b, w_down)
        sinks = jnp.broadcast_to(swa_sinks[l][order][:, None], (SWA_Q_HEADS, LANES)).astype(F32)
        (swa_q, swa_k, swa_v, fox_q, fox_k, fox_v, fcol, frow, mla_q, mla_k, mla_v) = _in_proj(
            x2, attn_pre_norm[l][None, :], wmain, wside, fb, q_latent_norm[l][None, :], wuq,
            kv_latent_norm[l][None, :], wukv, cos_t, sin_t, batch, seq)
        out_a = _swa(swa_q, swa_k, swa_v, bias, sinks, batch, seq)
        out_b = _fox(fox_q, fox_k, fox_v, fcol, frow, batch, seq)
        out_c = _mla(mla_q, mla_k, mla_v, batch, seq)
        x2 = _out_proj(out_a, out_b, out_c, gg, wo, attn_post_norm[l][None, :], x2)
        x2 = _ffn(x2, ffn_pre_norm[l][None, :], wup, cw, cb, wdown, ffn_post_norm[l][None, :], batch, seq)
    return x2.reshape(batch, seq, d)
```

```python
import functools
import math

import jax
import jax.numpy as jnp
from jax import lax
from jax.experimental import pallas as pl
from jax.experimental.pallas import tpu as pltpu

D_MODEL = 1024
HEAD_DIM = 64
SWA_Q_HEADS = 8
SWA_KV_HEADS = 2
WINDOW = 128
FOX_HEADS = 4
MLA_HEADS = 4
MLA_Q_RANK = 256
MLA_KV_RANK = 128
MLA_NOPE_DIM = 64
MLA_ROPE_DIM = 32
MLA_V_DIM = 64
MLA_QK_DIM = MLA_NOPE_DIM + MLA_ROPE_DIM
ROPE_THETA = 10000.0
REL_BUCKETS = 32
REL_MAX_DIST = 128
D_FF = 2816
EPS = 1e-6
NEG_INF = -1e30

SWA_WIDTH = SWA_Q_HEADS * HEAD_DIM
FOX_WIDTH = FOX_HEADS * HEAD_DIM
MLA_WIDTH = MLA_HEADS * MLA_V_DIM
SWA_COLS = (SWA_Q_HEADS + 2 * SWA_KV_HEADS) * HEAD_DIM
FOX_COLS = 3 * FOX_WIDTH + FOX_HEADS

LANES = 128
ROW_TILE = 512
ATTN_TILE = 512
FF_CHUNK = 256
N_FF_CHUNKS = D_FF // FF_CHUNK
VMEM_LIMIT = 56 * 1024 * 1024

SWA_HEAD_ORDER = (0, 4, 1, 5, 2, 6, 3, 7)

F32 = jnp.float32
BF16 = jnp.bfloat16
NT_DIMS = (((1,), (1,)), ((), ()))


def _rms(x, g):
    ms = jnp.mean(x * x, axis=-1, keepdims=True)
    return x * lax.rsqrt(ms + EPS) * g


def _params(*sem):
    return pltpu.CompilerParams(dimension_semantics=sem, vmem_limit_bytes=VMEM_LIMIT)


def _full(shape):
    nd = len(shape)
    return pl.BlockSpec(shape, lambda *_: (0,) * nd)


def _in_proj_kernel(x_ref, g_ref, wmain_ref, wside_ref, fb_ref, qg_ref, wuq_ref, kvg_ref, wukv_ref,
                    cos_ref, sin_ref,
                    swaq_ref, swak_ref, swav_ref, foxq_ref, foxk_ref, foxv_ref, fcol_ref, frow_ref,
                    mlaq_ref, mlak_ref, mlav_ref, carry_ref):
    tm = x_ref.shape[0]
    h = _rms(x_ref[...], g_ref[...]).astype(BF16)

    main = jnp.dot(h, wmain_ref[...], preferred_element_type=F32)
    scale = HEAD_DIM ** -0.5
    swaq_ref[...] = (main[:, 0:512] * scale).astype(BF16)
    swak_ref[...] = main[:, 512:640].astype(BF16)
    swav_ref[...] = main[:, 640:768].astype(BF16)
    foxq_ref[...] = (main[:, 768:1024] * scale).astype(BF16)
    foxk_ref[...] = main[:, 1024:1280].astype(BF16)
    foxv_ref[...] = main[:, 1280:1536].astype(BF16)

    side = jnp.dot(h, wside_ref[...], preferred_element_type=F32)
    cos = cos_ref[...]
    sin = sin_ref[...]

    cqn = _rms(side[:, 0:256], qg_ref[...]).astype(BF16)
    qa = jnp.dot(cqn, wuq_ref[...], preferred_element_type=F32)
    for hh in range(MLA_HEADS):
        lo = hh * LANES
        a = qa[:, lo:lo + LANES]
        b = qa[:, 512 + lo:512 + lo + LANES]
        mlaq_ref[:, lo:lo + LANES] = (a * cos + b * sin).astype(BF16)

    ckvn = _rms(side[:, 256:384], kvg_ref[...]).astype(BF16)
    kv = jnp.dot(ckvn, wukv_ref[...], preferred_element_type=F32)
    krot = side[:, 384:512] * cos + side[:, 512:640] * sin
    for hh in range(MLA_HEADS):
        lo = hh * LANES
        mlak_ref[:, lo:lo + LANES] = (kv[:, lo:lo + LANES] + krot).astype(BF16)
    mlav_ref[...] = kv[:, 512:768].astype(BF16)

    lane = lax.broadcasted_iota(jnp.int32, (tm, LANES), 1)
    row = lax.broadcasted_iota(jnp.int32, (tm, LANES), 0)
    c = jnp.where(lane < FOX_HEADS, jax.nn.log_sigmoid(side[:, 640:768] + fb_ref[...]), 0.0)
    shift = 1
    while shift < tm:
        c = c + jnp.where(row >= shift, pltpu.roll(c, shift, axis=0), 0.0)
        shift *= 2

    @pl.when(pl.program_id(1) == 0)
    def _():
        carry_ref[...] = jnp.zeros_like(carry_ref)

    c = c + carry_ref[0:1, :]
    carry_ref[...] = jnp.broadcast_to(c[tm - 1:tm, :], carry_ref.shape)
    fcol_ref[...] = c
    frow_ref[0] = jnp.transpose(c)[0:8, :]


def _in_proj(x2, g, wmain, wside, fb, qg, wuq, kvg, wukv, cos_t, sin_t, batch, seq):
    T = x2.shape[0]
    tm = ROW_TILE
    ns = seq // tm
    row = lambda b, s: (b * ns + s, 0)
    seqrow = lambda b, s: (s, 0)
    out_shape = [
        jax.ShapeDtypeStruct((T, 512), BF16), jax.ShapeDtypeStruct((T, 128), BF16),
        jax.ShapeDtypeStruct((T, 128), BF16), jax.ShapeDtypeStruct((T, 256), BF16),
        jax.ShapeDtypeStruct((T, 256), BF16), jax.ShapeDtypeStruct((T, 256), BF16),
        jax.ShapeDtypeStruct((T, LANES), F32), jax.ShapeDtypeStruct((batch, 8, seq), F32),
        jax.ShapeDtypeStruct((T, 512), BF16), jax.ShapeDtypeStruct((T, 512), BF16),
        jax.ShapeDtypeStruct((T, 256), BF16),
    ]
    out_specs = [
        pl.BlockSpec((tm, 512), row), pl.BlockSpec((tm, 128), row), pl.BlockSpec((tm, 128), row),
        pl.BlockSpec((tm, 256), row), pl.BlockSpec((tm, 256), row), pl.BlockSpec((tm, 256), row),
        pl.BlockSpec((tm, LANES), row), pl.BlockSpec((1, 8, tm), lambda b, s: (b, 0, s)),
        pl.BlockSpec((tm, 512), row), pl.BlockSpec((tm, 512), row), pl.BlockSpec((tm, 256), row),
    ]
    in_specs = [
        pl.BlockSpec((tm, D_MODEL), row), _full(g.shape), _full(wmain.shape), _full(wside.shape),
        _full(fb.shape), _full(qg.shape), _full(wuq.shape), _full(kvg.shape), _full(wukv.shape),
        pl.BlockSpec((tm, LANES), seqrow), pl.BlockSpec((tm, LANES), seqrow),
    ]
    return pl.pallas_call(
        _in_proj_kernel,
        grid=(batch, ns),
        in_specs=in_specs,
        out_specs=out_specs,
        out_shape=out_shape,
        scratch_shapes=[pltpu.VMEM((8, LANES), F32)],
        compiler_params=_params("arbitrary", "arbitrary"),
        name="in_proj",
    )(x2, g, wmain, wside, fb, qg, wuq, kvg, wukv, cos_t, sin_t)


def _half_lanes(shape, hh):
    lane = lax.broadcasted_iota(jnp.int32, shape, len(shape) - 1)
    return (lane < HEAD_DIM) if hh == 0 else (lane >= HEAD_DIM)


def _ones_in_other_half(v2, hh):
    return jnp.where(_half_lanes(v2.shape, hh), v2, jnp.ones_like(v2))


def _swa_kernel(q_ref, kc_ref, kp_ref, vc_ref, vp_ref, bias_ref, sink_ref, o_ref, kbuf, vbuf):
    first_tile = pl.program_id(1) == 0
    nwin = q_ref.shape[0] // WINDOW
    npair = SWA_Q_HEADS // 2
    kbuf[0:WINDOW, :] = kp_ref[...]
    kbuf[WINDOW:, :] = kc_ref[...]
    vbuf[0:WINDOW, :] = vp_ref[...]
    vbuf[WINDOW:, :] = vc_ref[...]
    kj = lax.broadcasted_iota(jnp.int32, (1, 2 * WINDOW), 1)
    no_prev = jnp.where(jnp.logical_and(kj < WINDOW, first_tile), NEG_INF, 0.0)
    low = _half_lanes((WINDOW, LANES), 0)
    for w in range(nwin):
        kw = kbuf[w * WINDOW:(w + 2) * WINDOW, :]
        vw = vbuf[w * WINDOW:(w + 2) * WINDOW, :]
        rows = slice(w * WINDOW, (w + 1) * WINDOW)
        q4 = [q_ref[rows, p * LANES:(p + 1) * LANES] for p in range(npair)]
        outs = []
        for hh in range(2):
            mine = _half_lanes((WINDOW, LANES), hh)
            qs = jnp.concatenate([jnp.where(mine, q, jnp.zeros_like(q)) for q in q4], axis=0)
            s = lax.dot_general(qs, kw, NT_DIMS, preferred_element_type=F32) + bias_ref[hh]
            if w == 0:
                s = s + no_prev
            sink = sink_ref[hh]
            s0, s1 = s[:, 0:LANES], s[:, LANES:]
            mloc = jnp.max(jnp.maximum(s0, s1), axis=1, keepdims=True)
            m = jnp.maximum(jnp.broadcast_to(mloc, sink.shape), sink)
            e = jnp.concatenate([jnp.exp(s0 - m), jnp.exp(s1 - m)], axis=1).astype(BF16)
            pv = jnp.dot(e, _ones_in_other_half(vw, hh), preferred_element_type=F32)
            den = pltpu.roll(pv, HEAD_DIM, axis=1) + jnp.exp(sink - m)
            outs.append(pv / den)
        for p in range(npair):
            pr = slice(p * WINDOW, (p + 1) * WINDOW)
            o_ref[rows, p * LANES:(p + 1) * LANES] = jnp.where(low, outs[0][pr], outs[1][pr])


def _swa(q, k, v, bias, sinks, batch, seq):
    T = q.shape[0]
    tm = ROW_TILE
    ns = seq // tm
    per = tm // WINDOW
    row = lambda b, s: (b * ns + s, 0)
    prev = lambda b, s: (jnp.maximum((b * ns + s) * per - 1, 0), 0)
    return pl.pallas_call(
        _swa_kernel,
        grid=(batch, ns),
        in_specs=[
            pl.BlockSpec((tm, SWA_WIDTH), row),
            pl.BlockSpec((tm, LANES), row), pl.BlockSpec((WINDOW, LANES), prev),
            pl.BlockSpec((tm, LANES), row), pl.BlockSpec((WINDOW, LANES), prev),
            _full(bias.shape), _full(sinks.shape),
        ],
        out_specs=pl.BlockSpec((tm, SWA_WIDTH), row),
        out_shape=jax.ShapeDtypeStruct((T, SWA_WIDTH), F32),
        scratch_shapes=[pltpu.VMEM((tm + WINDOW, LANES), BF16), pltpu.VMEM((tm + WINDOW, LANES), BF16)],
        compiler_params=_params("parallel", "arbitrary"),
        name="swa_attn",
    )(q, k, k, v, v, bias, sinks)


def _flash_init(m_sc, acc_sc):
    m_sc[...] = jnp.full(m_sc.shape, NEG_INF, F32)
    acc_sc[...] = jnp.zeros(acc_sc.shape, F32)


def _flash_update(s_tiles, v2, hh, m_sc, acc_sc, row_shift=None, exp_scale=None):
    mloc = functools.reduce(jnp.maximum, s_tiles)
    mloc = jnp.broadcast_to(jnp.max(mloc, axis=1, keepdims=True), mloc.shape)
    if row_shift is not None:
        mloc = mloc + row_shift
    m_prev = m_sc[hh]
    m_new = jnp.maximum(m_prev, mloc)
    m_sc[hh] = m_new
    m_row = m_new if row_shift is None else m_new - row_shift
    if exp_scale is None:
        alpha = jnp.exp(m_prev - m_new)
        p = [jnp.exp(s - m_row) for s in s_tiles]
    else:
        c2 = exp_scale * math.log2(math.e)
        alpha = jnp.exp2((m_prev - m_new) * c2)
        p = [jnp.exp2((s - m_row) * c2) for s in s_tiles]
    p = jnp.concatenate(p, axis=1).astype(BF16)
    pv = jnp.dot(p, _ones_in_other_half(v2, hh), preferred_element_type=F32)
    acc_sc[hh] = alpha * acc_sc[hh] + pv


def _flash_finish(o_ref, acc_sc):
    outs = [acc_sc[hh] / pltpu.roll(acc_sc[hh], HEAD_DIM, axis=1) for hh in range(2)]
    o_ref[...] = jnp.where(_half_lanes(o_ref.shape, 0), outs[0], outs[1])


def _lane_tiles(s):
    return [s[:, j * LANES:(j + 1) * LANES] for j in range(s.shape[1] // LANES)]


def _below_diagonal(tq, tk):
    r = lax.broadcasted_iota(jnp.int32, (tq, LANES), 0)
    c = lax.broadcasted_iota(jnp.int32, (tq, LANES), 1)
    return [c + j * LANES <= r for j in range(tk // LANES)]


def _fox_kernel(qi_ref, ki_ref, q_ref, k_ref, v_ref, fcol_ref, frow_ref, o_ref, m_sc, acc_sc, fq_sc):
    pair = pl.program_id(1)
    t = pl.program_id(2)
    qi = qi_ref[t]
    ki = ki_ref[t]
    tq, tk = q_ref.shape[0], k_ref.shape[0]

    @pl.when(ki == 0)
    def _():
        _flash_init(m_sc, acc_sc)
        fc = fcol_ref[...]
        lane = lax.broadcasted_iota(jnp.int32, fc.shape, 1)
        for hh in range(2):
            col = jnp.sum(jnp.where(lane == 2 * pair + hh, fc, 0.0), axis=1, keepdims=True)
            fq_sc[hh] = jnp.broadcast_to(col, fc.shape)

    def step(diagonal):
        q2 = q_ref[...]
        k2 = k_ref[...]
        v2 = v_ref[...]
        fr = frow_ref[0]
        head_row = lax.broadcasted_iota(jnp.int32, fr.shape, 0)
        for hh in range(2):
            qh = jnp.where(_half_lanes(q2.shape, hh), q2, jnp.zeros_like(q2))
            s = lax.dot_general(qh, k2, NT_DIMS, preferred_element_type=F32)
            fk = jnp.sum(jnp.where(head_row == 2 * pair + hh, fr, 0.0), axis=0, keepdims=True)
            tiles = _lane_tiles(s - fk)
            if diagonal:
                tiles = [jnp.where(keep, x, NEG_INF) for keep, x in zip(_below_diagonal(tq, tk), tiles)]
            _flash_update(tiles, v2, hh, m_sc, acc_sc, row_shift=fq_sc[hh])

    @pl.when(ki < qi)
    def _():
        step(False)

    @pl.when(ki == qi)
    def _():
        step(True)
        _flash_finish(o_ref, acc_sc)


def _mla_kernel(qi_ref, ki_ref, q_ref, k_ref, v_ref, o_ref, m_sc, acc_sc):
    t = pl.program_id(2)
    qi = qi_ref[t]
    ki = ki_ref[t]
    tq, tk = q_ref.shape[0], k_ref.shape[0]

    @pl.when(ki == 0)
    def _():
        _flash_init(m_sc, acc_sc)

    def step(diagonal):
        v2 = v_ref[...]
        for hh in range(2):
            qh = q_ref[:, hh * LANES:(hh + 1) * LANES]
            kh = k_ref[:, hh * LANES:(hh + 1) * LANES]
            tiles = _lane_tiles(lax.dot_general(qh, kh, NT_DIMS, preferred_element_type=F32))
            if diagonal:
                tiles = [jnp.where(keep, x, NEG_INF) for keep, x in zip(_below_diagonal(tq, tk), tiles)]
            _flash_update(tiles, v2, hh, m_sc, acc_sc, exp_scale=MLA_QK_DIM ** -0.5)

    @pl.when(ki < qi)
    def _():
        step(False)

    @pl.when(ki == qi)
    def _():
        step(True)
        _flash_finish(o_ref, acc_sc)


def _tri_tables(nq):
    qi = [q for q in range(nq) for _ in range(q + 1)]
    ki = [k for q in range(nq) for k in range(q + 1)]
    return jnp.asarray(qi, jnp.int32), jnp.asarray(ki, jnp.int32)


def _flash_scratch(tq):
    return [pltpu.VMEM((2, tq, LANES), F32), pltpu.VMEM((2, tq, LANES), F32)]


def _fox(q, k, v, fcol, frow, batch, seq):
    T = q.shape[0]
    tq = ATTN_TILE
    nq = seq // tq
    qi_tab, ki_tab = _tri_tables(nq)
    qrow = lambda b, p, t, qi, ki: (b * nq + qi[t], p)
    krow = lambda b, p, t, qi, ki: (b * nq + ki[t], p)
    grid_spec = pltpu.PrefetchScalarGridSpec(
        num_scalar_prefetch=2,
        grid=(batch, FOX_HEADS // 2, qi_tab.shape[0]),
        in_specs=[
            pl.BlockSpec((tq, LANES), qrow), pl.BlockSpec((tq, LANES), krow), pl.BlockSpec((tq, LANES), krow),
            pl.BlockSpec((tq, LANES), lambda b, p, t, qi, ki: (b * nq + qi[t], 0)),
            pl.BlockSpec((1, 8, tq), lambda b, p, t, qi, ki: (b, 0, ki[t])),
        ],
        out_specs=pl.BlockSpec((tq, LANES), qrow),
        scratch_shapes=_flash_scratch(tq) + [pltpu.VMEM((2, tq, LANES), F32)],
    )
    return pl.pallas_call(
        _fox_kernel,
        grid_spec=grid_spec,
        out_shape=jax.ShapeDtypeStruct((T, FOX_WIDTH), F32),
        compiler_params=_params("parallel", "parallel", "arbitrary"),
        name="fox_attn",
    )(qi_tab, ki_tab, q, k, v, fcol, frow)


def _mla(q, k, v, batch, seq):
    T = q.shape[0]
    tq = ATTN_TILE
    nq = seq // tq
    qi_tab, ki_tab = _tri_tables(nq)
    qrow = lambda b, p, t, qi, ki: (b * nq + qi[t], p)
    krow = lambda b, p, t, qi, ki: (b * nq + ki[t], p)
    grid_spec = pltpu.PrefetchScalarGridSpec(
        num_scalar_prefetch=2,
        grid=(batch, MLA_HEADS // 2, qi_tab.shape[0]),
        in_specs=[
            pl.BlockSpec((tq, 2 * LANES), qrow), pl.BlockSpec((tq, 2 * LANES), krow),
            pl.BlockSpec((tq, LANES), krow),
        ],
        out_specs=pl.BlockSpec((tq, LANES), qrow),
        scratch_shapes=_flash_scratch(tq),
    )
    return pl.pallas_call(
        _mla_kernel,
        grid_spec=grid_spec,
        out_shape=jax.ShapeDtypeStruct((T, MLA_WIDTH), F32),
        compiler_params=_params("parallel", "parallel", "arbitrary"),
        name="mla_attn",
    )(qi_tab, ki_tab, q, k, v)


def _out_proj_kernel(oa_ref, ob_ref, oc_ref, gg_ref, wout_ref, pg_ref, x_ref, o_ref):
    gg = gg_ref[...]
    a = _rms(oa_ref[...], gg[:, 0:SWA_WIDTH]).astype(BF16)
    b = _rms(ob_ref[...], gg[:, SWA_WIDTH:SWA_WIDTH + FOX_WIDTH]).astype(BF16)
    c = _rms(oc_ref[...], gg[:, SWA_WIDTH + FOX_WIDTH:]).astype(BF16)
    y = jnp.dot(a, wout_ref[0:SWA_WIDTH, :], preferred_element_type=F32)
    y = y + jnp.dot(b, wout_ref[SWA_WIDTH:SWA_WIDTH + FOX_WIDTH, :], preferred_element_type=F32)
    y = y + jnp.dot(c, wout_ref[SWA_WIDTH + FOX_WIDTH:, :], preferred_element_type=F32)
    o_ref[...] = x_ref[...] + _rms(y, pg_ref[...])


def _out_proj(oa, ob, oc, gg, wout, pg, x2):
    T = x2.shape[0]
    tm = ROW_TILE
    row = lambda i: (i, 0)
    return pl.pallas_call(
        _out_proj_kernel,
        grid=(T // tm,),
        in_specs=[
            pl.BlockSpec((tm, SWA_WIDTH), row), pl.BlockSpec((tm, FOX_WIDTH), row),
            pl.BlockSpec((tm, MLA_WIDTH), row), _full(gg.shape), _full(wout.shape), _full(pg.shape),
            pl.BlockSpec((tm, D_MODEL), row),
        ],
        out_specs=pl.BlockSpec((tm, D_MODEL), row),
        out_shape=jax.ShapeDtypeStruct((T, D_MODEL), F32),
        compiler_params=_params("parallel"),
        name="out_proj",
    )(oa, ob, oc, gg, wout, pg, x2)


def _ffn_kernel(x_ref, g_ref, wup_ref, cw_ref, cb_ref, wdown_ref, pg_ref, o_ref,
                hbuf, ubuf, acc_ref, carry_ref):
    tm = x_ref.shape[0]
    hbuf[...] = _rms(x_ref[...], g_ref[...]).astype(BF16)

    @pl.when(pl.program_id(1) == 0)
    def _():
        carry_ref[...] = jnp.zeros_like(carry_ref)

    row8 = lax.broadcasted_iota(jnp.int32, (8, 2 * FF_CHUNK), 0)

    def up_proj(c, slot):
        ubuf[slot] = jnp.dot(hbuf[...], wup_ref[c], preferred_element_type=F32)

    def conv_act_down(c, slot, first=False):
        u = ubuf[slot]
        prev = carry_ref[c]
        carry_ref[c] = u[tm - 8:tm, :]
        w = cw_ref[c]
        y = cb_ref[c] + w[2:3, :] * u
        for d in (1, 2):
            r = pltpu.roll(u, d, axis=0)
            head = jnp.where(row8 < d, pltpu.roll(prev, d, axis=0), r[0:8, :])
            r = jnp.concatenate([head, r[8:, :]], axis=0)
            y = y + w[2 - d:3 - d, :] * r
        act = (jax.nn.gelu(y[:, 0:FF_CHUNK], approximate=True) * y[:, FF_CHUNK:]).astype(BF16)
        down = jnp.dot(act, wdown_ref[c], preferred_element_type=F32)
        if first:
            acc_ref[...] = down
        else:
            acc_ref[...] += down

    up_proj(0, 0)
    up_proj(1, 1)
    conv_act_down(0, 0, first=True)

    def two_chunks(j, _):
        c = 2 * j + 1
        up_proj(c + 1, 0)
        conv_act_down(c, 1)
        up_proj(c + 2, 1)
        conv_act_down(c + 1, 0)
        return 0

    lax.fori_loop(0, (N_FF_CHUNKS - 3) // 2, two_chunks, 0)
    up_proj(N_FF_CHUNKS - 1, 0)
    conv_act_down(N_FF_CHUNKS - 2, 1)
    conv_act_down(N_FF_CHUNKS - 1, 0)
    o_ref[...] = x_ref[...] + _rms(acc_ref[...], pg_ref[...])


def _ffn(x2, g, wup, cw, cb, wdown, pg, batch, seq):
    T = x2.shape[0]
    tm = ROW_TILE
    ns = seq // tm
    row = lambda b, s: (b * ns + s, 0)
    return pl.pallas_call(
        _ffn_kernel,
        grid=(batch, ns),
        in_specs=[
            pl.BlockSpec((tm, D_MODEL), row), _full(g.shape), _full(wup.shape), _full(cw.shape),
            _full(cb.shape), _full(wdown.shape), _full(pg.shape),
        ],
        out_specs=pl.BlockSpec((tm, D_MODEL), row),
        out_shape=jax.ShapeDtypeStruct((T, D_MODEL), F32),
        scratch_shapes=[
            pltpu.VMEM((tm, D_MODEL), BF16), pltpu.VMEM((2, tm, 2 * FF_CHUNK), F32),
            pltpu.VMEM((tm, D_MODEL), F32), pltpu.VMEM((N_FF_CHUNKS, 8, 2 * FF_CHUNK), F32),
        ],
        compiler_params=_params("parallel", "arbitrary"),
        name="conv_ffn",
    )(x2, g, wup, cw, cb, wdown, pg)


def _t5_causal_bucket(dist):
    max_exact = REL_BUCKETS // 2
    d = jnp.maximum(dist, 0)
    log_ratio = jnp.log(jnp.maximum(d, 1).astype(F32) / max_exact) / math.log(REL_MAX_DIST / max_exact)
    large = max_exact + (log_ratio * (REL_BUCKETS - max_exact)).astype(jnp.int32)
    large = jnp.minimum(large, REL_BUCKETS - 1)
    return jnp.where(d < max_exact, d, large)


def _swa_bias_table(rel_bias):
    qi = jnp.arange(WINDOW, dtype=jnp.int32)[:, None] + WINDOW
    kj = jnp.arange(2 * WINDOW, dtype=jnp.int32)[None, :]
    dist = qi - kj
    in_band = (dist >= 0) & (dist < WINDOW)
    bias = rel_bias.astype(F32)[_t5_causal_bucket(dist)]
    bias = jnp.where(in_band[:, :, None], bias, NEG_INF).transpose(2, 0, 1)
    return bias.reshape(SWA_KV_HEADS, (SWA_Q_HEADS // SWA_KV_HEADS) * WINDOW, 2 * WINDOW)


def _rope_tables(seq):
    pos = jnp.arange(seq, dtype=F32)
    inv_freq = ROPE_THETA ** (-(jnp.arange(MLA_ROPE_DIM // 2, dtype=F32) * 2.0 / MLA_ROPE_DIM))
    ang = pos[:, None] * inv_freq[None, :]
    cos, sin = jnp.cos(ang), jnp.sin(ang)
    pad = jnp.zeros((seq, LANES - MLA_QK_DIM), F32)
    cos_t = jnp.concatenate([jnp.ones((seq, MLA_NOPE_DIM), F32), cos, cos, pad], axis=1)
    sin_t = jnp.concatenate([jnp.zeros((seq, MLA_NOPE_DIM), F32), -sin, sin, pad], axis=1)
    return cos_t, sin_t


def _swap_halves(w):
    half = w.shape[1] // 2
    return jnp.concatenate([w[:, half:], w[:, :half]], axis=1)


def _layer_params(l, w_in, forget_bias, w_uq, w_ukv, group_norm, w_out, w_up, conv_w, conv_b, w_down):
    wi = w_in[l]
    d = wi.shape[0]
    order = jnp.asarray(SWA_HEAD_ORDER)
    swa_q = wi[:, 0:SWA_WIDTH].reshape(d, SWA_Q_HEADS, HEAD_DIM)[:, order].reshape(d, SWA_WIDTH)
    wmain = jnp.concatenate([swa_q, wi[:, SWA_WIDTH:SWA_COLS + 3 * FOX_WIDTH]], axis=1).astype(BF16)

    f0 = SWA_COLS + 3 * FOX_WIDTH
    m0 = SWA_COLS + FOX_COLS
    z = lambda n: jnp.zeros((d, n), F32)
    w_kr = wi[:, m0 + MLA_Q_RANK + MLA_KV_RANK:]
    rope_pad = LANES - MLA_QK_DIM
    wside = jnp.concatenate([
        wi[:, m0:m0 + MLA_Q_RANK + MLA_KV_RANK],
        z(MLA_NOPE_DIM), w_kr, z(rope_pad),
        z(MLA_NOPE_DIM), _swap_halves(w_kr), z(rope_pad),
        wi[:, f0:f0 + FOX_HEADS], z(LANES - FOX_HEADS),
    ], axis=1).astype(BF16)
    fb = jnp.pad(forget_bias[l], (0, LANES - FOX_HEADS))[None, :]

    uq = w_uq[l].reshape(MLA_Q_RANK, MLA_HEADS, MLA_QK_DIM)
    zq = jnp.zeros((MLA_Q_RANK, MLA_HEADS, 1), F32)
    plain = jnp.concatenate([uq, jnp.tile(zq, (1, 1, rope_pad))], axis=2)
    rope = uq[:, :, MLA_NOPE_DIM:]
    swapped = jnp.concatenate([jnp.tile(zq, (1, 1, MLA_NOPE_DIM)), rope[:, :, MLA_ROPE_DIM // 2:],
                               rope[:, :, :MLA_ROPE_DIM // 2], jnp.tile(zq, (1, 1, rope_pad))], axis=2)
    wuq = jnp.concatenate([plain.reshape(MLA_Q_RANK, -1), swapped.reshape(MLA_Q_RANK, -1)], axis=1).astype(BF16)

    ukv = w_ukv[l].reshape(MLA_KV_RANK, MLA_HEADS, MLA_NOPE_DIM + MLA_V_DIM)
    k_slabs = jnp.concatenate([ukv[:, :, :MLA_NOPE_DIM],
                               jnp.zeros((MLA_KV_RANK, MLA_HEADS, LANES - MLA_NOPE_DIM), F32)], axis=2)
    wukv = jnp.concatenate([k_slabs.reshape(MLA_KV_RANK, -1),
                            ukv[:, :, MLA_NOPE_DIM:].reshape(MLA_KV_RANK, -1)], axis=1).astype(BF16)

    gg = group_norm[l]
    gg = jnp.concatenate([gg[:SWA_WIDTH].reshape(SWA_Q_HEADS, HEAD_DIM)[order].reshape(-1), gg[SWA_WIDTH:]])[None, :]
    wo = w_out[l]
    wo = jnp.concatenate([wo[:SWA_WIDTH].reshape(SWA_Q_HEADS, HEAD_DIM, -1)[order].reshape(SWA_WIDTH, -1),
                          wo[SWA_WIDTH:]], axis=0).astype(BF16)

    def gate_up(a):
        lead = a.shape[:-1]
        a = a.reshape(lead + (2, N_FF_CHUNKS, FF_CHUNK))
        a = jnp.moveaxis(a, -2, 0)
        return a.reshape((N_FF_CHUNKS,) + lead + (2 * FF_CHUNK,))

    wup = gate_up(w_up[l]).astype(BF16)
    cw = gate_up(conv_w[l])
    cb = gate_up(conv_b[l][None, :])
    wdown = w_down[l].reshape(N_FF_CHUNKS, FF_CHUNK, -1).astype(BF16)
    return wmain, wside, fb, wuq, wukv, gg, wo, wup, cw, cb, wdown


def kernel(x, attn_pre_norm, w_in, forget_bias, swa_sinks, rel_bias, q_latent_norm, w_uq, kv_latent_norm, w_ukv,
           group_norm, w_out, attn_post_norm, ffn_pre_norm, w_up, conv_w, conv_b, w_down, ffn_post_norm):
    batch, seq, d = x.shape
    assert d == D_MODEL and seq % ROW_TILE == 0 and seq % ATTN_TILE == 0
    depth = w_in.shape[0]
    cos_t, sin_t = _rope_tables(seq)
    bias = _swa_bias_table(rel_bias)
    x2 = x.reshape(batch * seq, d)
    for l in range(depth):
        wmain, wside, fb, wuq, wukv, gg, wo, wup, cw, cb, wdown = _layer_params(
            l, w_in, forget_bias, w_uq, w_ukv, group_norm, w_out, w_up, conv_w, conv_b, w_down)
        sinks = jnp.repeat(swa_sinks[l].astype(F32).reshape(SWA_KV_HEADS, -1), WINDOW, axis=1)
        sinks = jnp.broadcast_to(sinks[:, :, None], sinks.shape + (LANES,))
        (swa_q, swa_k, swa_v, fox_q, fox_k, fox_v, fcol, frow, mla_q, mla_k, mla_v) = _in_proj(
            x2, attn_pre_norm[l][None, :], wmain, wside, fb, q_latent_norm[l][None, :], wuq,
            kv_latent_norm[l][None, :], wukv, cos_t, sin_t, batch, seq)
        out_a = _swa(swa_q, swa_k, swa_v, bias, sinks, batch, seq)
        out_b = _fox(fox_q, fox_k, fox_v, fcol, frow, batch, seq)
        out_c = _mla(mla_q, mla_k, mla_v, batch, seq)
        x2 = _out_proj(out_a, out_b, out_c, gg, wo, attn_post_norm[l][None, :], x2)
        x2 = _ffn(x2, ffn_pre_norm[l][None, :], wup, cw, cb, wdown, ffn_post_norm[l][None, :], batch, seq)
    return x2.reshape(batch, seq, d)
```

```python
import functools
import math

import jax
import jax.numpy as jnp
from jax import lax
from jax.experimental import pallas as pl
from jax.experimental.pallas import tpu as pltpu

D_MODEL = 1024
HEAD_DIM = 64
SWA_Q_HEADS = 8
SWA_KV_HEADS = 2
WINDOW = 128
FOX_HEADS = 4
MLA_HEADS = 4
MLA_Q_RANK = 256
MLA_KV_RANK = 128
MLA_NOPE_DIM = 64
MLA_ROPE_DIM = 32
MLA_V_DIM = 64
MLA_QK_DIM = MLA_NOPE_DIM + MLA_ROPE_DIM
ROPE_THETA = 10000.0
REL_BUCKETS = 32
REL_MAX_DIST = 128
D_FF = 2816
EPS = 1e-6
NEG_INF = -1e30

SWA_WIDTH = SWA_Q_HEADS * HEAD_DIM
FOX_WIDTH = FOX_HEADS * HEAD_DIM
MLA_WIDTH = MLA_HEADS * MLA_V_DIM
SWA_COLS = (SWA_Q_HEADS + 2 * SWA_KV_HEADS) * HEAD_DIM
FOX_COLS = 3 * FOX_WIDTH + FOX_HEADS

LANES = 128
ROW_TILE = 512
ATTN_TILE = 512
FF_CHUNK = 256
N_FF_CHUNKS = D_FF // FF_CHUNK
VMEM_LIMIT = 56 * 1024 * 1024

SWA_HEAD_ORDER = (0, 4, 1, 5, 2, 6, 3, 7)

F32 = jnp.float32
BF16 = jnp.bfloat16
NT_DIMS = (((1,), (1,)), ((), ()))


def _rms(x, g):
    ms = jnp.mean(x * x, axis=-1, keepdims=True)
    return x * lax.rsqrt(ms + EPS) * g


def _params(*sem):
    return pltpu.CompilerParams(dimension_semantics=sem, vmem_limit_bytes=VMEM_LIMIT)


def _full(shape):
    nd = len(shape)
    return pl.BlockSpec(shape, lambda *_: (0,) * nd)


def _in_proj_kernel(x_ref, g_ref, wmain_ref, wside_ref, fb_ref, qg_ref, wuq_ref, kvg_ref, wukv_ref,
                    cos_ref, sin_ref,
                    swaq_ref, swak_ref, swav_ref, foxq_ref, foxk_ref, foxv_ref,
                    mlaq_ref, mlak_ref, mlav_ref, carry_ref):
    tm = x_ref.shape[0]
    h = _rms(x_ref[...], g_ref[...]).astype(BF16)

    main = jnp.dot(h, wmain_ref[...], preferred_element_type=F32)
    scale = HEAD_DIM ** -0.5
    swaq_ref[...] = (main[:, 0:512] * scale).astype(BF16)
    swak_ref[...] = main[:, 512:640].astype(BF16)
    swav_ref[...] = main[:, 640:768].astype(BF16)
    foxv_ref[...] = main[:, 1280:1536].astype(BF16)

    side = jnp.dot(h, wside_ref[...], preferred_element_type=F32)
    cos = cos_ref[...]
    sin = sin_ref[...]

    cqn = _rms(side[:, 0:256], qg_ref[...]).astype(BF16)
    qa = jnp.dot(cqn, wuq_ref[...], preferred_element_type=F32)
    for hh in range(MLA_HEADS):
        lo = hh * LANES
        a = qa[:, lo:lo + LANES]
        b = qa[:, 512 + lo:512 + lo + LANES]
        mlaq_ref[:, lo:lo + LANES] = (a * cos + b * sin).astype(BF16)

    ckvn = _rms(side[:, 256:384], kvg_ref[...]).astype(BF16)
    kv = jnp.dot(ckvn, wukv_ref[...], preferred_element_type=F32)
    krot = side[:, 384:512] * cos + side[:, 512:640] * sin
    for hh in range(MLA_HEADS):
        lo = hh * LANES
        mlak_ref[:, lo:lo + LANES] = (kv[:, lo:lo + LANES] + krot).astype(BF16)
    mlav_ref[...] = kv[:, 512:768].astype(BF16)

    lane = lax.broadcasted_iota(jnp.int32, (tm, LANES), 1)
    row = lax.broadcasted_iota(jnp.int32, (tm, LANES), 0)
    c = jnp.where(lane < FOX_HEADS, jax.nn.log_sigmoid(side[:, 640:768] + fb_ref[...]), 0.0)
    shift = 1
    while shift < tm:
        c = c + jnp.where(row >= shift, pltpu.roll(c, shift, axis=0), 0.0)
        shift *= 2

    @pl.when(pl.program_id(1) == 0)
    def _():
        carry_ref[...] = jnp.zeros_like(carry_ref)

    c = c + carry_ref[0:1, :]
    carry_ref[...] = jnp.broadcast_to(c[tm - 1:tm, :], carry_ref.shape)

    for hd in range(FOX_HEADS):
        pair, half = hd // 2, hd % 2
        o = 0 if half else HEAD_DIM
        f = jnp.broadcast_to(jnp.sum(jnp.where(lane == hd, c, 0.0), axis=1, keepdims=True), c.shape)
        hi = f.astype(BF16).astype(F32)
        mid = (f - hi).astype(BF16).astype(F32)
        low = (f - hi) - mid
        one_k = jnp.where(jnp.logical_and(lane >= o + 3, lane < o + 6), 1.0, 0.0)
        one_q = jnp.where(jnp.logical_and(lane >= o, lane < o + 3), 1.0, 0.0)
        extra_k = jnp.where(lane == o, -hi, jnp.where(lane == o + 1, -mid, jnp.where(lane == o + 2, -low, one_k)))
        extra_q = jnp.where(lane == o + 3, hi, jnp.where(lane == o + 4, mid, jnp.where(lane == o + 5, low, one_q)))
        data = _half_lanes(c.shape, half)
        slab = slice(hd * LANES, (hd + 1) * LANES)
        q_pair = main[:, 768 + pair * LANES:768 + (pair + 1) * LANES] * scale
        k_pair = main[:, 1024 + pair * LANES:1024 + (pair + 1) * LANES]
        foxq_ref[:, slab] = jnp.where(data, q_pair, extra_q).astype(BF16)
        foxk_ref[:, slab] = jnp.where(data, k_pair, extra_k).astype(BF16)


def _in_proj(x2, g, wmain, wside, fb, qg, wuq, kvg, wukv, cos_t, sin_t, batch, seq):
    T = x2.shape[0]
    tm = ROW_TILE
    ns = seq // tm
    row = lambda b, s: (b * ns + s, 0)
    seqrow = lambda b, s: (s, 0)
    out_shape = [
        jax.ShapeDtypeStruct((T, 512), BF16), jax.ShapeDtypeStruct((T, 128), BF16),
        jax.ShapeDtypeStruct((T, 128), BF16), jax.ShapeDtypeStruct((T, 512), BF16),
        jax.ShapeDtypeStruct((T, 512), BF16), jax.ShapeDtypeStruct((T, 256), BF16),
        jax.ShapeDtypeStruct((T, 512), BF16), jax.ShapeDtypeStruct((T, 512), BF16),
        jax.ShapeDtypeStruct((T, 256), BF16),
    ]
    out_specs = [
        pl.BlockSpec((tm, 512), row), pl.BlockSpec((tm, 128), row), pl.BlockSpec((tm, 128), row),
        pl.BlockSpec((tm, 512), row), pl.BlockSpec((tm, 512), row), pl.BlockSpec((tm, 256), row),
        pl.BlockSpec((tm, 512), row), pl.BlockSpec((tm, 512), row), pl.BlockSpec((tm, 256), row),
    ]
    in_specs = [
        pl.BlockSpec((tm, D_MODEL), row), _full(g.shape), _full(wmain.shape), _full(wside.shape),
        _full(fb.shape), _full(qg.shape), _full(wuq.shape), _full(kvg.shape), _full(wukv.shape),
        pl.BlockSpec((tm, LANES), seqrow), pl.BlockSpec((tm, LANES), seqrow),
    ]
    return pl.pallas_call(
        _in_proj_kernel,
        grid=(batch, ns),
        in_specs=in_specs,
        out_specs=out_specs,
        out_shape=out_shape,
        scratch_shapes=[pltpu.VMEM((8, LANES), F32)],
        compiler_params=_params("arbitrary", "arbitrary"),
        name="in_proj",
    )(x2, g, wmain, wside, fb, qg, wuq, kvg, wukv, cos_t, sin_t)


def _half_lanes(shape, hh):
    lane = lax.broadcasted_iota(jnp.int32, shape, len(shape) - 1)
    return (lane < HEAD_DIM) if hh == 0 else (lane >= HEAD_DIM)


def _ones_in_other_half(v2, hh):
    return jnp.where(_half_lanes(v2.shape, hh), v2, jnp.ones_like(v2))


def _swa_kernel(q_ref, kc_ref, kp_ref, vc_ref, vp_ref, bias_ref, sink_ref, o_ref, kbuf, vbuf):
    first_tile = pl.program_id(1) == 0
    nwin = q_ref.shape[0] // WINDOW
    npair = SWA_Q_HEADS // 2
    kbuf[0:WINDOW, :] = kp_ref[...]
    kbuf[WINDOW:, :] = kc_ref[...]
    vbuf[0:WINDOW, :] = vp_ref[...]
    vbuf[WINDOW:, :] = vc_ref[...]
    kj = lax.broadcasted_iota(jnp.int32, (1, 2 * WINDOW), 1)
    no_prev = jnp.where(jnp.logical_and(kj < WINDOW, first_tile), NEG_INF, 0.0)
    low = _half_lanes((WINDOW, LANES), 0)
    for w in range(nwin):
        kw = kbuf[w * WINDOW:(w + 2) * WINDOW, :]
        vw = vbuf[w * WINDOW:(w + 2) * WINDOW, :]
        rows = slice(w * WINDOW, (w + 1) * WINDOW)
        q4 = [q_ref[rows, p * LANES:(p + 1) * LANES] for p in range(npair)]
        outs = []
        for hh in range(2):
            mine = _half_lanes((WINDOW, LANES), hh)
            qs = jnp.concatenate([jnp.where(mine, q, jnp.zeros_like(q)) for q in q4], axis=0)
            s = lax.dot_general(qs, kw, NT_DIMS, preferred_element_type=F32) + bias_ref[hh]
            if w == 0:
                s = s + no_prev
            sink = sink_ref[hh]
            s0, s1 = s[:, 0:LANES], s[:, LANES:]
            mloc = jnp.max(jnp.maximum(s0, s1), axis=1, keepdims=True)
            m = jnp.maximum(jnp.broadcast_to(mloc, sink.shape), sink)
            e = jnp.concatenate([jnp.exp(s0 - m), jnp.exp(s1 - m)], axis=1).astype(BF16)
            pv = jnp.dot(e, _ones_in_other_half(vw, hh), preferred_element_type=F32)
            den = pltpu.roll(pv, HEAD_DIM, axis=1) + jnp.exp(sink - m)
            outs.append(pv / den)
        for p in range(npair):
            pr = slice(p * WINDOW, (p + 1) * WINDOW)
            o_ref[rows, p * LANES:(p + 1) * LANES] = jnp.where(low, outs[0][pr], outs[1][pr])


def _swa(q, k, v, bias, sinks, batch, seq):
    T = q.shape[0]
    tm = ROW_TILE
    ns = seq // tm
    per = tm // WINDOW
    row = lambda b, s: (b * ns + s, 0)
    prev = lambda b, s: (jnp.maximum((b * ns + s) * per - 1, 0), 0)
    return pl.pallas_call(
        _swa_kernel,
        grid=(batch, ns),
        in_specs=[
            pl.BlockSpec((tm, SWA_WIDTH), row),
            pl.BlockSpec((tm, LANES), row), pl.BlockSpec((WINDOW, LANES), prev),
            pl.BlockSpec((tm, LANES), row), pl.BlockSpec((WINDOW, LANES), prev),
            _full(bias.shape), _full(sinks.shape),
        ],
        out_specs=pl.BlockSpec((tm, SWA_WIDTH), row),
        out_shape=jax.ShapeDtypeStruct((T, SWA_WIDTH), F32),
        scratch_shapes=[pltpu.VMEM((tm + WINDOW, LANES), BF16), pltpu.VMEM((tm + WINDOW, LANES), BF16)],
        compiler_params=_params("parallel", "arbitrary"),
        name="swa_attn",
    )(q, k, k, v, v, bias, sinks)


def _flash_init(m_sc, acc_sc):
    m_sc[...] = jnp.full(m_sc.shape, NEG_INF, F32)
    acc_sc[...] = jnp.zeros(acc_sc.shape, F32)


def _flash_update(s_tiles, v2, head, m_sc, acc_sc, exp_scale):
    mloc = functools.reduce(jnp.maximum, s_tiles)
    mloc = jnp.broadcast_to(jnp.max(mloc, axis=1, keepdims=True), mloc.shape)
    m_prev = m_sc[head]
    m_new = jnp.maximum(m_prev, mloc)
    m_sc[head] = m_new
    c2 = exp_scale * math.log2(math.e)
    alpha = jnp.exp2((m_prev - m_new) * c2)
    p = jnp.concatenate([jnp.exp2((s - m_new) * c2) for s in s_tiles], axis=1).astype(BF16)
    pv = jnp.dot(p, _ones_in_other_half(v2, head % 2), preferred_element_type=F32)
    acc_sc[head] = alpha * acc_sc[head] + pv


def _flash_finish(o_ref, acc_sc):
    low = _half_lanes(acc_sc.shape[1:], 0)
    for pair in range(acc_sc.shape[0] // 2):
        outs = [acc_sc[2 * pair + hh] / pltpu.roll(acc_sc[2 * pair + hh], HEAD_DIM, axis=1) for hh in range(2)]
        o_ref[:, pair * LANES:(pair + 1) * LANES] = jnp.where(low, outs[0], outs[1])


def _lane_tiles(s):
    return [s[:, j * LANES:(j + 1) * LANES] for j in range(s.shape[1] // LANES)]


def _below_diagonal(tq, tk):
    r = lax.broadcasted_iota(jnp.int32, (tq, LANES), 0)
    c = lax.broadcasted_iota(jnp.int32, (tq, LANES), 1)
    return [c + j * LANES <= r for j in range(tk // LANES)]


def _flash_kernel(qi_ref, ki_ref, q_ref, k_ref, v_ref, o_ref, m_sc, acc_sc, *, exp_scale):
    t = pl.program_id(1)
    qi = qi_ref[t]
    ki = ki_ref[t]
    tq, tk = q_ref.shape[0], k_ref.shape[0]
    nheads = q_ref.shape[1] // LANES

    @pl.when(ki == 0)
    def _():
        _flash_init(m_sc, acc_sc)

    def step(diagonal):
        for head in range(nheads):
            slab = slice(head * LANES, (head + 1) * LANES)
            pair = slice((head // 2) * LANES, (head // 2 + 1) * LANES)
            s = lax.dot_general(q_ref[:, slab], k_ref[:, slab], NT_DIMS, preferred_element_type=F32)
            tiles = _lane_tiles(s)
            if diagonal:
                tiles = [jnp.where(keep, x, NEG_INF) for keep, x in zip(_below_diagonal(tq, tk), tiles)]
            _flash_update(tiles, v_ref[:, pair], head, m_sc, acc_sc, exp_scale)

    @pl.when(ki < qi)
    def _():
        step(False)

    @pl.when(ki == qi)
    def _():
        step(True)
        _flash_finish(o_ref, acc_sc)


def _tri_tables(nq):
    qi = [q for q in range(nq) for _ in range(q + 1)]
    ki = [k for q in range(nq) for k in range(q + 1)]
    return jnp.asarray(qi, jnp.int32), jnp.asarray(ki, jnp.int32)


def _flash(q, k, v, batch, seq, exp_scale, name):
    T = q.shape[0]
    nheads = q.shape[1] // LANES
    tq = ATTN_TILE
    nq = seq // tq
    qi_tab, ki_tab = _tri_tables(nq)
    qrow = lambda b, t, qi, ki: (b * nq + qi[t], 0)
    krow = lambda b, t, qi, ki: (b * nq + ki[t], 0)
    grid_spec = pltpu.PrefetchScalarGridSpec(
        num_scalar_prefetch=2,
        grid=(batch, qi_tab.shape[0]),
        in_specs=[
            pl.BlockSpec((tq, q.shape[1]), qrow), pl.BlockSpec((tq, k.shape[1]), krow),
            pl.BlockSpec((tq, v.shape[1]), krow),
        ],
        out_specs=pl.BlockSpec((tq, v.shape[1]), qrow),
        scratch_shapes=[pltpu.VMEM((nheads, tq, LANES), F32),
                        pltpu.VMEM((nheads, tq, LANES), F32)],
    )
    return pl.pallas_call(
        functools.partial(_flash_kernel, exp_scale=exp_scale),
        grid_spec=grid_spec,
        out_shape=jax.ShapeDtypeStruct((T, v.shape[1]), F32),
        compiler_params=_params("parallel", "arbitrary"),
        name=name,
    )(qi_tab, ki_tab, q, k, v)


def _out_proj_kernel(oa_ref, ob_ref, oc_ref, gg_ref, wout_ref, pg_ref, x_ref, o_ref):
    gg = gg_ref[...]
    a = _rms(oa_ref[...], gg[:, 0:SWA_WIDTH]).astype(BF16)
    b = _rms(ob_ref[...], gg[:, SWA_WIDTH:SWA_WIDTH + FOX_WIDTH]).astype(BF16)
    c = _rms(oc_ref[...], gg[:, SWA_WIDTH + FOX_WIDTH:]).astype(BF16)
    y = jnp.dot(a, wout_ref[0:SWA_WIDTH, :], preferred_element_type=F32)
    y = y + jnp.dot(b, wout_ref[SWA_WIDTH:SWA_WIDTH + FOX_WIDTH, :], preferred_element_type=F32)
    y = y + jnp.dot(c, wout_ref[SWA_WIDTH + FOX_WIDTH:, :], preferred_element_type=F32)
    o_ref[...] = x_ref[...] + _rms(y, pg_ref[...])


def _out_proj(oa, ob, oc, gg, wout, pg, x2):
    T = x2.shape[0]
    tm = ROW_TILE
    row = lambda i: (i, 0)
    return pl.pallas_call(
        _out_proj_kernel,
        grid=(T // tm,),
        in_specs=[
            pl.BlockSpec((tm, SWA_WIDTH), row), pl.BlockSpec((tm, FOX_WIDTH), row),
            pl.BlockSpec((tm, MLA_WIDTH), row), _full(gg.shape), _full(wout.shape), _full(pg.shape),
            pl.BlockSpec((tm, D_MODEL), row),
        ],
        out_specs=pl.BlockSpec((tm, D_MODEL), row),
        out_shape=jax.ShapeDtypeStruct((T, D_MODEL), F32),
        compiler_params=_params("parallel"),
        name="out_proj",
    )(oa, ob, oc, gg, wout, pg, x2)


def _ffn_kernel(x_ref, g_ref, wup_ref, cw_ref, cb_ref, wdown_ref, pg_ref, o_ref,
                hbuf, ubuf, acc_ref, carry_ref):
    tm = x_ref.shape[0]
    hbuf[...] = _rms(x_ref[...], g_ref[...]).astype(BF16)

    @pl.when(pl.program_id(1) == 0)
    def _():
        carry_ref[...] = jnp.zeros_like(carry_ref)

    row8 = lax.broadcasted_iota(jnp.int32, (8, 2 * FF_CHUNK), 0)

    def up_proj(c, slot):
        ubuf[slot] = jnp.dot(hbuf[...], wup_ref[c], preferred_element_type=F32)

    def conv_act_down(c, slot, first=False):
        u = ubuf[slot]
        prev = carry_ref[c]
        carry_ref[c] = u[tm - 8:tm, :]
        w = cw_ref[c]
        y = cb_ref[c] + w[2:3, :] * u
        for d in (1, 2):
            r = pltpu.roll(u, d, axis=0)
            head = jnp.where(row8 < d, pltpu.roll(prev, d, axis=0), r[0:8, :])
            r = jnp.concatenate([head, r[8:, :]], axis=0)
            y = y + w[2 - d:3 - d, :] * r
        act = (jax.nn.gelu(y[:, 0:FF_CHUNK], approximate=True) * y[:, FF_CHUNK:]).astype(BF16)
        down = jnp.dot(act, wdown_ref[c], preferred_element_type=F32)
        if first:
            acc_ref[...] = down
        else:
            acc_ref[...] += down

    up_proj(0, 0)
    up_proj(1, 1)
    conv_act_down(0, 0, first=True)

    def two_chunks(j, _):
        c = 2 * j + 1
        up_proj(c + 1, 0)
        conv_act_down(c, 1)
        up_proj(c + 2, 1)
        conv_act_down(c + 1, 0)
        return 0

    lax.fori_loop(0, (N_FF_CHUNKS - 3) // 2, two_chunks, 0)
    up_proj(N_FF_CHUNKS - 1, 0)
    conv_act_down(N_FF_CHUNKS - 2, 1)
    conv_act_down(N_FF_CHUNKS - 1, 0)
    o_ref[...] = x_ref[...] + _rms(acc_ref[...], pg_ref[...])


def _ffn(x2, g, wup, cw, cb, wdown, pg, batch, seq):
    T = x2.shape[0]
    tm = ROW_TILE
    ns = seq // tm
    row = lambda b, s: (b * ns + s, 0)
    return pl.pallas_call(
        _ffn_kernel,
        grid=(batch, ns),
        in_specs=[
            pl.BlockSpec((tm, D_MODEL), row), _full(g.shape), _full(wup.shape), _full(cw.shape),
            _full(cb.shape), _full(wdown.shape), _full(pg.shape),
        ],
        out_specs=pl.BlockSpec((tm, D_MODEL), row),
        out_shape=jax.ShapeDtypeStruct((T, D_MODEL), F32),
        scratch_shapes=[
            pltpu.VMEM((tm, D_MODEL), BF16), pltpu.VMEM((2, tm, 2 * FF_CHUNK), F32),
            pltpu.VMEM((tm, D_MODEL), F32), pltpu.VMEM((N_FF_CHUNKS, 8, 2 * FF_CHUNK), F32),
        ],
        compiler_params=_params("parallel", "arbitrary"),
        name="conv_ffn",
    )(x2, g, wup, cw, cb, wdown, pg)


def _t5_causal_bucket(dist):
    max_exact = REL_BUCKETS // 2
    d = jnp.maximum(dist, 0)
    log_ratio = jnp.log(jnp.maximum(d, 1).astype(F32) / max_exact) / math.log(REL_MAX_DIST / max_exact)
    large = max_exact + (log_ratio * (REL_BUCKETS - max_exact)).astype(jnp.int32)
    large = jnp.minimum(large, REL_BUCKETS - 1)
    return jnp.where(d < max_exact, d, large)


def _swa_bias_table(rel_bias):
    qi = jnp.arange(WINDOW, dtype=jnp.int32)[:, None] + WINDOW
    kj = jnp.arange(2 * WINDOW, dtype=jnp.int32)[None, :]
    dist = qi - kj
    in_band = (dist >= 0) & (dist < WINDOW)
    bias = rel_bias.astype(F32)[_t5_causal_bucket(dist)]
    bias = jnp.where(in_band[:, :, None], bias, NEG_INF).transpose(2, 0, 1)
    return bias.reshape(SWA_KV_HEADS, (SWA_Q_HEADS // SWA_KV_HEADS) * WINDOW, 2 * WINDOW)


def _rope_tables(seq):
    pos = jnp.arange(seq, dtype=F32)
    inv_freq = ROPE_THETA ** (-(jnp.arange(MLA_ROPE_DIM // 2, dtype=F32) * 2.0 / MLA_ROPE_DIM))
    ang = pos[:, None] * inv_freq[None, :]
    cos, sin = jnp.cos(ang), jnp.sin(ang)
    pad = jnp.zeros((seq, LANES - MLA_QK_DIM), F32)
    cos_t = jnp.concatenate([jnp.ones((seq, MLA_NOPE_DIM), F32), cos, cos, pad], axis=1)
    sin_t = jnp.concatenate([jnp.zeros((seq, MLA_NOPE_DIM), F32), -sin, sin, pad], axis=1)
    return cos_t, sin_t


def _swap_halves(w):
    half = w.shape[1] // 2
    return jnp.concatenate([w[:, half:], w[:, :half]], axis=1)


def _layer_params(l, w_in, forget_bias, w_uq, w_ukv, group_norm, w_out, w_up, conv_w, conv_b, w_down):
    wi = w_in[l]
    d = wi.shape[0]
    order = jnp.asarray(SWA_HEAD_ORDER)
    swa_q = wi[:, 0:SWA_WIDTH].reshape(d, SWA_Q_HEADS, HEAD_DIM)[:, order].reshape(d, SWA_WIDTH)
    wmain = jnp.concatenate([swa_q, wi[:, SWA_WIDTH:SWA_COLS + 3 * FOX_WIDTH]], axis=1).astype(BF16)

    f0 = SWA_COLS + 3 * FOX_WIDTH
    m0 = SWA_COLS + FOX_COLS
    z = lambda n: jnp.zeros((d, n), F32)
    w_kr = wi[:, m0 + MLA_Q_RANK + MLA_KV_RANK:]
    rope_pad = LANES - MLA_QK_DIM
    wside = jnp.concatenate([
        wi[:, m0:m0 + MLA_Q_RANK + MLA_KV_RANK],
        z(MLA_NOPE_DIM), w_kr, z(rope_pad),
        z(MLA_NOPE_DIM), _swap_halves(w_kr), z(rope_pad),
        wi[:, f0:f0 + FOX_HEADS], z(LANES - FOX_HEADS),
    ], axis=1).astype(BF16)
    fb = jnp.pad(forget_bias[l], (0, LANES - FOX_HEADS))[None, :]

    uq = w_uq[l].reshape(MLA_Q_RANK, MLA_HEADS, MLA_QK_DIM)
    zq = jnp.zeros((MLA_Q_RANK, MLA_HEADS, 1), F32)
    plain = jnp.concatenate([uq, jnp.tile(zq, (1, 1, rope_pad))], axis=2)
    rope = uq[:, :, MLA_NOPE_DIM:]
    swapped = jnp.concatenate([jnp.tile(zq, (1, 1, MLA_NOPE_DIM)), rope[:, :, MLA_ROPE_DIM // 2:],
                               rope[:, :, :MLA_ROPE_DIM // 2], jnp.tile(zq, (1, 1, rope_pad))], axis=2)
    wuq = jnp.concatenate([plain.reshape(MLA_Q_RANK, -1), swapped.reshape(MLA_Q_RANK, -1)], axis=1).astype(BF16)

    ukv = w_ukv[l].reshape(MLA_KV_RANK, MLA_HEADS, MLA_NOPE_DIM + MLA_V_DIM)
    k_slabs = jnp.concatenate([ukv[:, :, :MLA_NOPE_DIM],
                               jnp.zeros((MLA_KV_RANK, MLA_HEADS, LANES - MLA_NOPE_DIM), F32)], axis=2)
    wukv = jnp.concatenate([k_slabs.reshape(MLA_KV_RANK, -1),
                            ukv[:, :, MLA_NOPE_DIM:].reshape(MLA_KV_RANK, -1)], axis=1).astype(BF16)

    gg = group_norm[l]
    gg = jnp.concatenate([gg[:SWA_WIDTH].reshape(SWA_Q_HEADS, HEAD_DIM)[order].reshape(-1), gg[SWA_WIDTH:]])[None, :]
    wo = w_out[l]
    wo = jnp.concatenate([wo[:SWA_WIDTH].reshape(SWA_Q_HEADS, HEAD_DIM, -1)[order].reshape(SWA_WIDTH, -1),
                          wo[SWA_WIDTH:]], axis=0).astype(BF16)

    def gate_up(a):
        lead = a.shape[:-1]
        a = a.reshape(lead + (2, N_FF_CHUNKS, FF_CHUNK))
        a = jnp.moveaxis(a, -2, 0)
        return a.reshape((N_FF_CHUNKS,) + lead + (2 * FF_CHUNK,))

    wup = gate_up(w_up[l]).astype(BF16)
    cw = gate_up(conv_w[l])
    cb = gate_up(conv_b[l][None, :])
    wdown = w_down[l].reshape(N_FF_CHUNKS, FF_CHUNK, -1).astype(BF16)
    return wmain, wside, fb, wuq, wukv, gg, wo, wup, cw, cb, wdown


def kernel(x, attn_pre_norm, w_in, forget_bias, swa_sinks, rel_bias, q_latent_norm, w_uq, kv_latent_norm, w_ukv,
           group_norm, w_out, attn_post_norm, ffn_pre_norm, w_up, conv_w, conv_b, w_down, ffn_post_norm):
    batch, seq, d = x.shape
    assert d == D_MODEL and seq % ROW_TILE == 0 and seq % ATTN_TILE == 0
    depth = w_in.shape[0]
    cos_t, sin_t = _rope_tables(seq)
    bias = _swa_bias_table(rel_bias)
    x2 = x.reshape(batch * seq, d)
    for l in range(depth):
        wmain, wside, fb, wuq, wukv, gg, wo, wup, cw, cb, wdown = _layer_params(
            l, w_in, forget_bias, w_uq, w_ukv, group_norm, w_out, w_up, conv_w, conv_b, w_down)
        sinks = jnp.repeat(swa_sinks[l].astype(F32).reshape(SWA_KV_HEADS, -1), WINDOW, axis=1)
        sinks = jnp.broadcast_to(sinks[:, :, None], sinks.shape + (LANES,))
        (swa_q, swa_k, swa_v, fox_q, fox_k, fox_v, mla_q, mla_k, mla_v) = _in_proj(
            x2, attn_pre_norm[l][None, :], wmain, wside, fb, q_latent_norm[l][None, :], wuq,
            kv_latent_norm[l][None, :], wukv, cos_t, sin_t, batch, seq)
        out_a = _swa(swa_q, swa_k, swa_v, bias, sinks, batch, seq)
        out_b = _flash(fox_q, fox_k, fox_v, batch, seq, 1.0, "fox_attn")
        out_c = _flash(mla_q, mla_k, mla_v, batch, seq, MLA_QK_DIM ** -0.5, "mla_attn")
        x2 = _out_proj(out_a, out_b, out_c, gg, wo, attn_post_norm[l][None, :], x2)
        x2 = _ffn(x2, ffn_pre_norm[l][None, :], wup, cw, cb, wdown, ffn_post_norm[l][None, :], batch, seq)
    return x2.reshape(batch, seq, d)
```

```python
import functools
import math

import jax
import jax.numpy as jnp
from jax import lax
from jax.experimental import pallas as pl
from jax.experimental.pallas import tpu as pltpu

D_MODEL = 1024
HEAD_DIM = 64
SWA_Q_HEADS = 8
SWA_KV_HEADS = 2
WINDOW = 128
FOX_HEADS = 4
MLA_HEADS = 4
MLA_Q_RANK = 256
MLA_KV_RANK = 128
MLA_NOPE_DIM = 64
MLA_ROPE_DIM = 32
MLA_V_DIM = 64
MLA_QK_DIM = MLA_NOPE_DIM + MLA_ROPE_DIM
ROPE_THETA = 10000.0
REL_BUCKETS = 32
REL_MAX_DIST = 128
D_FF = 2816
EPS = 1e-6
NEG_INF = -1e30

SWA_WIDTH = SWA_Q_HEADS * HEAD_DIM
FOX_WIDTH = FOX_HEADS * HEAD_DIM
MLA_WIDTH = MLA_HEADS * MLA_V_DIM
SWA_COLS = (SWA_Q_HEADS + 2 * SWA_KV_HEADS) * HEAD_DIM
FOX_COLS = 3 * FOX_WIDTH + FOX_HEADS

LANES = 128
ROW_TILE = 512
ATTN_TILE = 512
FF_CHUNK = 256
N_FF_CHUNKS = D_FF // FF_CHUNK
VMEM_LIMIT = 56 * 1024 * 1024

SWA_HEAD_ORDER = (0, 4, 1, 5, 2, 6, 3, 7)

F32 = jnp.float32
BF16 = jnp.bfloat16
NT_DIMS = (((1,), (1,)), ((), ()))


def _rms(x, g):
    ms = jnp.mean(x * x, axis=-1, keepdims=True)
    return x * lax.rsqrt(ms + EPS) * g


def _params(*sem):
    return pltpu.CompilerParams(dimension_semantics=sem, vmem_limit_bytes=VMEM_LIMIT)


def _full(shape):
    nd = len(shape)
    return pl.BlockSpec(shape, lambda *_: (0,) * nd)


def _in_proj_kernel(x_ref, g_ref, wmain_ref, wside_ref, fb_ref, qg_ref, wuq_ref, kvg_ref, wukv_ref,
                    cos_ref, sin_ref,
                    swaq_ref, swak_ref, swav_ref, foxq_ref, foxk_ref, foxv_ref,
                    mlaq_ref, mlak_ref, mlav_ref, carry_ref):
    tm = x_ref.shape[0]
    h = _rms(x_ref[...], g_ref[...]).astype(BF16)

    main = jnp.dot(h, wmain_ref[...], preferred_element_type=F32)
    scale = HEAD_DIM ** -0.5
    swaq_ref[...] = (main[:, 0:512] * scale).astype(BF16)
    swak_ref[...] = main[:, 512:640].astype(BF16)
    swav_ref[...] = main[:, 640:768].astype(BF16)
    foxv_ref[...] = main[:, 1280:1536].astype(BF16)

    side = jnp.dot(h, wside_ref[...], preferred_element_type=F32)
    cos = cos_ref[...]
    sin = sin_ref[...]

    cqn = _rms(side[:, 0:256], qg_ref[...]).astype(BF16)
    qa = jnp.dot(cqn, wuq_ref[...], preferred_element_type=F32)
    for hh in range(MLA_HEADS):
        lo = hh * LANES
        a = qa[:, lo:lo + LANES]
        b = qa[:, 512 + lo:512 + lo + LANES]
        mlaq_ref[:, lo:lo + LANES] = (a * cos + b * sin).astype(BF16)

    ckvn = _rms(side[:, 256:384], kvg_ref[...]).astype(BF16)
    kv = jnp.dot(ckvn, wukv_ref[...], preferred_element_type=F32)
    krot = side[:, 384:512] * cos + side[:, 512:640] * sin
    for hh in range(MLA_HEADS):
        lo = hh * LANES
        mlak_ref[:, lo:lo + LANES] = (kv[:, lo:lo + LANES] + krot).astype(BF16)
    mlav_ref[...] = kv[:, 512:768].astype(BF16)

    lane = lax.broadcasted_iota(jnp.int32, (tm, LANES), 1)
    row = lax.broadcasted_iota(jnp.int32, (tm, LANES), 0)
    c = jnp.where(lane < FOX_HEADS, jax.nn.log_sigmoid(side[:, 640:768] + fb_ref[...]), 0.0)
    shift = 1
    while shift < tm:
        c = c + jnp.where(row >= shift, pltpu.roll(c, shift, axis=0), 0.0)
        shift *= 2

    @pl.when(pl.program_id(1) == 0)
    def _():
        carry_ref[...] = jnp.zeros_like(carry_ref)

    c = c + carry_ref[0:1, :]
    carry_ref[...] = jnp.broadcast_to(c[tm - 1:tm, :], carry_ref.shape)

    for hd in range(FOX_HEADS):
        pair, half = hd // 2, hd % 2
        o = 0 if half else HEAD_DIM
        f = jnp.broadcast_to(jnp.sum(jnp.where(lane == hd, c, 0.0), axis=1, keepdims=True), c.shape)
        hi = f.astype(BF16).astype(F32)
        mid = (f - hi).astype(BF16).astype(F32)
        low = (f - hi) - mid
        one_k = jnp.where(jnp.logical_and(lane >= o + 3, lane < o + 6), 1.0, 0.0)
        one_q = jnp.where(jnp.logical_and(lane >= o, lane < o + 3), 1.0, 0.0)
        extra_k = jnp.where(lane == o, -hi, jnp.where(lane == o + 1, -mid, jnp.where(lane == o + 2, -low, one_k)))
        extra_q = jnp.where(lane == o + 3, hi, jnp.where(lane == o + 4, mid, jnp.where(lane == o + 5, low, one_q)))
        data = _half_lanes(c.shape, half)
        slab = slice(hd * LANES, (hd + 1) * LANES)
        q_pair = main[:, 768 + pair * LANES:768 + (pair + 1) * LANES] * scale
        k_pair = main[:, 1024 + pair * LANES:1024 + (pair + 1) * LANES]
        foxq_ref[:, slab] = jnp.where(data, q_pair, extra_q).astype(BF16)
        foxk_ref[:, slab] = jnp.where(data, k_pair, extra_k).astype(BF16)


def _in_proj(x2, g, wmain, wside, fb, qg, wuq, kvg, wukv, cos_t, sin_t, batch, seq):
    T = x2.shape[0]
    tm = ROW_TILE
    ns = seq // tm
    row = lambda b, s: (b * ns + s, 0)
    seqrow = lambda b, s: (s, 0)
    out_shape = [
        jax.ShapeDtypeStruct((T, 512), BF16), jax.ShapeDtypeStruct((T, 128), BF16),
        jax.ShapeDtypeStruct((T, 128), BF16), jax.ShapeDtypeStruct((T, 512), BF16),
        jax.ShapeDtypeStruct((T, 512), BF16), jax.ShapeDtypeStruct((T, 256), BF16),
        jax.ShapeDtypeStruct((T, 512), BF16), jax.ShapeDtypeStruct((T, 512), BF16),
        jax.ShapeDtypeStruct((T, 256), BF16),
    ]
    out_specs = [
        pl.BlockSpec((tm, 512), row), pl.BlockSpec((tm, 128), row), pl.BlockSpec((tm, 128), row),
        pl.BlockSpec((tm, 512), row), pl.BlockSpec((tm, 512), row), pl.BlockSpec((tm, 256), row),
        pl.BlockSpec((tm, 512), row), pl.BlockSpec((tm, 512), row), pl.BlockSpec((tm, 256), row),
    ]
    in_specs = [
        pl.BlockSpec((tm, D_MODEL), row), _full(g.shape), _full(wmain.shape), _full(wside.shape),
        _full(fb.shape), _full(qg.shape), _full(wuq.shape), _full(kvg.shape), _full(wukv.shape),
        pl.BlockSpec((tm, LANES), seqrow), pl.BlockSpec((tm, LANES), seqrow),
    ]
    return pl.pallas_call(
        _in_proj_kernel,
        grid=(batch, ns),
        in_specs=in_specs,
        out_specs=out_specs,
        out_shape=out_shape,
        scratch_shapes=[pltpu.VMEM((8, LANES), F32)],
        compiler_params=_params("arbitrary", "arbitrary"),
        name="in_proj",
    )(x2, g, wmain, wside, fb, qg, wuq, kvg, wukv, cos_t, sin_t)


def _half_lanes(shape, hh):
    lane = lax.broadcasted_iota(jnp.int32, shape, len(shape) - 1)
    return (lane < HEAD_DIM) if hh == 0 else (lane >= HEAD_DIM)


def _ones_in_other_half(v2, hh):
    return jnp.where(_half_lanes(v2.shape, hh), v2, jnp.ones_like(v2))


def _swa_kernel(q_ref, kc_ref, kp_ref, vc_ref, vp_ref, bias_ref, sink_ref, o_ref, kbuf, vbuf):
    first_tile = pl.program_id(1) == 0
    nwin = q_ref.shape[0] // WINDOW
    npair = SWA_Q_HEADS // 2
    kbuf[0:WINDOW, :] = kp_ref[...]
    kbuf[WINDOW:, :] = kc_ref[...]
    vbuf[0:WINDOW, :] = vp_ref[...]
    vbuf[WINDOW:, :] = vc_ref[...]
    kj = lax.broadcasted_iota(jnp.int32, (1, 2 * WINDOW), 1)
    no_prev = jnp.where(jnp.logical_and(kj < WINDOW, first_tile), NEG_INF, 0.0)
    low = _half_lanes((WINDOW, LANES), 0)
    for w in range(nwin):
        kw = kbuf[w * WINDOW:(w + 2) * WINDOW, :]
        vw = vbuf[w * WINDOW:(w + 2) * WINDOW, :]
        rows = slice(w * WINDOW, (w + 1) * WINDOW)
        q4 = [q_ref[rows, p * LANES:(p + 1) * LANES] for p in range(npair)]
        outs = []
        for hh in range(2):
            mine = _half_lanes((WINDOW, LANES), hh)
            qs = jnp.concatenate([jnp.where(mine, q, jnp.zeros_like(q)) for q in q4], axis=0)
            s = lax.dot_general(qs, kw, NT_DIMS, preferred_element_type=F32) + bias_ref[hh]
            if w == 0:
                s = s + no_prev
            sink = sink_ref[hh]
            s0, s1 = s[:, 0:LANES], s[:, LANES:]
            mloc = jnp.max(jnp.maximum(s0, s1), axis=1, keepdims=True)
            m = jnp.maximum(jnp.broadcast_to(mloc, sink.shape), sink)
            e = jnp.concatenate([jnp.exp(s0 - m), jnp.exp(s1 - m)], axis=1).astype(BF16)
            pv = jnp.dot(e, _ones_in_other_half(vw, hh), preferred_element_type=F32)
            den = pltpu.roll(pv, HEAD_DIM, axis=1) + jnp.exp(sink - m)
            outs.append(pv / den)
        for p in range(npair):
            pr = slice(p * WINDOW, (p + 1) * WINDOW)
            o_ref[rows, p * LANES:(p + 1) * LANES] = jnp.where(low, outs[0][pr], outs[1][pr])


def _swa(q, k, v, bias, sinks, batch, seq):
    T = q.shape[0]
    tm = ROW_TILE
    ns = seq // tm
    per = tm // WINDOW
    row = lambda b, s: (b * ns + s, 0)
    prev = lambda b, s: (jnp.maximum((b * ns + s) * per - 1, 0), 0)
    return pl.pallas_call(
        _swa_kernel,
        grid=(batch, ns),
        in_specs=[
            pl.BlockSpec((tm, SWA_WIDTH), row),
            pl.BlockSpec((tm, LANES), row), pl.BlockSpec((WINDOW, LANES), prev),
            pl.BlockSpec((tm, LANES), row), pl.BlockSpec((WINDOW, LANES), prev),
            _full(bias.shape), _full(sinks.shape),
        ],
        out_specs=pl.BlockSpec((tm, SWA_WIDTH), row),
        out_shape=jax.ShapeDtypeStruct((T, SWA_WIDTH), F32),
        scratch_shapes=[pltpu.VMEM((tm + WINDOW, LANES), BF16), pltpu.VMEM((tm + WINDOW, LANES), BF16)],
        compiler_params=_params("parallel", "arbitrary"),
        name="swa_attn",
    )(q, k, k, v, v, bias, sinks)


def _flash_init(m_sc, acc_sc):
    m_sc[...] = jnp.full(m_sc.shape, NEG_INF, F32)
    acc_sc[...] = jnp.zeros(acc_sc.shape, F32)


def _flash_update(s_tiles, v2, head, m_sc, acc_sc, exp_scale):
    mloc = functools.reduce(jnp.maximum, s_tiles)
    mloc = jnp.broadcast_to(jnp.max(mloc, axis=1, keepdims=True), mloc.shape)
    m_prev = m_sc[head]
    m_new = jnp.maximum(m_prev, mloc)
    m_sc[head] = m_new
    c2 = exp_scale * math.log2(math.e)
    alpha = jnp.exp2((m_prev - m_new) * c2)
    p = jnp.concatenate([jnp.exp2((s - m_new) * c2) for s in s_tiles], axis=1).astype(BF16)
    pv = jnp.dot(p, _ones_in_other_half(v2, head % 2), preferred_element_type=F32)
    acc_sc[head] = alpha * acc_sc[head] + pv


def _flash_finish(o_ref, acc_sc):
    low = _half_lanes(acc_sc.shape[1:], 0)
    for pair in range(acc_sc.shape[0] // 2):
        outs = [acc_sc[2 * pair + hh] / pltpu.roll(acc_sc[2 * pair + hh], HEAD_DIM, axis=1) for hh in range(2)]
        o_ref[:, pair * LANES:(pair + 1) * LANES] = jnp.where(low, outs[0], outs[1])


def _lane_tiles(s):
    return [s[:, j * LANES:(j + 1) * LANES] for j in range(s.shape[1] // LANES)]


def _below_diagonal(tq, tk):
    r = lax.broadcasted_iota(jnp.int32, (tq, LANES), 0)
    c = lax.broadcasted_iota(jnp.int32, (tq, LANES), 1)
    return [c + j * LANES <= r for j in range(tk // LANES)]


def _flash_kernel(qi_ref, ki_ref, q_ref, k_ref, v_ref, o_ref, m_sc, acc_sc, *, exp_scale):
    t = pl.program_id(1)
    qi = qi_ref[t]
    ki = ki_ref[t]
    tq, tk = q_ref.shape[0], k_ref.shape[0]
    nheads = q_ref.shape[1] // LANES

    @pl.when(ki == 0)
    def _():
        _flash_init(m_sc, acc_sc)

    def step(diagonal):
        for head in range(nheads):
            slab = slice(head * LANES, (head + 1) * LANES)
            pair = slice((head // 2) * LANES, (head // 2 + 1) * LANES)
            s = lax.dot_general(q_ref[:, slab], k_ref[:, slab], NT_DIMS, preferred_element_type=F32)
            tiles = _lane_tiles(s)
            if diagonal:
                tiles = [jnp.where(keep, x, NEG_INF) for keep, x in zip(_below_diagonal(tq, tk), tiles)]
            _flash_update(tiles, v_ref[:, pair], head, m_sc, acc_sc, exp_scale)

    @pl.when(ki < qi)
    def _():
        step(False)

    @pl.when(ki == qi)
    def _():
        step(True)
        _flash_finish(o_ref, acc_sc)


def _tri_tables(nq):
    qi = [q for q in range(nq) for _ in range(q + 1)]
    ki = [k for q in range(nq) for k in range(q + 1)]
    return jnp.asarray(qi, jnp.int32), jnp.asarray(ki, jnp.int32)


def _flash(q, k, v, batch, seq, exp_scale, name):
    T = q.shape[0]
    nheads = q.shape[1] // LANES
    tq = ATTN_TILE
    nq = seq // tq
    qi_tab, ki_tab = _tri_tables(nq)
    qrow = lambda b, t, qi, ki: (b * nq + qi[t], 0)
    krow = lambda b, t, qi, ki: (b * nq + ki[t], 0)
    grid_spec = pltpu.PrefetchScalarGridSpec(
        num_scalar_prefetch=2,
        grid=(batch, qi_tab.shape[0]),
        in_specs=[
            pl.BlockSpec((tq, q.shape[1]), qrow), pl.BlockSpec((tq, k.shape[1]), krow),
            pl.BlockSpec((tq, v.shape[1]), krow),
        ],
        out_specs=pl.BlockSpec((tq, v.shape[1]), qrow),
        scratch_shapes=[pltpu.VMEM((nheads, tq, LANES), F32),
                        pltpu.VMEM((nheads, tq, LANES), F32)],
    )
    return pl.pallas_call(
        functools.partial(_flash_kernel, exp_scale=exp_scale),
        grid_spec=grid_spec,
        out_shape=jax.ShapeDtypeStruct((T, v.shape[1]), F32),
        compiler_params=_params("parallel", "arbitrary"),
        name=name,
    )(qi_tab, ki_tab, q, k, v)


def _out_proj_kernel(oa_ref, ob_ref, oc_ref, gg_ref, wout_ref, pg_ref, x_ref, o_ref):
    gg = gg_ref[...]
    a = _rms(oa_ref[...], gg[:, 0:SWA_WIDTH]).astype(BF16)
    b = _rms(ob_ref[...], gg[:, SWA_WIDTH:SWA_WIDTH + FOX_WIDTH]).astype(BF16)
    c = _rms(oc_ref[...], gg[:, SWA_WIDTH + FOX_WIDTH:]).astype(BF16)
    y = jnp.dot(a, wout_ref[0:SWA_WIDTH, :], preferred_element_type=F32)
    y = y + jnp.dot(b, wout_ref[SWA_WIDTH:SWA_WIDTH + FOX_WIDTH, :], preferred_element_type=F32)
    y = y + jnp.dot(c, wout_ref[SWA_WIDTH + FOX_WIDTH:, :], preferred_element_type=F32)
    o_ref[...] = x_ref[...] + _rms(y, pg_ref[...])


def _out_proj(oa, ob, oc, gg, wout, pg, x2):
    T = x2.shape[0]
    tm = ROW_TILE
    row = lambda i: (i, 0)
    return pl.pallas_call(
        _out_proj_kernel,
        grid=(T // tm,),
        in_specs=[
            pl.BlockSpec((tm, SWA_WIDTH), row), pl.BlockSpec((tm, FOX_WIDTH), row),
            pl.BlockSpec((tm, MLA_WIDTH), row), _full(gg.shape), _full(wout.shape), _full(pg.shape),
            pl.BlockSpec((tm, D_MODEL), row),
        ],
        out_specs=pl.BlockSpec((tm, D_MODEL), row),
        out_shape=jax.ShapeDtypeStruct((T, D_MODEL), F32),
        compiler_params=_params("parallel"),
        name="out_proj",
    )(oa, ob, oc, gg, wout, pg, x2)


def _ffn_kernel(x_ref, g_ref, wup_ref, cw_ref, cb_ref, wdown_ref, pg_ref, o_ref,
                hbuf, ubuf, acc_ref, carry_ref):
    tm = x_ref.shape[0]
    hbuf[...] = _rms(x_ref[...], g_ref[...]).astype(BF16)

    @pl.when(pl.program_id(1) == 0)
    def _():
        carry_ref[...] = jnp.zeros_like(carry_ref)

    row8 = lax.broadcasted_iota(jnp.int32, (8, FF_CHUNK), 0)

    def columns(c, part):
        return slice(part * D_FF + c * FF_CHUNK, part * D_FF + (c + 1) * FF_CHUNK)

    def up_proj(c, slot):
        for part in range(2):
            ubuf[slot, part] = jnp.dot(hbuf[...], wup_ref[:, columns(c, part)], preferred_element_type=F32)

    def conv(c, slot, part):
        cols = columns(c, part)
        u = ubuf[slot, part]
        prev = carry_ref[:, cols]
        carry_ref[:, cols] = u[tm - 8:tm, :]
        w = cw_ref[:, cols]
        y = cb_ref[:, cols] + w[2:3, :] * u
        for d in (1, 2):
            r = pltpu.roll(u, d, axis=0)
            head = jnp.where(row8 < d, pltpu.roll(prev, d, axis=0), r[0:8, :])
            r = jnp.concatenate([head, r[8:, :]], axis=0)
            y = y + w[2 - d:3 - d, :] * r
        return y

    def conv_act_down(c, slot):
        act = (jax.nn.gelu(conv(c, slot, 0), approximate=True) * conv(c, slot, 1)).astype(BF16)
        down = jnp.dot(act, wdown_ref[c * FF_CHUNK:(c + 1) * FF_CHUNK, :], preferred_element_type=F32)
        if c == 0:
            acc_ref[...] = down
        else:
            acc_ref[...] += down

    up_proj(0, 0)
    for c in range(N_FF_CHUNKS):
        if c + 1 < N_FF_CHUNKS:
            up_proj(c + 1, (c + 1) % 2)
        conv_act_down(c, c % 2)
    o_ref[...] = x_ref[...] + _rms(acc_ref[...], pg_ref[...])


def _ffn(x2, g, wup, cw, cb, wdown, pg, batch, seq):
    T = x2.shape[0]
    tm = ROW_TILE
    ns = seq // tm
    row = lambda b, s: (b * ns + s, 0)
    return pl.pallas_call(
        _ffn_kernel,
        grid=(batch, ns),
        in_specs=[
            pl.BlockSpec((tm, D_MODEL), row), _full(g.shape), _full(wup.shape), _full(cw.shape),
            _full(cb.shape), _full(wdown.shape), _full(pg.shape),
        ],
        out_specs=pl.BlockSpec((tm, D_MODEL), row),
        out_shape=jax.ShapeDtypeStruct((T, D_MODEL), F32),
        scratch_shapes=[
            pltpu.VMEM((tm, D_MODEL), BF16), pltpu.VMEM((2, 2, tm, FF_CHUNK), F32),
            pltpu.VMEM((tm, D_MODEL), F32), pltpu.VMEM((8, 2 * D_FF), F32),
        ],
        compiler_params=_params("parallel", "arbitrary"),
        name="conv_ffn",
    )(x2, g, wup, cw, cb, wdown, pg)


def _t5_causal_bucket(dist):
    max_exact = REL_BUCKETS // 2
    d = jnp.maximum(dist, 0)
    log_ratio = jnp.log(jnp.maximum(d, 1).astype(F32) / max_exact) / math.log(REL_MAX_DIST / max_exact)
    large = max_exact + (log_ratio * (REL_BUCKETS - max_exact)).astype(jnp.int32)
    large = jnp.minimum(large, REL_BUCKETS - 1)
    return jnp.where(d < max_exact, d, large)


def _swa_bias_table(rel_bias):
    qi = jnp.arange(WINDOW, dtype=jnp.int32)[:, None] + WINDOW
    kj = jnp.arange(2 * WINDOW, dtype=jnp.int32)[None, :]
    dist = qi - kj
    in_band = (dist >= 0) & (dist < WINDOW)
    hit = _t5_causal_bucket(dist)[None, :, :] == jnp.arange(REL_BUCKETS, dtype=jnp.int32)[:, None, None]
    bias = jnp.sum(jnp.where(hit[None], rel_bias.astype(F32).T[:, :, None, None], 0.0), axis=1)
    bias = jnp.where(in_band[None], bias, NEG_INF)
    return bias.reshape(SWA_KV_HEADS, (SWA_Q_HEADS // SWA_KV_HEADS) * WINDOW, 2 * WINDOW)


def _rope_tables(seq):
    pos = jnp.arange(seq, dtype=F32)
    inv_freq = ROPE_THETA ** (-(jnp.arange(MLA_ROPE_DIM // 2, dtype=F32) * 2.0 / MLA_ROPE_DIM))
    ang = pos[:, None] * inv_freq[None, :]
    cos, sin = jnp.cos(ang), jnp.sin(ang)
    pad = jnp.zeros((seq, LANES - MLA_QK_DIM), F32)
    cos_t = jnp.concatenate([jnp.ones((seq, MLA_NOPE_DIM), F32), cos, cos, pad], axis=1)
    sin_t = jnp.concatenate([jnp.zeros((seq, MLA_NOPE_DIM), F32), -sin, sin, pad], axis=1)
    return cos_t, sin_t


def _swap_halves(w):
    half = w.shape[1] // 2
    return jnp.concatenate([w[:, half:], w[:, :half]], axis=1)


def _layer_params(l, w_in, forget_bias, w_uq, w_ukv, group_norm, w_out, w_up, conv_w, conv_b, w_down):
    wi = w_in[l]
    d = wi.shape[0]
    order = jnp.asarray(SWA_HEAD_ORDER)
    swa_q = wi[:, 0:SWA_WIDTH].reshape(d, SWA_Q_HEADS, HEAD_DIM)[:, order].reshape(d, SWA_WIDTH)
    wmain = jnp.concatenate([swa_q, wi[:, SWA_WIDTH:SWA_COLS + 3 * FOX_WIDTH]], axis=1).astype(BF16)

    f0 = SWA_COLS + 3 * FOX_WIDTH
    m0 = SWA_COLS + FOX_COLS
    z = lambda n: jnp.zeros((d, n), F32)
    w_kr = wi[:, m0 + MLA_Q_RANK + MLA_KV_RANK:]
    rope_pad = LANES - MLA_QK_DIM
    wside = jnp.concatenate([
        wi[:, m0:m0 + MLA_Q_RANK + MLA_KV_RANK],
        z(MLA_NOPE_DIM), w_kr, z(rope_pad),
        z(MLA_NOPE_DIM), _swap_halves(w_kr), z(rope_pad),
        wi[:, f0:f0 + FOX_HEADS], z(LANES - FOX_HEADS),
    ], axis=1).astype(BF16)
    fb = jnp.pad(forget_bias[l], (0, LANES - FOX_HEADS))[None, :]

    uq = w_uq[l].reshape(MLA_Q_RANK, MLA_HEADS, MLA_QK_DIM)
    zq = jnp.zeros((MLA_Q_RANK, MLA_HEADS, 1), F32)
    plain = jnp.concatenate([uq, jnp.tile(zq, (1, 1, rope_pad))], axis=2)
    rope = uq[:, :, MLA_NOPE_DIM:]
    swapped = jnp.concatenate([jnp.tile(zq, (1, 1, MLA_NOPE_DIM)), rope[:, :, MLA_ROPE_DIM // 2:],
                               rope[:, :, :MLA_ROPE_DIM // 2], jnp.tile(zq, (1, 1, rope_pad))], axis=2)
    wuq = jnp.concatenate([plain.reshape(MLA_Q_RANK, -1), swapped.reshape(MLA_Q_RANK, -1)], axis=1).astype(BF16)

    ukv = w_ukv[l].reshape(MLA_KV_RANK, MLA_HEADS, MLA_NOPE_DIM + MLA_V_DIM)
    k_slabs = jnp.concatenate([ukv[:, :, :MLA_NOPE_DIM],
                               jnp.zeros((MLA_KV_RANK, MLA_HEADS, LANES - MLA_NOPE_DIM), F32)], axis=2)
    wukv = jnp.concatenate([k_slabs.reshape(MLA_KV_RANK, -1),
                            ukv[:, :, MLA_NOPE_DIM:].reshape(MLA_KV_RANK, -1)], axis=1).astype(BF16)

    gg = group_norm[l]
    gg = jnp.concatenate([gg[:SWA_WIDTH].reshape(SWA_Q_HEADS, HEAD_DIM)[order].reshape(-1), gg[SWA_WIDTH:]])[None, :]
    wo = w_out[l]
    wo = jnp.concatenate([wo[:SWA_WIDTH].reshape(SWA_Q_HEADS, HEAD_DIM, -1)[order].reshape(SWA_WIDTH, -1),
                          wo[SWA_WIDTH:]], axis=0).astype(BF16)

    return wmain, wside, fb, wuq, wukv, gg, wo, w_up[l].astype(BF16), conv_w[l], conv_b[l][None, :], \
        w_down[l].astype(BF16)


def kernel(x, attn_pre_norm, w_in, forget_bias, swa_sinks, rel_bias, q_latent_norm, w_uq, kv_latent_norm, w_ukv,
           group_norm, w_out, attn_post_norm, ffn_pre_norm, w_up, conv_w, conv_b, w_down, ffn_post_norm):
    batch, seq, d = x.shape
    assert d == D_MODEL and seq % ROW_TILE == 0 and seq % ATTN_TILE == 0
    depth = w_in.shape[0]
    cos_t, sin_t = _rope_tables(seq)
    bias = _swa_bias_table(rel_bias)
    x2 = x.reshape(batch * seq, d)
    for l in range(depth):
        wmain, wside, fb, wuq, wukv, gg, wo, wup, cw, cb, wdown = _layer_params(
            l, w_in, forget_bias, w_uq, w_ukv, group_norm, w_out, w_up, conv_w, conv_b, w_down)
        sinks = jnp.repeat(swa_sinks[l].astype(F32).reshape(SWA_KV_HEADS, -1), WINDOW, axis=1)
        sinks = jnp.broadcast_to(sinks[:, :, None], sinks.shape + (LANES,))
        (swa_q, swa_k, swa_v, fox_q, fox_k, fox_v, mla_q, mla_k, mla_v) = _in_proj(
            x2, attn_pre_norm[l][None, :], wmain, wside, fb, q_latent_norm[l][None, :], wuq,
            kv_latent_norm[l][None, :], wukv, cos_t, sin_t, batch, seq)
        out_a = _swa(swa_q, swa_k, swa_v, bias, sinks, batch, seq)
        out_b = _flash(fox_q, fox_k, fox_v, batch, seq, 1.0, "fox_attn")
        out_c = _flash(mla_q, mla_k, mla_v, batch, seq, MLA_QK_DIM ** -0.5, "mla_attn")
        x2 = _out_proj(out_a, out_b, out_c, gg, wo, attn_post_norm[l][None, :], x2)
        x2 = _ffn(x2, ffn_pre_norm[l][None, :], wup, cw, cb, wdown, ffn_post_norm[l][None, :], batch, seq)
    return x2.reshape(batch, seq, d)
```

```python
import functools
import math

import jax
import jax.numpy as jnp
from jax import lax
from jax.experimental import pallas as pl
from jax.experimental.pallas import tpu as pltpu

D_MODEL = 1024
HEAD_DIM = 64
SWA_Q_HEADS = 8
SWA_KV_HEADS = 2
WINDOW = 128
FOX_HEADS = 4
MLA_HEADS = 4
MLA_Q_RANK = 256
MLA_KV_RANK = 128
MLA_NOPE_DIM = 64
MLA_ROPE_DIM = 32
MLA_V_DIM = 64
MLA_QK_DIM = MLA_NOPE_DIM + MLA_ROPE_DIM
ROPE_THETA = 10000.0
REL_BUCKETS = 32
REL_MAX_DIST = 128
D_FF = 2816
EPS = 1e-6
NEG_INF = -1e30

SWA_WIDTH = SWA_Q_HEADS * HEAD_DIM
FOX_WIDTH = FOX_HEADS * HEAD_DIM
MLA_WIDTH = MLA_HEADS * MLA_V_DIM
SWA_COLS = (SWA_Q_HEADS + 2 * SWA_KV_HEADS) * HEAD_DIM
FOX_COLS = 3 * FOX_WIDTH + FOX_HEADS

LANES = 128
ROW_TILE = 512
ATTN_TILE = 512
FF_CHUNK = 256
N_FF_CHUNKS = D_FF // FF_CHUNK
VMEM_LIMIT = 56 * 1024 * 1024

SWA_HEAD_ORDER = (0, 4, 1, 5, 2, 6, 3, 7)

F32 = jnp.float32
BF16 = jnp.bfloat16
NT_DIMS = (((1,), (1,)), ((), ()))
LOG2E = math.log2(math.e)


def _rms(x, g):
    ms = jnp.mean(x * x, axis=-1, keepdims=True)
    return x * lax.rsqrt(ms + EPS) * g


def _params(*sem):
    return pltpu.CompilerParams(dimension_semantics=sem, vmem_limit_bytes=VMEM_LIMIT)


def _full(shape):
    nd = len(shape)
    return pl.BlockSpec(shape, lambda *_: (0,) * nd)


def _in_proj_kernel(x_ref, g_ref, wmain_ref, wside_ref, fb_ref, qg_ref, wuq_ref, kvg_ref, wukv_ref,
                    cos_ref, sin_ref,
                    swaq_ref, swak_ref, swav_ref, foxq_ref, foxk_ref, foxv_ref,
                    mlaq_ref, mlak_ref, mlav_ref, carry_ref):
    tm = x_ref.shape[0]
    h = _rms(x_ref[...], g_ref[...]).astype(BF16)

    main = jnp.dot(h, wmain_ref[...], preferred_element_type=F32)
    scale = HEAD_DIM ** -0.5
    swaq_ref[...] = (main[:, 0:512] * scale).astype(BF16)
    swak_ref[...] = main[:, 512:640].astype(BF16)
    swav_ref[...] = main[:, 640:768].astype(BF16)
    foxv_ref[...] = main[:, 1280:1536].astype(BF16)

    side = jnp.dot(h, wside_ref[...], preferred_element_type=F32)
    cos = cos_ref[...]
    sin = sin_ref[...]

    cqn = _rms(side[:, 0:256], qg_ref[...]).astype(BF16)
    qa = jnp.dot(cqn, wuq_ref[...], preferred_element_type=F32)
    for hh in range(MLA_HEADS):
        lo = hh * LANES
        a = qa[:, lo:lo + LANES]
        b = qa[:, 512 + lo:512 + lo + LANES]
        mlaq_ref[:, lo:lo + LANES] = ((a * cos + b * sin) * (MLA_QK_DIM ** -0.5 * LOG2E)).astype(BF16)

    ckvn = _rms(side[:, 256:384], kvg_ref[...]).astype(BF16)
    kv = jnp.dot(ckvn, wukv_ref[...], preferred_element_type=F32)
    krot = side[:, 384:512] * cos + side[:, 512:640] * sin
    for hh in range(MLA_HEADS):
        lo = hh * LANES
        mlak_ref[:, lo:lo + LANES] = (kv[:, lo:lo + LANES] + krot).astype(BF16)
    mlav_ref[...] = kv[:, 512:768].astype(BF16)

    lane = lax.broadcasted_iota(jnp.int32, (tm, LANES), 1)
    row = lax.broadcasted_iota(jnp.int32, (tm, LANES), 0)
    c = jnp.where(lane < FOX_HEADS, jax.nn.log_sigmoid(side[:, 640:768] + fb_ref[...]), 0.0)
    shift = 1
    while shift < tm:
        c = c + jnp.where(row >= shift, pltpu.roll(c, shift, axis=0), 0.0)
        shift *= 2

    @pl.when(pl.program_id(1) == 0)
    def _():
        carry_ref[...] = jnp.zeros_like(carry_ref)

    c = c + carry_ref[0:1, :]
    carry_ref[...] = jnp.broadcast_to(c[tm - 1:tm, :], carry_ref.shape)

    for hd in range(FOX_HEADS):
        pair, half = hd // 2, hd % 2
        o = 0 if half else HEAD_DIM
        f = jnp.broadcast_to(jnp.sum(jnp.where(lane == hd, c, 0.0), axis=1, keepdims=True), c.shape) * LOG2E
        hi = f.astype(BF16).astype(F32)
        mid = (f - hi).astype(BF16).astype(F32)
        low = (f - hi) - mid
        one_k = jnp.where(jnp.logical_and(lane >= o + 3, lane < o + 6), 1.0, 0.0)
        one_q = jnp.where(jnp.logical_and(lane >= o, lane < o + 3), 1.0, 0.0)
        extra_k = jnp.where(lane == o, -hi, jnp.where(lane == o + 1, -mid, jnp.where(lane == o + 2, -low, one_k)))
        extra_q = jnp.where(lane == o + 3, hi, jnp.where(lane == o + 4, mid, jnp.where(lane == o + 5, low, one_q)))
        data = _half_lanes(c.shape, half)
        slab = slice(hd * LANES, (hd + 1) * LANES)
        q_pair = main[:, 768 + pair * LANES:768 + (pair + 1) * LANES] * (scale * LOG2E)
        k_pair = main[:, 1024 + pair * LANES:1024 + (pair + 1) * LANES]
        foxq_ref[:, slab] = jnp.where(data, q_pair, extra_q).astype(BF16)
        foxk_ref[:, slab] = jnp.where(data, k_pair, extra_k).astype(BF16)


def _in_proj(x2, g, wmain, wside, fb, qg, wuq, kvg, wukv, cos_t, sin_t, batch, seq):
    T = x2.shape[0]
    tm = ROW_TILE
    ns = seq // tm
    row = lambda b, s: (b * ns + s, 0)
    seqrow = lambda b, s: (s, 0)
    out_shape = [
        jax.ShapeDtypeStruct((T, 512), BF16), jax.ShapeDtypeStruct((T, 128), BF16),
        jax.ShapeDtypeStruct((T, 128), BF16), jax.ShapeDtypeStruct((T, 512), BF16),
        jax.ShapeDtypeStruct((T, 512), BF16), jax.ShapeDtypeStruct((T, 256), BF16),
        jax.ShapeDtypeStruct((T, 512), BF16), jax.ShapeDtypeStruct((T, 512), BF16),
        jax.ShapeDtypeStruct((T, 256), BF16),
    ]
    out_specs = [
        pl.BlockSpec((tm, 512), row), pl.BlockSpec((tm, 128), row), pl.BlockSpec((tm, 128), row),
        pl.BlockSpec((tm, 512), row), pl.BlockSpec((tm, 512), row), pl.BlockSpec((tm, 256), row),
        pl.BlockSpec((tm, 512), row), pl.BlockSpec((tm, 512), row), pl.BlockSpec((tm, 256), row),
    ]
    in_specs = [
        pl.BlockSpec((tm, D_MODEL), row), _full(g.shape), _full(wmain.shape), _full(wside.shape),
        _full(fb.shape), _full(qg.shape), _full(wuq.shape), _full(kvg.shape), _full(wukv.shape),
        pl.BlockSpec((tm, LANES), seqrow), pl.BlockSpec((tm, LANES), seqrow),
    ]
    return pl.pallas_call(
        _in_proj_kernel,
        grid=(batch, ns),
        in_specs=in_specs,
        out_specs=out_specs,
        out_shape=out_shape,
        scratch_shapes=[pltpu.VMEM((8, LANES), F32)],
        compiler_params=_params("arbitrary", "arbitrary"),
        name="in_proj",
    )(x2, g, wmain, wside, fb, qg, wuq, kvg, wukv, cos_t, sin_t)


def _half_lanes(shape, hh):
    lane = lax.broadcasted_iota(jnp.int32, shape, len(shape) - 1)
    return (lane < HEAD_DIM) if hh == 0 else (lane >= HEAD_DIM)


def _ones_in_other_half(v2, hh):
    return jnp.where(_half_lanes(v2.shape, hh), v2, jnp.ones_like(v2))


def _swa_kernel(q_ref, kc_ref, kp_ref, vc_ref, vp_ref, bias_ref, sink_ref, o_ref, kbuf, vbuf):
    first_tile = pl.program_id(1) == 0
    nwin = q_ref.shape[0] // WINDOW
    npair = SWA_Q_HEADS // 2
    kbuf[0:WINDOW, :] = kp_ref[...]
    kbuf[WINDOW:, :] = kc_ref[...]
    vbuf[0:WINDOW, :] = vp_ref[...]
    vbuf[WINDOW:, :] = vc_ref[...]
    kj = lax.broadcasted_iota(jnp.int32, (1, 2 * WINDOW), 1)
    no_prev = jnp.where(jnp.logical_and(kj < WINDOW, first_tile), NEG_INF, 0.0)
    low = _half_lanes((WINDOW, LANES), 0)
    for w in range(nwin):
        kw = kbuf[w * WINDOW:(w + 2) * WINDOW, :]
        vw = vbuf[w * WINDOW:(w + 2) * WINDOW, :]
        rows = slice(w * WINDOW, (w + 1) * WINDOW)
        q4 = [q_ref[rows, p * LANES:(p + 1) * LANES] for p in range(npair)]
        outs = []
        for hh in range(2):
            mine = _half_lanes((WINDOW, LANES), hh)
            qs = jnp.concatenate([jnp.where(mine, q, jnp.zeros_like(q)) for q in q4], axis=0)
            s = lax.dot_general(qs, kw, NT_DIMS, preferred_element_type=F32) + bias_ref[hh]
            if w == 0:
                s = s + no_prev
            sink = sink_ref[hh]
            s0, s1 = s[:, 0:LANES], s[:, LANES:]
            mloc = jnp.max(jnp.maximum(s0, s1), axis=1, keepdims=True)
            m = jnp.maximum(jnp.broadcast_to(mloc, sink.shape), sink)
            e = jnp.concatenate([jnp.exp(s0 - m), jnp.exp(s1 - m)], axis=1).astype(BF16)
            pv = jnp.dot(e, _ones_in_other_half(vw, hh), preferred_element_type=F32)
            den = pltpu.roll(pv, HEAD_DIM, axis=1) + jnp.exp(sink - m)
            outs.append(pv / den)
        for p in range(npair):
            pr = slice(p * WINDOW, (p + 1) * WINDOW)
            o_ref[rows, p * LANES:(p + 1) * LANES] = jnp.where(low, outs[0][pr], outs[1][pr])


def _swa(q, k, v, bias, sinks, batch, seq):
    T = q.shape[0]
    tm = ROW_TILE
    ns = seq // tm
    per = tm // WINDOW
    row = lambda b, s: (b * ns + s, 0)
    prev = lambda b, s: (jnp.maximum((b * ns + s) * per - 1, 0), 0)
    return pl.pallas_call(
        _swa_kernel,
        grid=(batch, ns),
        in_specs=[
            pl.BlockSpec((tm, SWA_WIDTH), row),
            pl.BlockSpec((tm, LANES), row), pl.BlockSpec((WINDOW, LANES), prev),
            pl.BlockSpec((tm, LANES), row), pl.BlockSpec((WINDOW, LANES), prev),
            _full(bias.shape), _full(sinks.shape),
        ],
        out_specs=pl.BlockSpec((tm, SWA_WIDTH), row),
        out_shape=jax.ShapeDtypeStruct((T, SWA_WIDTH), F32),
        scratch_shapes=[pltpu.VMEM((tm + WINDOW, LANES), BF16), pltpu.VMEM((tm + WINDOW, LANES), BF16)],
        compiler_params=_params("parallel", "arbitrary"),
        name="swa_attn",
    )(q, k, k, v, v, bias, sinks)


def _flash_init(m_sc, acc_sc):
    m_sc[...] = jnp.full(m_sc.shape, NEG_INF, F32)
    acc_sc[...] = jnp.zeros(acc_sc.shape, F32)


def _flash_update(s_tiles, v2, head, m_sc, acc_sc):
    mloc = functools.reduce(jnp.maximum, s_tiles)
    mloc = jnp.broadcast_to(jnp.max(mloc, axis=1, keepdims=True), mloc.shape)
    m_prev = m_sc[head]
    m_new = jnp.maximum(m_prev, mloc)
    m_sc[head] = m_new
    alpha = jnp.exp2(m_prev - m_new)
    p = jnp.concatenate([jnp.exp2(s - m_new) for s in s_tiles], axis=1).astype(BF16)
    pv = jnp.dot(p, _ones_in_other_half(v2, head % 2), preferred_element_type=F32)
    acc_sc[head] = alpha * acc_sc[head] + pv


def _flash_finish(o_ref, acc_sc):
    low = _half_lanes(acc_sc.shape[1:], 0)
    for pair in range(acc_sc.shape[0] // 2):
        outs = [acc_sc[2 * pair + hh] / pltpu.roll(acc_sc[2 * pair + hh], HEAD_DIM, axis=1) for hh in range(2)]
        o_ref[:, pair * LANES:(pair + 1) * LANES] = jnp.where(low, outs[0], outs[1])


def _lane_tiles(s):
    return [s[:, j * LANES:(j + 1) * LANES] for j in range(s.shape[1] // LANES)]


def _below_diagonal(tq, tk):
    r = lax.broadcasted_iota(jnp.int32, (tq, LANES), 0)
    c = lax.broadcasted_iota(jnp.int32, (tq, LANES), 1)
    return [c + j * LANES <= r for j in range(tk // LANES)]


def _flash_kernel(qi_ref, ki_ref, q_ref, k_ref, v_ref, o_ref, m_sc, acc_sc):
    t = pl.program_id(1)
    qi = qi_ref[t]
    ki = ki_ref[t]
    tq, tk = q_ref.shape[0], k_ref.shape[0]
    nheads = q_ref.shape[1] // LANES

    @pl.when(ki == 0)
    def _():
        _flash_init(m_sc, acc_sc)

    def step(diagonal):
        for head in range(nheads):
            slab = slice(head * LANES, (head + 1) * LANES)
            pair = slice((head // 2) * LANES, (head // 2 + 1) * LANES)
            s = lax.dot_general(q_ref[:, slab], k_ref[:, slab], NT_DIMS, preferred_element_type=F32)
            tiles = _lane_tiles(s)
            if diagonal:
                tiles = [jnp.where(keep, x, NEG_INF) for keep, x in zip(_below_diagonal(tq, tk), tiles)]
            _flash_update(tiles, v_ref[:, pair], head, m_sc, acc_sc)

    @pl.when(ki < qi)
    def _():
        step(False)

    @pl.when(ki == qi)
    def _():
        step(True)
        _flash_finish(o_ref, acc_sc)


def _tri_tables(nq):
    qi = [q for q in range(nq) for _ in range(q + 1)]
    ki = [k for q in range(nq) for k in range(q + 1)]
    return jnp.asarray(qi, jnp.int32), jnp.asarray(ki, jnp.int32)


def _flash(q, k, v, batch, seq, name):
    T = q.shape[0]
    nheads = q.shape[1] // LANES
    tq = ATTN_TILE
    nq = seq // tq
    qi_tab, ki_tab = _tri_tables(nq)
    qrow = lambda b, t, qi, ki: (b * nq + qi[t], 0)
    krow = lambda b, t, qi, ki: (b * nq + ki[t], 0)
    grid_spec = pltpu.PrefetchScalarGridSpec(
        num_scalar_prefetch=2,
        grid=(batch, qi_tab.shape[0]),
        in_specs=[
            pl.BlockSpec((tq, q.shape[1]), qrow), pl.BlockSpec((tq, k.shape[1]), krow),
            pl.BlockSpec((tq, v.shape[1]), krow),
        ],
        out_specs=pl.BlockSpec((tq, v.shape[1]), qrow),
        scratch_shapes=[pltpu.VMEM((nheads, tq, LANES), F32),
                        pltpu.VMEM((nheads, tq, LANES), F32)],
    )
    return pl.pallas_call(
        _flash_kernel,
        grid_spec=grid_spec,
        out_shape=jax.ShapeDtypeStruct((T, v.shape[1]), F32),
        compiler_params=_params("parallel", "arbitrary"),
        name=name,
    )(qi_tab, ki_tab, q, k, v)


def _mix_ffn_kernel(oa_ref, ob_ref, oc_ref, gg_ref, wout_ref, apg_ref, x_ref,
                    g_ref, wup_ref, cw_ref, cb_ref, wdown_ref, pg_ref, o_ref,
                    xbuf, hbuf, ubuf, acc_ref, carry_ref):
    tm = x_ref.shape[0]

    gg = gg_ref[...]
    a = _rms(oa_ref[...], gg[:, 0:SWA_WIDTH]).astype(BF16)
    b = _rms(ob_ref[...], gg[:, SWA_WIDTH:SWA_WIDTH + FOX_WIDTH]).astype(BF16)
    c = _rms(oc_ref[...], gg[:, SWA_WIDTH + FOX_WIDTH:]).astype(BF16)
    y = jnp.dot(a, wout_ref[0:SWA_WIDTH, :], preferred_element_type=F32)
    y = y + jnp.dot(b, wout_ref[SWA_WIDTH:SWA_WIDTH + FOX_WIDTH, :], preferred_element_type=F32)
    y = y + jnp.dot(c, wout_ref[SWA_WIDTH + FOX_WIDTH:, :], preferred_element_type=F32)
    x1 = x_ref[...] + _rms(y, apg_ref[...])
    xbuf[...] = x1
    hbuf[...] = _rms(x1, g_ref[...]).astype(BF16)

    @pl.when(pl.program_id(1) == 0)
    def _():
        carry_ref[...] = jnp.zeros_like(carry_ref)

    row8 = lax.broadcasted_iota(jnp.int32, (8, 2 * FF_CHUNK), 0)

    def up_proj(c, slot):
        ubuf[slot] = jnp.dot(hbuf[...], wup_ref[c], preferred_element_type=F32)

    def conv_act_down(c, slot, first=False):
        u = ubuf[slot]
        prev = carry_ref[c]
        carry_ref[c] = u[tm - 8:tm, :]
        w = cw_ref[c]
        y = cb_ref[c] + w[2:3, :] * u
        for d in (1, 2):
            r = pltpu.roll(u, d, axis=0)
            head = jnp.where(row8 < d, pltpu.roll(prev, d, axis=0), r[0:8, :])
            r = jnp.concatenate([head, r[8:, :]], axis=0)
            y = y + w[2 - d:3 - d, :] * r
        act = (jax.nn.gelu(y[:, 0:FF_CHUNK], approximate=True) * y[:, FF_CHUNK:]).astype(BF16)
        down = jnp.dot(act, wdown_ref[c], preferred_element_type=F32)
        if first:
            acc_ref[...] = down
        else:
            acc_ref[...] += down

    up_proj(0, 0)
    up_proj(1, 1)
    conv_act_down(0, 0, first=True)

    def two_chunks(j, _):
        c = 2 * j + 1
        up_proj(c + 1, 0)
        conv_act_down(c, 1)
        up_proj(c + 2, 1)
        conv_act_down(c + 1, 0)
        return 0

    lax.fori_loop(0, (N_FF_CHUNKS - 3) // 2, two_chunks, 0)
    up_proj(N_FF_CHUNKS - 1, 0)
    conv_act_down(N_FF_CHUNKS - 2, 1)
    conv_act_down(N_FF_CHUNKS - 1, 0)
    o_ref[...] = xbuf[...] + _rms(acc_ref[...], pg_ref[...])


def _mix_ffn(oa, ob, oc, gg, wout, apg, x2, g, wup, cw, cb, wdown, pg, batch, seq):
    T = x2.shape[0]
    tm = ROW_TILE
    ns = seq // tm
    row = lambda b, s: (b * ns + s, 0)
    return pl.pallas_call(
        _mix_ffn_kernel,
        grid=(batch, ns),
        in_specs=[
            pl.BlockSpec((tm, SWA_WIDTH), row), pl.BlockSpec((tm, FOX_WIDTH), row),
            pl.BlockSpec((tm, MLA_WIDTH), row), _full(gg.shape), _full(wout.shape), _full(apg.shape),
            pl.BlockSpec((tm, D_MODEL), row), _full(g.shape), _full(wup.shape), _full(cw.shape),
            _full(cb.shape), _full(wdown.shape), _full(pg.shape),
        ],
        out_specs=pl.BlockSpec((tm, D_MODEL), row),
        out_shape=jax.ShapeDtypeStruct((T, D_MODEL), F32),
        scratch_shapes=[
            pltpu.VMEM((tm, D_MODEL), F32), pltpu.VMEM((tm, D_MODEL), BF16),
            pltpu.VMEM((2, tm, 2 * FF_CHUNK), F32), pltpu.VMEM((tm, D_MODEL), F32),
            pltpu.VMEM((N_FF_CHUNKS, 8, 2 * FF_CHUNK), F32),
        ],
        compiler_params=_params("parallel", "arbitrary"),
        name="mix_ffn",
    )(oa, ob, oc, gg, wout, apg, x2, g, wup, cw, cb, wdown, pg)


def _t5_causal_bucket(dist):
    max_exact = REL_BUCKETS // 2
    d = jnp.maximum(dist, 0)
    log_ratio = jnp.log(jnp.maximum(d, 1).astype(F32) / max_exact) / math.log(REL_MAX_DIST / max_exact)
    large = max_exact + (log_ratio * (REL_BUCKETS - max_exact)).astype(jnp.int32)
    large = jnp.minimum(large, REL_BUCKETS - 1)
    return jnp.where(d < max_exact, d, large)


def _swa_bias_table(rel_bias):
    qi = jnp.arange(WINDOW, dtype=jnp.int32)[:, None] + WINDOW
    kj = jnp.arange(2 * WINDOW, dtype=jnp.int32)[None, :]
    dist = qi - kj
    in_band = (dist >= 0) & (dist < WINDOW)
    hit = _t5_causal_bucket(dist)[None, :, :] == jnp.arange(REL_BUCKETS, dtype=jnp.int32)[:, None, None]
    bias = jnp.sum(jnp.where(hit[None], rel_bias.astype(F32).T[:, :, None, None], 0.0), axis=1)
    bias = jnp.where(in_band[None], bias, NEG_INF)
    return bias.reshape(SWA_KV_HEADS, (SWA_Q_HEADS // SWA_KV_HEADS) * WINDOW, 2 * WINDOW)


def _rope_tables(seq):
    pos = jnp.arange(seq, dtype=F32)
    inv_freq = ROPE_THETA ** (-(jnp.arange(MLA_ROPE_DIM // 2, dtype=F32) * 2.0 / MLA_ROPE_DIM))
    ang = pos[:, None] * inv_freq[None, :]
    cos, sin = jnp.cos(ang), jnp.sin(ang)
    pad = jnp.zeros((seq, LANES - MLA_QK_DIM), F32)
    cos_t = jnp.concatenate([jnp.ones((seq, MLA_NOPE_DIM), F32), cos, cos, pad], axis=1)
    sin_t = jnp.concatenate([jnp.zeros((seq, MLA_NOPE_DIM), F32), -sin, sin, pad], axis=1)
    return cos_t, sin_t


def _swap_halves(w):
    half = w.shape[1] // 2
    return jnp.concatenate([w[:, half:], w[:, :half]], axis=1)


def _layer_params(l, w_in, forget_bias, w_uq, w_ukv, group_norm, w_out, w_up, conv_w, conv_b, w_down):
    wi = w_in[l]
    d = wi.shape[0]
    order = jnp.asarray(SWA_HEAD_ORDER)
    swa_q = wi[:, 0:SWA_WIDTH].reshape(d, SWA_Q_HEADS, HEAD_DIM)[:, order].reshape(d, SWA_WIDTH)
    wmain = jnp.concatenate([swa_q, wi[:, SWA_WIDTH:SWA_COLS + 3 * FOX_WIDTH]], axis=1).astype(BF16)

    f0 = SWA_COLS + 3 * FOX_WIDTH
    m0 = SWA_COLS + FOX_COLS
    z = lambda n: jnp.zeros((d, n), F32)
    w_kr = wi[:, m0 + MLA_Q_RANK + MLA_KV_RANK:]
    rope_pad = LANES - MLA_QK_DIM
    wside = jnp.concatenate([
        wi[:, m0:m0 + MLA_Q_RANK + MLA_KV_RANK],
        z(MLA_NOPE_DIM), w_kr, z(rope_pad),
        z(MLA_NOPE_DIM), _swap_halves(w_kr), z(rope_pad),
        wi[:, f0:f0 + FOX_HEADS], z(LANES - FOX_HEADS),
    ], axis=1).astype(BF16)
    fb = jnp.pad(forget_bias[l], (0, LANES - FOX_HEADS))[None, :]

    uq = w_uq[l].reshape(MLA_Q_RANK, MLA_HEADS, MLA_QK_DIM)
    zq = jnp.zeros((MLA_Q_RANK, MLA_HEADS, 1), F32)
    plain = jnp.concatenate([uq, jnp.tile(zq, (1, 1, rope_pad))], axis=2)
    rope = uq[:, :, MLA_NOPE_DIM:]
    swapped = jnp.concatenate([jnp.tile(zq, (1, 1, MLA_NOPE_DIM)), rope[:, :, MLA_ROPE_DIM // 2:],
                               rope[:, :, :MLA_ROPE_DIM // 2], jnp.tile(zq, (1, 1, rope_pad))], axis=2)
    wuq = jnp.concatenate([plain.reshape(MLA_Q_RANK, -1), swapped.reshape(MLA_Q_RANK, -1)], axis=1).astype(BF16)

    ukv = w_ukv[l].reshape(MLA_KV_RANK, MLA_HEADS, MLA_NOPE_DIM + MLA_V_DIM)
    k_slabs = jnp.concatenate([ukv[:, :, :MLA_NOPE_DIM],
                               jnp.zeros((MLA_KV_RANK, MLA_HEADS, LANES - MLA_NOPE_DIM), F32)], axis=2)
    wukv = jnp.concatenate([k_slabs.reshape(MLA_KV_RANK, -1),
                            ukv[:, :, MLA_NOPE_DIM:].reshape(MLA_KV_RANK, -1)], axis=1).astype(BF16)

    gg = group_norm[l]
    gg = jnp.concatenate([gg[:SWA_WIDTH].reshape(SWA_Q_HEADS, HEAD_DIM)[order].reshape(-1), gg[SWA_WIDTH:]])[None, :]
    wo = w_out[l]
    wo = jnp.concatenate([wo[:SWA_WIDTH].reshape(SWA_Q_HEADS, HEAD_DIM, -1)[order].reshape(SWA_WIDTH, -1),
                          wo[SWA_WIDTH:]], axis=0).astype(BF16)

    def gate_up(a):
        lead = a.shape[:-1]
        a = a.reshape(lead + (2, N_FF_CHUNKS, FF_CHUNK))
        a = jnp.moveaxis(a, -2, 0)
        return a.reshape((N_FF_CHUNKS,) + lead + (2 * FF_CHUNK,))

    wup = gate_up(w_up[l].astype(BF16))
    cw = gate_up(conv_w[l])
    cb = gate_up(conv_b[l][None, :])
    wdown = w_down[l].astype(BF16).reshape(N_FF_CHUNKS, FF_CHUNK, -1)
    return wmain, wside, fb, wuq, wukv, gg, wo, wup, cw, cb, wdown


def kernel(x, attn_pre_norm, w_in, forget_bias, swa_sinks, rel_bias, q_latent_norm, w_uq, kv_latent_norm, w_ukv,
           group_norm, w_out, attn_post_norm, ffn_pre_norm, w_up, conv_w, conv_b, w_down, ffn_post_norm):
    batch, seq, d = x.shape
    assert d == D_MODEL and seq % ROW_TILE == 0 and seq % ATTN_TILE == 0
    depth = w_in.shape[0]
    cos_t, sin_t = _rope_tables(seq)
    bias = _swa_bias_table(rel_bias)
    x2 = x.reshape(batch * seq, d)
    for l in range(depth):
        wmain, wside, fb, wuq, wukv, gg, wo, wup, cw, cb, wdown = _layer_params(
            l, w_in, forget_bias, w_uq, w_ukv, group_norm, w_out, w_up, conv_w, conv_b, w_down)
        sinks = jnp.repeat(swa_sinks[l].astype(F32).reshape(SWA_KV_HEADS, -1), WINDOW, axis=1)
        sinks = jnp.broadcast_to(sinks[:, :, None], sinks.shape + (LANES,))
        (swa_q, swa_k, swa_v, fox_q, fox_k, fox_v, mla_q, mla_k, mla_v) = _in_proj(
            x2, attn_pre_norm[l][None, :], wmain, wside, fb, q_latent_norm[l][None, :], wuq,
            kv_latent_norm[l][None, :], wukv, cos_t, sin_t, batch, seq)
        out_a = _swa(swa_q, swa_k, swa_v, bias, sinks, batch, seq)
        out_b = _flash(fox_q, fox_k, fox_v, batch, seq, "fox_attn")
        out_c = _flash(mla_q, mla_k, mla_v, batch, seq, "mla_attn")
        x2 = _mix_ffn(out_a, out_b, out_c, gg, wo, attn_post_norm[l][None, :], x2, ffn_pre_norm[l][None, :],
                      wup, cw, cb, wdown, ffn_post_norm[l][None, :], batch, seq)
    return x2.reshape(batch, seq, d)
```

```python
import functools
import math

import jax
import jax.numpy as jnp
from jax import lax
from jax.experimental import pallas as pl
from jax.experimental.pallas import tpu as pltpu

D_MODEL = 1024
HEAD_DIM = 64
SWA_Q_HEADS = 8
SWA_KV_HEADS = 2
WINDOW = 128
FOX_HEADS = 4
MLA_HEADS = 4
MLA_Q_RANK = 256
MLA_KV_RANK = 128
MLA_NOPE_DIM = 64
MLA_ROPE_DIM = 32
MLA_V_DIM = 64
MLA_QK_DIM = MLA_NOPE_DIM + MLA_ROPE_DIM
ROPE_THETA = 10000.0
REL_BUCKETS = 32
REL_MAX_DIST = 128
D_FF = 2816
EPS = 1e-6
NEG_INF = -1e30

SWA_WIDTH = SWA_Q_HEADS * HEAD_DIM
FOX_WIDTH = FOX_HEADS * HEAD_DIM
MLA_WIDTH = MLA_HEADS * MLA_V_DIM
SWA_COLS = (SWA_Q_HEADS + 2 * SWA_KV_HEADS) * HEAD_DIM
FOX_COLS = 3 * FOX_WIDTH + FOX_HEADS

LANES = 128
ROW_TILE = 512
IN_PROJ_TILE = 1024
ATTN_TILE = 1024
FF_CHUNK = 256
N_FF_CHUNKS = D_FF // FF_CHUNK
VMEM_LIMIT = 56 * 1024 * 1024

SWA_HEAD_ORDER = (0, 4, 1, 5, 2, 6, 3, 7)

F32 = jnp.float32
BF16 = jnp.bfloat16
NT_DIMS = (((1,), (1,)), ((), ()))
LOG2E = math.log2(math.e)


def _rms(x, g):
    ms = jnp.mean(x * x, axis=-1, keepdims=True)
    return x * lax.rsqrt(ms + EPS) * g


def _params(*sem):
    return pltpu.CompilerParams(dimension_semantics=sem, vmem_limit_bytes=VMEM_LIMIT)


def _full(shape):
    nd = len(shape)
    return pl.BlockSpec(shape, lambda *_: (0,) * nd)


def _in_proj_kernel(x_ref, g_ref, wmain_ref, wside_ref, fb_ref, qg_ref, wuq_ref, kvg_ref, wukv_ref,
                    cos_ref, sin_ref,
                    swaq_ref, swak_ref, swav_ref, foxq_ref, foxk_ref, foxv_ref,
                    mlaq_ref, mlak_ref, mlav_ref, carry_ref):
    tm = x_ref.shape[0]
    h = _rms(x_ref[...], g_ref[...]).astype(BF16)

    main = jnp.dot(h, wmain_ref[...], preferred_element_type=F32)
    scale = HEAD_DIM ** -0.5
    swaq_ref[...] = (main[:, 0:512] * scale).astype(BF16)
    swak_ref[...] = main[:, 512:640].astype(BF16)
    swav_ref[...] = main[:, 640:768].astype(BF16)
    foxv_ref[...] = main[:, 1280:1536].astype(BF16)

    side = jnp.dot(h, wside_ref[...], preferred_element_type=F32)
    cos = cos_ref[...]
    sin = sin_ref[...]

    cqn = _rms(side[:, 0:256], qg_ref[...]).astype(BF16)
    qa = jnp.dot(cqn, wuq_ref[...], preferred_element_type=F32)
    for hh in range(MLA_HEADS):
        lo = hh * LANES
        a = qa[:, lo:lo + LANES]
        b = qa[:, 512 + lo:512 + lo + LANES]
        mlaq_ref[:, lo:lo + LANES] = ((a * cos + b * sin) * (MLA_QK_DIM ** -0.5 * LOG2E)).astype(BF16)

    ckvn = _rms(side[:, 256:384], kvg_ref[...]).astype(BF16)
    kv = jnp.dot(ckvn, wukv_ref[...], preferred_element_type=F32)
    krot = side[:, 384:512] * cos + side[:, 512:640] * sin
    for hh in range(MLA_HEADS):
        lo = hh * LANES
        mlak_ref[:, lo:lo + LANES] = (kv[:, lo:lo + LANES] + krot).astype(BF16)
    mlav_ref[...] = kv[:, 512:768].astype(BF16)

    lane = lax.broadcasted_iota(jnp.int32, (tm, LANES), 1)
    row = lax.broadcasted_iota(jnp.int32, (tm, LANES), 0)
    c = jnp.where(lane < FOX_HEADS, jax.nn.log_sigmoid(side[:, 640:768] + fb_ref[...]), 0.0)
    shift = 1
    while shift < tm:
        c = c + jnp.where(row >= shift, pltpu.roll(c, shift, axis=0), 0.0)
        shift *= 2

    @pl.when(pl.program_id(1) == 0)
    def _():
        carry_ref[...] = jnp.zeros_like(carry_ref)

    c = c + carry_ref[0:1, :]
    carry_ref[...] = jnp.broadcast_to(c[tm - 1:tm, :], carry_ref.shape)

    for hd in range(FOX_HEADS):
        pair, half = hd // 2, hd % 2
        o = 0 if half else HEAD_DIM
        f = jnp.broadcast_to(jnp.sum(jnp.where(lane == hd, c, 0.0), axis=1, keepdims=True), c.shape) * LOG2E
        hi = f.astype(BF16).astype(F32)
        mid = (f - hi).astype(BF16).astype(F32)
        low = (f - hi) - mid
        one_k = jnp.where(jnp.logical_and(lane >= o + 3, lane < o + 6), 1.0, 0.0)
        one_q = jnp.where(jnp.logical_and(lane >= o, lane < o + 3), 1.0, 0.0)
        extra_k = jnp.where(lane == o, -hi, jnp.where(lane == o + 1, -mid, jnp.where(lane == o + 2, -low, one_k)))
        extra_q = jnp.where(lane == o + 3, hi, jnp.where(lane == o + 4, mid, jnp.where(lane == o + 5, low, one_q)))
        data = _half_lanes(c.shape, half)
        slab = slice(hd * LANES, (hd + 1) * LANES)
        q_pair = main[:, 768 + pair * LANES:768 + (pair + 1) * LANES] * (scale * LOG2E)
        k_pair = main[:, 1024 + pair * LANES:1024 + (pair + 1) * LANES]
        foxq_ref[:, slab] = jnp.where(data, q_pair, extra_q).astype(BF16)
        foxk_ref[:, slab] = jnp.where(data, k_pair, extra_k).astype(BF16)


def _in_proj(x2, g, wmain, wside, fb, qg, wuq, kvg, wukv, cos_t, sin_t, batch, seq):
    T = x2.shape[0]
    tm = IN_PROJ_TILE
    ns = seq // tm
    row = lambda b, s: (b * ns + s, 0)
    seqrow = lambda b, s: (s, 0)
    out_shape = [
        jax.ShapeDtypeStruct((T, 512), BF16), jax.ShapeDtypeStruct((T, 128), BF16),
        jax.ShapeDtypeStruct((T, 128), BF16), jax.ShapeDtypeStruct((T, 512), BF16),
        jax.ShapeDtypeStruct((T, 512), BF16), jax.ShapeDtypeStruct((T, 256), BF16),
        jax.ShapeDtypeStruct((T, 512), BF16), jax.ShapeDtypeStruct((T, 512), BF16),
        jax.ShapeDtypeStruct((T, 256), BF16),
    ]
    out_specs = [
        pl.BlockSpec((tm, 512), row), pl.BlockSpec((tm, 128), row), pl.BlockSpec((tm, 128), row),
        pl.BlockSpec((tm, 512), row), pl.BlockSpec((tm, 512), row), pl.BlockSpec((tm, 256), row),
        pl.BlockSpec((tm, 512), row), pl.BlockSpec((tm, 512), row), pl.BlockSpec((tm, 256), row),
    ]
    in_specs = [
        pl.BlockSpec((tm, D_MODEL), row), _full(g.shape), _full(wmain.shape), _full(wside.shape),
        _full(fb.shape), _full(qg.shape), _full(wuq.shape), _full(kvg.shape), _full(wukv.shape),
        pl.BlockSpec((tm, LANES), seqrow), pl.BlockSpec((tm, LANES), seqrow),
    ]
    return pl.pallas_call(
        _in_proj_kernel,
        grid=(batch, ns),
        in_specs=in_specs,
        out_specs=out_specs,
        out_shape=out_shape,
        scratch_shapes=[pltpu.VMEM((8, LANES), F32)],
        compiler_params=_params("arbitrary", "arbitrary"),
        name="in_proj",
    )(x2, g, wmain, wside, fb, qg, wuq, kvg, wukv, cos_t, sin_t)


def _half_lanes(shape, hh):
    lane = lax.broadcasted_iota(jnp.int32, shape, len(shape) - 1)
    return (lane < HEAD_DIM) if hh == 0 else (lane >= HEAD_DIM)


def _ones_in_other_half(v2, hh):
    return jnp.where(_half_lanes(v2.shape, hh), v2, jnp.ones_like(v2))


def _swa_kernel(q_ref, kc_ref, kp_ref, vc_ref, vp_ref, bias_ref, sink_ref, o_ref, kbuf, vbuf):
    first_tile = pl.program_id(1) == 0
    nwin = q_ref.shape[0] // WINDOW
    npair = SWA_Q_HEADS // 2
    kbuf[0:WINDOW, :] = kp_ref[...]
    kbuf[WINDOW:, :] = kc_ref[...]
    vbuf[0:WINDOW, :] = vp_ref[...]
    vbuf[WINDOW:, :] = vc_ref[...]
    kj = lax.broadcasted_iota(jnp.int32, (1, 2 * WINDOW), 1)
    no_prev = jnp.where(jnp.logical_and(kj < WINDOW, first_tile), NEG_INF, 0.0)
    low = _half_lanes((WINDOW, LANES), 0)
    for w in range(nwin):
        kw = kbuf[w * WINDOW:(w + 2) * WINDOW, :]
        vw = vbuf[w * WINDOW:(w + 2) * WINDOW, :]
        rows = slice(w * WINDOW, (w + 1) * WINDOW)
        q4 = [q_ref[rows, p * LANES:(p + 1) * LANES] for p in range(npair)]
        outs = []
        for hh in range(2):
            mine = _half_lanes((WINDOW, LANES), hh)
            qs = jnp.concatenate([jnp.where(mine, q, jnp.zeros_like(q)) for q in q4], axis=0)
            s = lax.dot_general(qs, kw, NT_DIMS, preferred_element_type=F32) + bias_ref[hh]
            if w == 0:
                s = s + no_prev
            sink = sink_ref[hh]
            s0, s1 = s[:, 0:LANES], s[:, LANES:]
            mloc = jnp.max(jnp.maximum(s0, s1), axis=1, keepdims=True)
            m = jnp.maximum(jnp.broadcast_to(mloc, sink.shape), sink)
            e = jnp.concatenate([jnp.exp(s0 - m), jnp.exp(s1 - m)], axis=1).astype(BF16)
            pv = jnp.dot(e, _ones_in_other_half(vw, hh), preferred_element_type=F32)
            den = pltpu.roll(pv, HEAD_DIM, axis=1) + jnp.exp(sink - m)
            outs.append(pv / den)
        for p in range(npair):
            pr = slice(p * WINDOW, (p + 1) * WINDOW)
            o_ref[rows, p * LANES:(p + 1) * LANES] = jnp.where(low, outs[0][pr], outs[1][pr])


def _swa(q, k, v, bias, sinks, batch, seq):
    T = q.shape[0]
    tm = ROW_TILE
    ns = seq // tm
    per = tm // WINDOW
    row = lambda b, s: (b * ns + s, 0)
    prev = lambda b, s: (jnp.maximum((b * ns + s) * per - 1, 0), 0)
    return pl.pallas_call(
        _swa_kernel,
        grid=(batch, ns),
        in_specs=[
            pl.BlockSpec((tm, SWA_WIDTH), row),
            pl.BlockSpec((tm, LANES), row), pl.BlockSpec((WINDOW, LANES), prev),
            pl.BlockSpec((tm, LANES), row), pl.BlockSpec((WINDOW, LANES), prev),
            _full(bias.shape), _full(sinks.shape),
        ],
        out_specs=pl.BlockSpec((tm, SWA_WIDTH), row),
        out_shape=jax.ShapeDtypeStruct((T, SWA_WIDTH), F32),
        scratch_shapes=[pltpu.VMEM((tm + WINDOW, LANES), BF16), pltpu.VMEM((tm + WINDOW, LANES), BF16)],
        compiler_params=_params("parallel", "arbitrary"),
        name="swa_attn",
    )(q, k, k, v, v, bias, sinks)


def _flash_init(m_sc, acc_sc):
    m_sc[...] = jnp.full(m_sc.shape, NEG_INF, F32)
    acc_sc[...] = jnp.zeros(acc_sc.shape, F32)


def _flash_update(s_tiles, v2, head, m_sc, acc_sc):
    mloc = functools.reduce(jnp.maximum, s_tiles)
    mloc = jnp.broadcast_to(jnp.max(mloc, axis=1, keepdims=True), mloc.shape)
    m_prev = m_sc[head]
    m_new = jnp.maximum(m_prev, mloc)
    m_sc[head] = m_new
    alpha = jnp.exp2(m_prev - m_new)
    p = jnp.concatenate([jnp.exp2(s - m_new) for s in s_tiles], axis=1).astype(BF16)
    pv = jnp.dot(p, _ones_in_other_half(v2, head % 2), preferred_element_type=F32)
    acc_sc[head] = alpha * acc_sc[head] + pv


def _flash_finish(o_ref, acc_sc):
    low = _half_lanes(acc_sc.shape[1:], 0)
    for pair in range(acc_sc.shape[0] // 2):
        outs = [acc_sc[2 * pair + hh] / pltpu.roll(acc_sc[2 * pair + hh], HEAD_DIM, axis=1) for hh in range(2)]
        o_ref[:, pair * LANES:(pair + 1) * LANES] = jnp.where(low, outs[0], outs[1])


def _lane_tiles(s):
    return [s[:, j * LANES:(j + 1) * LANES] for j in range(s.shape[1] // LANES)]


def _below_diagonal(tq, tk):
    r = lax.broadcasted_iota(jnp.int32, (tq, LANES), 0)
    c = lax.broadcasted_iota(jnp.int32, (tq, LANES), 1)
    return [c + j * LANES <= r for j in range(tk // LANES)]


def _flash_kernel(qi_ref, ki_ref, q_ref, k_ref, v_ref, o_ref, m_sc, acc_sc):
    t = pl.program_id(1)
    qi = qi_ref[t]
    ki = ki_ref[t]
    tq, tk = q_ref.shape[0], k_ref.shape[0]
    nheads = q_ref.shape[1] // LANES

    @pl.when(ki == 0)
    def _():
        _flash_init(m_sc, acc_sc)

    def step(diagonal):
        for head in range(nheads):
            slab = slice(head * LANES, (head + 1) * LANES)
            pair = slice((head // 2) * LANES, (head // 2 + 1) * LANES)
            s = lax.dot_general(q_ref[:, slab], k_ref[:, slab], NT_DIMS, preferred_element_type=F32)
            tiles = _lane_tiles(s)
            if diagonal:
                tiles = [jnp.where(keep, x, NEG_INF) for keep, x in zip(_below_diagonal(tq, tk), tiles)]
            _flash_update(tiles, v_ref[:, pair], head, m_sc, acc_sc)

    @pl.when(ki < qi)
    def _():
        step(False)

    @pl.when(ki == qi)
    def _():
        step(True)
        _flash_finish(o_ref, acc_sc)


def _tri_tables(nq):
    qi = [q for q in range(nq) for _ in range(q + 1)]
    ki = [k for q in range(nq) for k in range(q + 1)]
    return jnp.asarray(qi, jnp.int32), jnp.asarray(ki, jnp.int32)


def _flash(q, k, v, batch, seq, name):
    T = q.shape[0]
    nheads = q.shape[1] // LANES
    tq = ATTN_TILE
    nq = seq // tq
    qi_tab, ki_tab = _tri_tables(nq)
    qrow = lambda b, t, qi, ki: (b * nq + qi[t], 0)
    krow = lambda b, t, qi, ki: (b * nq + ki[t], 0)
    grid_spec = pltpu.PrefetchScalarGridSpec(
        num_scalar_prefetch=2,
        grid=(batch, qi_tab.shape[0]),
        in_specs=[
            pl.BlockSpec((tq, q.shape[1]), qrow), pl.BlockSpec((tq, k.shape[1]), krow),
            pl.BlockSpec((tq, v.shape[1]), krow),
        ],
        out_specs=pl.BlockSpec((tq, v.shape[1]), qrow),
        scratch_shapes=[pltpu.VMEM((nheads, tq, LANES), F32),
                        pltpu.VMEM((nheads, tq, LANES), F32)],
    )
    return pl.pallas_call(
        _flash_kernel,
        grid_spec=grid_spec,
        out_shape=jax.ShapeDtypeStruct((T, v.shape[1]), F32),
        compiler_params=_params("parallel", "arbitrary"),
        name=name,
    )(qi_tab, ki_tab, q, k, v)


def _mix_ffn_kernel(oa_ref, ob_ref, oc_ref, gg_ref, wout_ref, apg_ref, x_ref,
                    g_ref, wup_ref, cw_ref, cb_ref, wdown_ref, pg_ref, o_ref,
                    xbuf, hbuf, ubuf, acc_ref, carry_ref):
    tm = x_ref.shape[0]

    gg = gg_ref[...]
    a = _rms(oa_ref[...], gg[:, 0:SWA_WIDTH]).astype(BF16)
    b = _rms(ob_ref[...], gg[:, SWA_WIDTH:SWA_WIDTH + FOX_WIDTH]).astype(BF16)
    c = _rms(oc_ref[...], gg[:, SWA_WIDTH + FOX_WIDTH:]).astype(BF16)
    y = jnp.dot(a, wout_ref[0:SWA_WIDTH, :], preferred_element_type=F32)
    y = y + jnp.dot(b, wout_ref[SWA_WIDTH:SWA_WIDTH + FOX_WIDTH, :], preferred_element_type=F32)
    y = y + jnp.dot(c, wout_ref[SWA_WIDTH + FOX_WIDTH:, :], preferred_element_type=F32)
    x1 = x_ref[...] + _rms(y, apg_ref[...])
    xbuf[...] = x1
    hbuf[...] = _rms(x1, g_ref[...]).astype(BF16)

    @pl.when(pl.program_id(1) == 0)
    def _():
        carry_ref[...] = jnp.zeros_like(carry_ref)

    row8 = lax.broadcasted_iota(jnp.int32, (8, 2 * FF_CHUNK), 0)

    def up_proj(c, slot):
        ubuf[slot] = jnp.dot(hbuf[...], wup_ref[c], preferred_element_type=F32)

    def conv_act_down(c, slot, first=False):
        u = ubuf[slot]
        prev = carry_ref[c]
        carry_ref[c] = u[tm - 8:tm, :]
        w = cw_ref[c]
        y = cb_ref[c] + w[2:3, :] * u
        for d in (1, 2):
            r = pltpu.roll(u, d, axis=0)
            head = jnp.where(row8 < d, pltpu.roll(prev, d, axis=0), r[0:8, :])
            r = jnp.concatenate([head, r[8:, :]], axis=0)
            y = y + w[2 - d:3 - d, :] * r
        act = (jax.nn.gelu(y[:, 0:FF_CHUNK], approximate=True) * y[:, FF_CHUNK:]).astype(BF16)
        down = jnp.dot(act, wdown_ref[c], preferred_element_type=F32)
        if first:
            acc_ref[...] = down
        else:
            acc_ref[...] += down

    up_proj(0, 0)
    up_proj(1, 1)
    conv_act_down(0, 0, first=True)

    def two_chunks(j, _):
        c = 2 * j + 1
        up_proj(c + 1, 0)
        conv_act_down(c, 1)
        up_proj(c + 2, 1)
        conv_act_down(c + 1, 0)
        return 0

    lax.fori_loop(0, (N_FF_CHUNKS - 3) // 2, two_chunks, 0)
    up_proj(N_FF_CHUNKS - 1, 0)
    conv_act_down(N_FF_CHUNKS - 2, 1)
    conv_act_down(N_FF_CHUNKS - 1, 0)
    o_ref[...] = xbuf[...] + _rms(acc_ref[...], pg_ref[...])


def _mix_ffn(oa, ob, oc, gg, wout, apg, x2, g, wup, cw, cb, wdown, pg, batch, seq):
    T = x2.shape[0]
    tm = ROW_TILE
    ns = seq // tm
    row = lambda b, s: (b * ns + s, 0)
    return pl.pallas_call(
        _mix_ffn_kernel,
        grid=(batch, ns),
        in_specs=[
            pl.BlockSpec((tm, SWA_WIDTH), row), pl.BlockSpec((tm, FOX_WIDTH), row),
            pl.BlockSpec((tm, MLA_WIDTH), row), _full(gg.shape), _full(wout.shape), _full(apg.shape),
            pl.BlockSpec((tm, D_MODEL), row), _full(g.shape), _full(wup.shape), _full(cw.shape),
            _full(cb.shape), _full(wdown.shape), _full(pg.shape),
        ],
        out_specs=pl.BlockSpec((tm, D_MODEL), row),
        out_shape=jax.ShapeDtypeStruct((T, D_MODEL), F32),
        scratch_shapes=[
            pltpu.VMEM((tm, D_MODEL), F32), pltpu.VMEM((tm, D_MODEL), BF16),
            pltpu.VMEM((2, tm, 2 * FF_CHUNK), F32), pltpu.VMEM((tm, D_MODEL), F32),
            pltpu.VMEM((N_FF_CHUNKS, 8, 2 * FF_CHUNK), F32),
        ],
        compiler_params=_params("parallel", "arbitrary"),
        name="mix_ffn",
    )(oa, ob, oc, gg, wout, apg, x2, g, wup, cw, cb, wdown, pg)


def _t5_causal_bucket(dist):
    max_exact = REL_BUCKETS // 2
    d = jnp.maximum(dist, 0)
    log_ratio = jnp.log(jnp.maximum(d, 1).astype(F32) / max_exact) / math.log(REL_MAX_DIST / max_exact)
    large = max_exact + (log_ratio * (REL_BUCKETS - max_exact)).astype(jnp.int32)
    large = jnp.minimum(large, REL_BUCKETS - 1)
    return jnp.where(d < max_exact, d, large)


def _swa_bias_table(rel_bias):
    qi = jnp.arange(WINDOW, dtype=jnp.int32)[:, None] + WINDOW
    kj = jnp.arange(2 * WINDOW, dtype=jnp.int32)[None, :]
    dist = qi - kj
    in_band = (dist >= 0) & (dist < WINDOW)
    hit = _t5_causal_bucket(dist)[None, :, :] == jnp.arange(REL_BUCKETS, dtype=jnp.int32)[:, None, None]
    bias = jnp.sum(jnp.where(hit[None], rel_bias.astype(F32).T[:, :, None, None], 0.0), axis=1)
    bias = jnp.where(in_band[None], bias, NEG_INF)
    return bias.reshape(SWA_KV_HEADS, (SWA_Q_HEADS // SWA_KV_HEADS) * WINDOW, 2 * WINDOW)


def _rope_tables(seq):
    pos = jnp.arange(seq, dtype=F32)
    inv_freq = ROPE_THETA ** (-(jnp.arange(MLA_ROPE_DIM // 2, dtype=F32) * 2.0 / MLA_ROPE_DIM))
    ang = pos[:, None] * inv_freq[None, :]
    cos, sin = jnp.cos(ang), jnp.sin(ang)
    pad = jnp.zeros((seq, LANES - MLA_QK_DIM), F32)
    cos_t = jnp.concatenate([jnp.ones((seq, MLA_NOPE_DIM), F32), cos, cos, pad], axis=1)
    sin_t = jnp.concatenate([jnp.zeros((seq, MLA_NOPE_DIM), F32), -sin, sin, pad], axis=1)
    return cos_t, sin_t


def _swap_halves(w):
    half = w.shape[1] // 2
    return jnp.concatenate([w[:, half:], w[:, :half]], axis=1)


def _layer_params(l, w_in, forget_bias, w_uq, w_ukv, group_norm, w_out, w_up, conv_w, conv_b, w_down):
    wi = w_in[l]
    d = wi.shape[0]
    order = jnp.asarray(SWA_HEAD_ORDER)
    swa_q = wi[:, 0:SWA_WIDTH].reshape(d, SWA_Q_HEADS, HEAD_DIM)[:, order].reshape(d, SWA_WIDTH)
    wmain = jnp.concatenate([swa_q, wi[:, SWA_WIDTH:SWA_COLS + 3 * FOX_WIDTH]], axis=1).astype(BF16)

    f0 = SWA_COLS + 3 * FOX_WIDTH
    m0 = SWA_COLS + FOX_COLS
    z = lambda n: jnp.zeros((d, n), F32)
    w_kr = wi[:, m0 + MLA_Q_RANK + MLA_KV_RANK:]
    rope_pad = LANES - MLA_QK_DIM
    wside = jnp.concatenate([
        wi[:, m0:m0 + MLA_Q_RANK + MLA_KV_RANK],
        z(MLA_NOPE_DIM), w_kr, z(rope_pad),
        z(MLA_NOPE_DIM), _swap_halves(w_kr), z(rope_pad),
        wi[:, f0:f0 + FOX_HEADS], z(LANES - FOX_HEADS),
    ], axis=1).astype(BF16)
    fb = jnp.pad(forget_bias[l], (0, LANES - FOX_HEADS))[None, :]

    uq = w_uq[l].reshape(MLA_Q_RANK, MLA_HEADS, MLA_QK_DIM)
    zq = jnp.zeros((MLA_Q_RANK, MLA_HEADS, 1), F32)
    plain = jnp.concatenate([uq, jnp.tile(zq, (1, 1, rope_pad))], axis=2)
    rope = uq[:, :, MLA_NOPE_DIM:]
    swapped = jnp.concatenate([jnp.tile(zq, (1, 1, MLA_NOPE_DIM)), rope[:, :, MLA_ROPE_DIM // 2:],
                               rope[:, :, :MLA_ROPE_DIM // 2], jnp.tile(zq, (1, 1, rope_pad))], axis=2)
    wuq = jnp.concatenate([plain.reshape(MLA_Q_RANK, -1), swapped.reshape(MLA_Q_RANK, -1)], axis=1).astype(BF16)

    ukv = w_ukv[l].reshape(MLA_KV_RANK, MLA_HEADS, MLA_NOPE_DIM + MLA_V_DIM)
    k_slabs = jnp.concatenate([ukv[:, :, :MLA_NOPE_DIM],
                               jnp.zeros((MLA_KV_RANK, MLA_HEADS, LANES - MLA_NOPE_DIM), F32)], axis=2)
    wukv = jnp.concatenate([k_slabs.reshape(MLA_KV_RANK, -1),
                            ukv[:, :, MLA_NOPE_DIM:].reshape(MLA_KV_RANK, -1)], axis=1).astype(BF16)

    gg = group_norm[l]
    gg = jnp.concatenate([gg[:SWA_WIDTH].reshape(SWA_Q_HEADS, HEAD_DIM)[order].reshape(-1), gg[SWA_WIDTH:]])[None, :]
    wo = w_out[l]
    wo = jnp.concatenate([wo[:SWA_WIDTH].reshape(SWA_Q_HEADS, HEAD_DIM, -1)[order].reshape(SWA_WIDTH, -1),
                          wo[SWA_WIDTH:]], axis=0).astype(BF16)

    def gate_up(a):
        lead = a.shape[:-1]
        a = a.reshape(lead + (2, N_FF_CHUNKS, FF_CHUNK))
        a = jnp.moveaxis(a, -2, 0)
        return a.reshape((N_FF_CHUNKS,) + lead + (2 * FF_CHUNK,))

    wup = gate_up(w_up[l].astype(BF16))
    cw = gate_up(conv_w[l])
    cb = gate_up(conv_b[l][None, :])
    wdown = w_down[l].astype(BF16).reshape(N_FF_CHUNKS, FF_CHUNK, -1)
    return wmain, wside, fb, wuq, wukv, gg, wo, wup, cw, cb, wdown


def kernel(x, attn_pre_norm, w_in, forget_bias, swa_sinks, rel_bias, q_latent_norm, w_uq, kv_latent_norm, w_ukv,
           group_norm, w_out, attn_post_norm, ffn_pre_norm, w_up, conv_w, conv_b, w_down, ffn_post_norm):
    batch, seq, d = x.shape
    assert d == D_MODEL and seq % ROW_TILE == 0 and seq % ATTN_TILE == 0 and seq % IN_PROJ_TILE == 0
    depth = w_in.shape[0]
    cos_t, sin_t = _rope_tables(seq)
    bias = _swa_bias_table(rel_bias)
    x2 = x.reshape(batch * seq, d)
    for l in range(depth):
        wmain, wside, fb, wuq, wukv, gg, wo, wup, cw, cb, wdown = _layer_params(
            l, w_in, forget_bias, w_uq, w_ukv, group_norm, w_out, w_up, conv_w, conv_b, w_down)
        sinks = jnp.repeat(swa_sinks[l].astype(F32).reshape(SWA_KV_HEADS, -1), WINDOW, axis=1)
        sinks = jnp.broadcast_to(sinks[:, :, None], sinks.shape + (LANES,))
        (swa_q, swa_k, swa_v, fox_q, fox_k, fox_v, mla_q, mla_k, mla_v) = _in_proj(
            x2, attn_pre_norm[l][None, :], wmain, wside, fb, q_latent_norm[l][None, :], wuq,
            kv_latent_norm[l][None, :], wukv, cos_t, sin_t, batch, seq)
        out_a = _swa(swa_q, swa_k, swa_v, bias, sinks, batch, seq)
        out_b = _flash(fox_q, fox_k, fox_v, batch, seq, "fox_attn")
        out_c = _flash(mla_q, mla_k, mla_v, batch, seq, "mla_attn")
        x2 = _mix_ffn(out_a, out_b, out_c, gg, wo, attn_post_norm[l][None, :], x2, ffn_pre_norm[l][None, :],
                      wup, cw, cb, wdown, ffn_post_norm[l][None, :], batch, seq)
    return x2.reshape(batch, seq, d)
```

```python
import functools
import math

import jax
import jax.numpy as jnp
from jax import lax
from jax.experimental import pallas as pl
from jax.experimental.pallas import tpu as pltpu

D_MODEL = 1024
HEAD_DIM = 64
SWA_Q_HEADS = 8
SWA_KV_HEADS = 2
WINDOW = 128
FOX_HEADS = 4
MLA_HEADS = 4
MLA_Q_RANK = 256
MLA_KV_RANK = 128
MLA_NOPE_DIM = 64
MLA_ROPE_DIM = 32
MLA_V_DIM = 64
MLA_QK_DIM = MLA_NOPE_DIM + MLA_ROPE_DIM
ROPE_THETA = 10000.0
REL_BUCKETS = 32
REL_MAX_DIST = 128
D_FF = 2816
EPS = 1e-6
NEG_INF = -1e30

SWA_WIDTH = SWA_Q_HEADS * HEAD_DIM
FOX_WIDTH = FOX_HEADS * HEAD_DIM
MLA_WIDTH = MLA_HEADS * MLA_V_DIM
SWA_COLS = (SWA_Q_HEADS + 2 * SWA_KV_HEADS) * HEAD_DIM
FOX_COLS = 3 * FOX_WIDTH + FOX_HEADS

LANES = 128
ROW_TILE = 512
IN_PROJ_TILE = 1024
MIX_TILE = 1024
ATTN_TILE = 1024
FF_CHUNK = 256
N_FF_CHUNKS = D_FF // FF_CHUNK
VMEM_LIMIT = 60 * 1024 * 1024

SWA_HEAD_ORDER = (0, 4, 1, 5, 2, 6, 3, 7)

F32 = jnp.float32
BF16 = jnp.bfloat16
NT_DIMS = (((1,), (1,)), ((), ()))
LOG2E = math.log2(math.e)


def _rms(x, g):
    ms = jnp.mean(x * x, axis=-1, keepdims=True)
    return x * lax.rsqrt(ms + EPS) * g


def _params(*sem):
    return pltpu.CompilerParams(dimension_semantics=sem, vmem_limit_bytes=VMEM_LIMIT)


def _full(shape):
    nd = len(shape)
    return pl.BlockSpec(shape, lambda *_: (0,) * nd)


def _in_proj_kernel(x_ref, g_ref, wmain_ref, wside_ref, fb_ref, qg_ref, wuq_ref, kvg_ref, wukv_ref,
                    cos_ref, sin_ref,
                    swaq_ref, swak_ref, swav_ref, foxq_ref, foxk_ref, foxv_ref,
                    mlaq_ref, mlak_ref, mlav_ref, carry_ref):
    tm = x_ref.shape[0]
    h = _rms(x_ref[...], g_ref[...]).astype(BF16)

    main = jnp.dot(h, wmain_ref[...], preferred_element_type=F32)
    scale = HEAD_DIM ** -0.5
    swaq_ref[...] = (main[:, 0:512] * scale).astype(BF16)
    swak_ref[...] = main[:, 512:640].astype(BF16)
    swav_ref[...] = main[:, 640:768].astype(BF16)
    foxv_ref[...] = main[:, 1280:1536].astype(BF16)

    side = jnp.dot(h, wside_ref[...], preferred_element_type=F32)
    cos = cos_ref[...]
    sin = sin_ref[...]

    cqn = _rms(side[:, 0:256], qg_ref[...]).astype(BF16)
    qa = jnp.dot(cqn, wuq_ref[...], preferred_element_type=F32)
    for hh in range(MLA_HEADS):
        lo = hh * LANES
        a = qa[:, lo:lo + LANES]
        b = qa[:, 512 + lo:512 + lo + LANES]
        mlaq_ref[:, lo:lo + LANES] = ((a * cos + b * sin) * (MLA_QK_DIM ** -0.5 * LOG2E)).astype(BF16)

    ckvn = _rms(side[:, 256:384], kvg_ref[...]).astype(BF16)
    kv = jnp.dot(ckvn, wukv_ref[...], preferred_element_type=F32)
    krot = side[:, 384:512] * cos + side[:, 512:640] * sin
    for hh in range(MLA_HEADS):
        lo = hh * LANES
        mlak_ref[:, lo:lo + LANES] = (kv[:, lo:lo + LANES] + krot).astype(BF16)
    mlav_ref[...] = kv[:, 512:768].astype(BF16)

    lane = lax.broadcasted_iota(jnp.int32, (tm, LANES), 1)
    row = lax.broadcasted_iota(jnp.int32, (tm, LANES), 0)
    c = jnp.where(lane < FOX_HEADS, jax.nn.log_sigmoid(side[:, 640:768] + fb_ref[...]), 0.0)
    shift = 1
    while shift < tm:
        c = c + jnp.where(row >= shift, pltpu.roll(c, shift, axis=0), 0.0)
        shift *= 2

    @pl.when(pl.program_id(1) == 0)
    def _():
        carry_ref[...] = jnp.zeros_like(carry_ref)

    c = c + carry_ref[0:1, :]
    carry_ref[...] = jnp.broadcast_to(c[tm - 1:tm, :], carry_ref.shape)

    for hd in range(FOX_HEADS):
        pair, half = hd // 2, hd % 2
        o = 0 if half else HEAD_DIM
        f = jnp.broadcast_to(jnp.sum(jnp.where(lane == hd, c, 0.0), axis=1, keepdims=True), c.shape) * LOG2E
        hi = f.astype(BF16).astype(F32)
        mid = (f - hi).astype(BF16).astype(F32)
        low = (f - hi) - mid
        one_k = jnp.where(jnp.logical_and(lane >= o + 3, lane < o + 6), 1.0, 0.0)
        one_q = jnp.where(jnp.logical_and(lane >= o, lane < o + 3), 1.0, 0.0)
        extra_k = jnp.where(lane == o, -hi, jnp.where(lane == o + 1, -mid, jnp.where(lane == o + 2, -low, one_k)))
        extra_q = jnp.where(lane == o + 3, hi, jnp.where(lane == o + 4, mid, jnp.where(lane == o + 5, low, one_q)))
        data = _half_lanes(c.shape, half)
        slab = slice(hd * LANES, (hd + 1) * LANES)
        q_pair = main[:, 768 + pair * LANES:768 + (pair + 1) * LANES] * (scale * LOG2E)
        k_pair = main[:, 1024 + pair * LANES:1024 + (pair + 1) * LANES]
        foxq_ref[:, slab] = jnp.where(data, q_pair, extra_q).astype(BF16)
        foxk_ref[:, slab] = jnp.where(data, k_pair, extra_k).astype(BF16)


def _in_proj(x2, g, wmain, wside, fb, qg, wuq, kvg, wukv, cos_t, sin_t, batch, seq):
    T = x2.shape[0]
    tm = IN_PROJ_TILE
    ns = seq // tm
    row = lambda b, s: (b * ns + s, 0)
    seqrow = lambda b, s: (s, 0)
    out_shape = [
        jax.ShapeDtypeStruct((T, 512), BF16), jax.ShapeDtypeStruct((T, 128), BF16),
        jax.ShapeDtypeStruct((T, 128), BF16), jax.ShapeDtypeStruct((T, 512), BF16),
        jax.ShapeDtypeStruct((T, 512), BF16), jax.ShapeDtypeStruct((T, 256), BF16),
        jax.ShapeDtypeStruct((T, 512), BF16), jax.ShapeDtypeStruct((T, 512), BF16),
        jax.ShapeDtypeStruct((T, 256), BF16),
    ]
    out_specs = [
        pl.BlockSpec((tm, 512), row), pl.BlockSpec((tm, 128), row), pl.BlockSpec((tm, 128), row),
        pl.BlockSpec((tm, 512), row), pl.BlockSpec((tm, 512), row), pl.BlockSpec((tm, 256), row),
        pl.BlockSpec((tm, 512), row), pl.BlockSpec((tm, 512), row), pl.BlockSpec((tm, 256), row),
    ]
    in_specs = [
        pl.BlockSpec((tm, D_MODEL), row), _full(g.shape), _full(wmain.shape), _full(wside.shape),
        _full(fb.shape), _full(qg.shape), _full(wuq.shape), _full(kvg.shape), _full(wukv.shape),
        pl.BlockSpec((tm, LANES), seqrow), pl.BlockSpec((tm, LANES), seqrow),
    ]
    return pl.pallas_call(
        _in_proj_kernel,
        grid=(batch, ns),
        in_specs=in_specs,
        out_specs=out_specs,
        out_shape=out_shape,
        scratch_shapes=[pltpu.VMEM((8, LANES), F32)],
        compiler_params=_params("arbitrary", "arbitrary"),
        name="in_proj",
    )(x2, g, wmain, wside, fb, qg, wuq, kvg, wukv, cos_t, sin_t)


def _half_lanes(shape, hh):
    lane = lax.broadcasted_iota(jnp.int32, shape, len(shape) - 1)
    return (lane < HEAD_DIM) if hh == 0 else (lane >= HEAD_DIM)


def _ones_in_other_half(v2, hh):
    return jnp.where(_half_lanes(v2.shape, hh), v2, jnp.ones_like(v2))


def _swa_kernel(q_ref, kc_ref, kp_ref, vc_ref, vp_ref, bias_ref, sink_ref, o_ref, kbuf, vbuf):
    first_tile = pl.program_id(1) == 0
    nwin = q_ref.shape[0] // WINDOW
    npair = SWA_Q_HEADS // 2
    kbuf[0:WINDOW, :] = kp_ref[...]
    kbuf[WINDOW:, :] = kc_ref[...]
    vbuf[0:WINDOW, :] = vp_ref[...]
    vbuf[WINDOW:, :] = vc_ref[...]
    kj = lax.broadcasted_iota(jnp.int32, (1, 2 * WINDOW), 1)
    no_prev = jnp.where(jnp.logical_and(kj < WINDOW, first_tile), NEG_INF, 0.0)
    low = _half_lanes((WINDOW, LANES), 0)
    for w in range(nwin):
        kw = kbuf[w * WINDOW:(w + 2) * WINDOW, :]
        vw = vbuf[w * WINDOW:(w + 2) * WINDOW, :]
        rows = slice(w * WINDOW, (w + 1) * WINDOW)
        q4 = [q_ref[rows, p * LANES:(p + 1) * LANES] for p in range(npair)]
        outs = []
        for hh in range(2):
            mine = _half_lanes((WINDOW, LANES), hh)
            qs = jnp.concatenate([jnp.where(mine, q, jnp.zeros_like(q)) for q in q4], axis=0)
            s = lax.dot_general(qs, kw, NT_DIMS, preferred_element_type=F32) + bias_ref[hh]
            if w == 0:
                s = s + no_prev
            sink = sink_ref[hh]
            s0, s1 = s[:, 0:LANES], s[:, LANES:]
            mloc = jnp.max(jnp.maximum(s0, s1), axis=1, keepdims=True)
            m = jnp.maximum(jnp.broadcast_to(mloc, sink.shape), sink)
            e = jnp.concatenate([jnp.exp(s0 - m), jnp.exp(s1 - m)], axis=1).astype(BF16)
            pv = jnp.dot(e, _ones_in_other_half(vw, hh), preferred_element_type=F32)
            den = pltpu.roll(pv, HEAD_DIM, axis=1) + jnp.exp(sink - m)
            outs.append(pv / den)
        for p in range(npair):
            pr = slice(p * WINDOW, (p + 1) * WINDOW)
            o_ref[rows, p * LANES:(p + 1) * LANES] = jnp.where(low, outs[0][pr], outs[1][pr]).astype(o_ref.dtype)


def _swa(q, k, v, bias, sinks, batch, seq):
    T = q.shape[0]
    tm = ROW_TILE
    ns = seq // tm
    per = tm // WINDOW
    row = lambda b, s: (b * ns + s, 0)
    prev = lambda b, s: (jnp.maximum((b * ns + s) * per - 1, 0), 0)
    return pl.pallas_call(
        _swa_kernel,
        grid=(batch, ns),
        in_specs=[
            pl.BlockSpec((tm, SWA_WIDTH), row),
            pl.BlockSpec((tm, LANES), row), pl.BlockSpec((WINDOW, LANES), prev),
            pl.BlockSpec((tm, LANES), row), pl.BlockSpec((WINDOW, LANES), prev),
            _full(bias.shape), _full(sinks.shape),
        ],
        out_specs=pl.BlockSpec((tm, SWA_WIDTH), row),
        out_shape=jax.ShapeDtypeStruct((T, SWA_WIDTH), BF16),
        scratch_shapes=[pltpu.VMEM((tm + WINDOW, LANES), BF16), pltpu.VMEM((tm + WINDOW, LANES), BF16)],
        compiler_params=_params("parallel", "arbitrary"),
        name="swa_attn",
    )(q, k, k, v, v, bias, sinks)


def _flash_init(m_sc, acc_sc):
    m_sc[...] = jnp.full(m_sc.shape, NEG_INF, F32)
    acc_sc[...] = jnp.zeros(acc_sc.shape, F32)


def _flash_update(s_tiles, v2, head, rows, m_sc, acc_sc):
    mloc = functools.reduce(jnp.maximum, s_tiles)
    mloc = jnp.broadcast_to(jnp.max(mloc, axis=1, keepdims=True), mloc.shape)
    m_prev = m_sc[head, rows]
    m_new = jnp.maximum(m_prev, mloc)
    m_sc[head, rows] = m_new
    alpha = jnp.exp2(m_prev - m_new)
    p = jnp.concatenate([jnp.exp2(s - m_new) for s in s_tiles], axis=1).astype(BF16)
    pv = jnp.dot(p, _ones_in_other_half(v2, head % 2), preferred_element_type=F32)
    acc_sc[head, rows] = alpha * acc_sc[head, rows] + pv


def _flash_finish(o_ref, acc_sc):
    low = _half_lanes(acc_sc.shape[1:], 0)
    for pair in range(acc_sc.shape[0] // 2):
        outs = [acc_sc[2 * pair + hh] / pltpu.roll(acc_sc[2 * pair + hh], HEAD_DIM, axis=1) for hh in range(2)]
        o_ref[:, pair * LANES:(pair + 1) * LANES] = jnp.where(low, outs[0], outs[1]).astype(o_ref.dtype)


def _lane_tiles(s):
    return [s[:, j * LANES:(j + 1) * LANES] for j in range(s.shape[1] // LANES)]


def _mask_from(tiles, first):
    r = lax.broadcasted_iota(jnp.int32, tiles[0].shape, 0)
    c = lax.broadcasted_iota(jnp.int32, tiles[0].shape, 1)
    return [x if j < first else jnp.where(c + (j - first) * LANES <= r, x, NEG_INF) for j, x in enumerate(tiles)]


def _flash_kernel(qi_ref, ki_ref, q_ref, k_ref, v_ref, o_ref, m_sc, acc_sc):
    t = pl.program_id(1)
    qi = qi_ref[t]
    ki = ki_ref[t]
    tq, tk = q_ref.shape[0], k_ref.shape[0]
    half = tq // 2
    nheads = q_ref.shape[1] // LANES

    @pl.when(ki == 0)
    def _():
        _flash_init(m_sc, acc_sc)

    def scores(head, rows, keys):
        slab = slice(head * LANES, (head + 1) * LANES)
        s = lax.dot_general(q_ref[rows, slab], k_ref[keys, slab], NT_DIMS, preferred_element_type=F32)
        return _lane_tiles(s)

    def step(diagonal):
        for head in range(nheads):
            pair = slice((head // 2) * LANES, (head // 2 + 1) * LANES)
            if not diagonal:
                _flash_update(scores(head, slice(0, tq), slice(0, tk)), v_ref[:, pair], head, slice(0, tq),
                              m_sc, acc_sc)
                continue
            top, bottom = slice(0, half), slice(half, tq)
            _flash_update(_mask_from(scores(head, top, top), 0), v_ref[top, pair], head, top, m_sc, acc_sc)
            _flash_update(_mask_from(scores(head, bottom, slice(0, tk)), half // LANES), v_ref[:, pair], head,
                          bottom, m_sc, acc_sc)

    @pl.when(ki < qi)
    def _():
        step(False)

    @pl.when(ki == qi)
    def _():
        step(True)
        _flash_finish(o_ref, acc_sc)


def _tri_tables(nq):
    qi = [q for q in range(nq) for _ in range(q + 1)]
    ki = [k for q in range(nq) for k in range(q + 1)]
    return jnp.asarray(qi, jnp.int32), jnp.asarray(ki, jnp.int32)


def _flash(q, k, v, batch, seq, name):
    T = q.shape[0]
    nheads = q.shape[1] // LANES
    tq = ATTN_TILE
    nq = seq // tq
    qi_tab, ki_tab = _tri_tables(nq)
    qrow = lambda b, t, qi, ki: (b * nq + qi[t], 0)
    krow = lambda b, t, qi, ki: (b * nq + ki[t], 0)
    grid_spec = pltpu.PrefetchScalarGridSpec(
        num_scalar_prefetch=2,
        grid=(batch, qi_tab.shape[0]),
        in_specs=[
            pl.BlockSpec((tq, q.shape[1]), qrow), pl.BlockSpec((tq, k.shape[1]), krow),
            pl.BlockSpec((tq, v.shape[1]), krow),
        ],
        out_specs=pl.BlockSpec((tq, v.shape[1]), qrow),
        scratch_shapes=[pltpu.VMEM((nheads, tq, LANES), F32),
                        pltpu.VMEM((nheads, tq, LANES), F32)],
    )
    return pl.pallas_call(
        _flash_kernel,
        grid_spec=grid_spec,
        out_shape=jax.ShapeDtypeStruct((T, v.shape[1]), BF16),
        compiler_params=_params("parallel", "arbitrary"),
        name=name,
    )(qi_tab, ki_tab, q, k, v)


def _mix_ffn_kernel(oa_ref, ob_ref, oc_ref, gg_ref, wout_ref, apg_ref, x_ref,
                    g_ref, wup_ref, cw_ref, cb_ref, wdown_ref, pg_ref, o_ref,
                    hbuf, ubuf, acc_ref, carry_ref):
    tm = x_ref.shape[0]

    gg = gg_ref[...]
    a = _rms(oa_ref[...].astype(F32), gg[:, 0:SWA_WIDTH]).astype(BF16)
    b = _rms(ob_ref[...].astype(F32), gg[:, SWA_WIDTH:SWA_WIDTH + FOX_WIDTH]).astype(BF16)
    c = _rms(oc_ref[...].astype(F32), gg[:, SWA_WIDTH + FOX_WIDTH:]).astype(BF16)
    y = jnp.dot(a, wout_ref[0:SWA_WIDTH, :], preferred_element_type=F32)
    y = y + jnp.dot(b, wout_ref[SWA_WIDTH:SWA_WIDTH + FOX_WIDTH, :], preferred_element_type=F32)
    y = y + jnp.dot(c, wout_ref[SWA_WIDTH + FOX_WIDTH:, :], preferred_element_type=F32)
    x1 = x_ref[...] + _rms(y, apg_ref[...])
    o_ref[...] = x1
    hbuf[...] = _rms(x1, g_ref[...]).astype(BF16)

    @pl.when(pl.program_id(1) == 0)
    def _():
        carry_ref[...] = jnp.zeros_like(carry_ref)

    row8 = lax.broadcasted_iota(jnp.int32, (8, 2 * FF_CHUNK), 0)

    def up_proj(c, slot):
        ubuf[slot] = jnp.dot(hbuf[...], wup_ref[c], preferred_element_type=F32)

    def conv_act_down(c, slot, first=False):
        u = ubuf[slot]
        prev = carry_ref[c]
        carry_ref[c] = u[tm - 8:tm, :]
        w = cw_ref[c]
        y = cb_ref[c] + w[2:3, :] * u
        for d in (1, 2):
            r = pltpu.roll(u, d, axis=0)
            head = jnp.where(row8 < d, pltpu.roll(prev, d, axis=0), r[0:8, :])
            r = jnp.concatenate([head, r[8:, :]], axis=0)
            y = y + w[2 - d:3 - d, :] * r
        act = (jax.nn.gelu(y[:, 0:FF_CHUNK], approximate=True) * y[:, FF_CHUNK:]).astype(BF16)
        down = jnp.dot(act, wdown_ref[c], preferred_element_type=F32)
        if first:
            acc_ref[...] = down
        else:
            acc_ref[...] += down

    up_proj(0, 0)
    up_proj(1, 1)
    conv_act_down(0, 0, first=True)

    def two_chunks(j, _):
        c = 2 * j + 1
        up_proj(c + 1, 0)
        conv_act_down(c, 1)
        up_proj(c + 2, 1)
        conv_act_down(c + 1, 0)
        return 0

    lax.fori_loop(0, (N_FF_CHUNKS - 3) // 2, two_chunks, 0)
    up_proj(N_FF_CHUNKS - 1, 0)
    conv_act_down(N_FF_CHUNKS - 2, 1)
    conv_act_down(N_FF_CHUNKS - 1, 0)
    o_ref[...] += _rms(acc_ref[...], pg_ref[...])


def _mix_ffn(oa, ob, oc, gg, wout, apg, x2, g, wup, cw, cb, wdown, pg, batch, seq):
    T = x2.shape[0]
    tm = MIX_TILE
    ns = seq // tm
    row = lambda b, s: (b * ns + s, 0)
    return pl.pallas_call(
        _mix_ffn_kernel,
        grid=(batch, ns),
        in_specs=[
            pl.BlockSpec((tm, SWA_WIDTH), row), pl.BlockSpec((tm, FOX_WIDTH), row),
            pl.BlockSpec((tm, MLA_WIDTH), row), _full(gg.shape), _full(wout.shape), _full(apg.shape),
            pl.BlockSpec((tm, D_MODEL), row), _full(g.shape), _full(wup.shape), _full(cw.shape),
            _full(cb.shape), _full(wdown.shape), _full(pg.shape),
        ],
        out_specs=pl.BlockSpec((tm, D_MODEL), row),
        out_shape=jax.ShapeDtypeStruct((T, D_MODEL), F32),
        scratch_shapes=[
            pltpu.VMEM((tm, D_MODEL), BF16),
            pltpu.VMEM((2, tm, 2 * FF_CHUNK), F32), pltpu.VMEM((tm, D_MODEL), F32),
            pltpu.VMEM((N_FF_CHUNKS, 8, 2 * FF_CHUNK), F32),
        ],
        compiler_params=_params("parallel", "arbitrary"),
        name="mix_ffn",
    )(oa, ob, oc, gg, wout, apg, x2, g, wup, cw, cb, wdown, pg)


def _t5_causal_bucket(dist):
    max_exact = REL_BUCKETS // 2
    d = jnp.maximum(dist, 0)
    log_ratio = jnp.log(jnp.maximum(d, 1).astype(F32) / max_exact) / math.log(REL_MAX_DIST / max_exact)
    large = max_exact + (log_ratio * (REL_BUCKETS - max_exact)).astype(jnp.int32)
    large = jnp.minimum(large, REL_BUCKETS - 1)
    return jnp.where(d < max_exact, d, large)


def _swa_bias_table(rel_bias):
    qi = jnp.arange(WINDOW, dtype=jnp.int32)[:, None] + WINDOW
    kj = jnp.arange(2 * WINDOW, dtype=jnp.int32)[None, :]
    dist = qi - kj
    in_band = (dist >= 0) & (dist < WINDOW)
    hit = _t5_causal_bucket(dist)[None, :, :] == jnp.arange(REL_BUCKETS, dtype=jnp.int32)[:, None, None]
    bias = jnp.sum(jnp.where(hit[None], rel_bias.astype(F32).T[:, :, None, None], 0.0), axis=1)
    bias = jnp.where(in_band[None], bias, NEG_INF)
    return bias.reshape(SWA_KV_HEADS, (SWA_Q_HEADS // SWA_KV_HEADS) * WINDOW, 2 * WINDOW)


def _rope_tables(seq):
    pos = jnp.arange(seq, dtype=F32)
    inv_freq = ROPE_THETA ** (-(jnp.arange(MLA_ROPE_DIM // 2, dtype=F32) * 2.0 / MLA_ROPE_DIM))
    ang = pos[:, None] * inv_freq[None, :]
    cos, sin = jnp.cos(ang), jnp.sin(ang)
    pad = jnp.zeros((seq, LANES - MLA_QK_DIM), F32)
    cos_t = jnp.concatenate([jnp.ones((seq, MLA_NOPE_DIM), F32), cos, cos, pad], axis=1)
    sin_t = jnp.concatenate([jnp.zeros((seq, MLA_NOPE_DIM), F32), -sin, sin, pad], axis=1)
    return cos_t, sin_t


def _swap_halves(w):
    half = w.shape[1] // 2
    return jnp.concatenate([w[:, half:], w[:, :half]], axis=1)


def _layer_params(l, w_in, forget_bias, w_uq, w_ukv, group_norm, w_out, w_up, conv_w, conv_b, w_down):
    wi = w_in[l]
    d = wi.shape[0]
    order = jnp.asarray(SWA_HEAD_ORDER)
    swa_q = wi[:, 0:SWA_WIDTH].reshape(d, SWA_Q_HEADS, HEAD_DIM)[:, order].reshape(d, SWA_WIDTH)
    wmain = jnp.concatenate([swa_q, wi[:, SWA_WIDTH:SWA_COLS + 3 * FOX_WIDTH]], axis=1).astype(BF16)

    f0 = SWA_COLS + 3 * FOX_WIDTH
    m0 = SWA_COLS + FOX_COLS
    z = lambda n: jnp.zeros((d, n), F32)
    w_kr = wi[:, m0 + MLA_Q_RANK + MLA_KV_RANK:]
    rope_pad = LANES - MLA_QK_DIM
    wside = jnp.concatenate([
        wi[:, m0:m0 + MLA_Q_RANK + MLA_KV_RANK],
        z(MLA_NOPE_DIM), w_kr, z(rope_pad),
        z(MLA_NOPE_DIM), _swap_halves(w_kr), z(rope_pad),
        wi[:, f0:f0 + FOX_HEADS], z(LANES - FOX_HEADS),
    ], axis=1).astype(BF16)
    fb = jnp.pad(forget_bias[l], (0, LANES - FOX_HEADS))[None, :]

    uq = w_uq[l].reshape(MLA_Q_RANK, MLA_HEADS, MLA_QK_DIM)
    zq = jnp.zeros((MLA_Q_RANK, MLA_HEADS, 1), F32)
    plain = jnp.concatenate([uq, jnp.tile(zq, (1, 1, rope_pad))], axis=2)
    rope = uq[:, :, MLA_NOPE_DIM:]
    swapped = jnp.concatenate([jnp.tile(zq, (1, 1, MLA_NOPE_DIM)), rope[:, :, MLA_ROPE_DIM // 2:],
                               rope[:, :, :MLA_ROPE_DIM // 2], jnp.tile(zq, (1, 1, rope_pad))], axis=2)
    wuq = jnp.concatenate([plain.reshape(MLA_Q_RANK, -1), swapped.reshape(MLA_Q_RANK, -1)], axis=1).astype(BF16)

    ukv = w_ukv[l].reshape(MLA_KV_RANK, MLA_HEADS, MLA_NOPE_DIM + MLA_V_DIM)
    k_slabs = jnp.concatenate([ukv[:, :, :MLA_NOPE_DIM],
                               jnp.zeros((MLA_KV_RANK, MLA_HEADS, LANES - MLA_NOPE_DIM), F32)], axis=2)
    wukv = jnp.concatenate([k_slabs.reshape(MLA_KV_RANK, -1),
                            ukv[:, :, MLA_NOPE_DIM:].reshape(MLA_KV_RANK, -1)], axis=1).astype(BF16)

    gg = group_norm[l]
    gg = jnp.concatenate([gg[:SWA_WIDTH].reshape(SWA_Q_HEADS, HEAD_DIM)[order].reshape(-1), gg[SWA_WIDTH:]])[None, :]
    wo = w_out[l]
    wo = jnp.concatenate([wo[:SWA_WIDTH].reshape(SWA_Q_HEADS, HEAD_DIM, -1)[order].reshape(SWA_WIDTH, -1),
                          wo[SWA_WIDTH:]], axis=0).astype(BF16)

    def gate_up(a):
        lead = a.shape[:-1]
        a = a.reshape(lead + (2, N_FF_CHUNKS, FF_CHUNK))
        a = jnp.moveaxis(a, -2, 0)
        return a.reshape((N_FF_CHUNKS,) + lead + (2 * FF_CHUNK,))

    wup = gate_up(w_up[l].astype(BF16))
    cw = gate_up(conv_w[l])
    cb = gate_up(conv_b[l][None, :])
    wdown = w_down[l].astype(BF16).reshape(N_FF_CHUNKS, FF_CHUNK, -1)
    return wmain, wside, fb, wuq, wukv, gg, wo, wup, cw, cb, wdown


def kernel(x, attn_pre_norm, w_in, forget_bias, swa_sinks, rel_bias, q_latent_norm, w_uq, kv_latent_norm, w_ukv,
           group_norm, w_out, attn_post_norm, ffn_pre_norm, w_up, conv_w, conv_b, w_down, ffn_post_norm):
    batch, seq, d = x.shape
    assert d == D_MODEL and all(seq % t == 0 for t in (ROW_TILE, ATTN_TILE, IN_PROJ_TILE, MIX_TILE))
    depth = w_in.shape[0]
    cos_t, sin_t = _rope_tables(seq)
    bias = _swa_bias_table(rel_bias)
    x2 = x.reshape(batch * seq, d)
    for l in range(depth):
        wmain, wside, fb, wuq, wukv, gg, wo, wup, cw, cb, wdown = _layer_params(
            l, w_in, forget_bias, w_uq, w_ukv, group_norm, w_out, w_up, conv_w, conv_b, w_down)
        sinks = jnp.repeat(swa_sinks[l].astype(F32).reshape(SWA_KV_HEADS, -1), WINDOW, axis=1)
        sinks = jnp.broadcast_to(sinks[:, :, None], sinks.shape + (LANES,))
        (swa_q, swa_k, swa_v, fox_q, fox_k, fox_v, mla_q, mla_k, mla_v) = _in_proj(
            x2, attn_pre_norm[l][None, :], wmain, wside, fb, q_latent_norm[l][None, :], wuq,
            kv_latent_norm[l][None, :], wukv, cos_t, sin_t, batch, seq)
        out_a = _swa(swa_q, swa_k, swa_v, bias, sinks, batch, seq)
        out_b = _flash(fox_q, fox_k, fox_v, batch, seq, "fox_attn")
        out_c = _flash(mla_q, mla_k, mla_v, batch, seq, "mla_attn")
        x2 = _mix_ffn(out_a, out_b, out_c, gg, wo, attn_post_norm[l][None, :], x2, ffn_pre_norm[l][None, :],
                      wup, cw, cb, wdown, ffn_post_norm[l][None, :], batch, seq)
    return x2.reshape(batch, seq, d)
```

```python
import functools
import math

import jax
import jax.numpy as jnp
from jax import lax
from jax.experimental import pallas as pl
from jax.experimental.pallas import tpu as pltpu

D_MODEL = 1024
HEAD_DIM = 64
SWA_Q_HEADS = 8
SWA_KV_HEADS = 2
WINDOW = 128
FOX_HEADS = 4
MLA_HEADS = 4
MLA_Q_RANK = 256
MLA_KV_RANK = 128
MLA_NOPE_DIM = 64
MLA_ROPE_DIM = 32
MLA_V_DIM = 64
MLA_QK_DIM = MLA_NOPE_DIM + MLA_ROPE_DIM
ROPE_THETA = 10000.0
REL_BUCKETS = 32
REL_MAX_DIST = 128
D_FF = 2816
EPS = 1e-6
NEG_INF = -1e30

SWA_WIDTH = SWA_Q_HEADS * HEAD_DIM
FOX_WIDTH = FOX_HEADS * HEAD_DIM
MLA_WIDTH = MLA_HEADS * MLA_V_DIM
SWA_COLS = (SWA_Q_HEADS + 2 * SWA_KV_HEADS) * HEAD_DIM
FOX_COLS = 3 * FOX_WIDTH + FOX_HEADS

LANES = 128
ROW_TILE = 512
IN_PROJ_TILE = 1024
MIX_TILE = 1024
ATTN_TILE = 1024
FF_CHUNK = 256
N_FF_CHUNKS = D_FF // FF_CHUNK
VMEM_LIMIT = 60 * 1024 * 1024

SWA_HEAD_ORDER = (0, 4, 1, 5, 2, 6, 3, 7)

F32 = jnp.float32
BF16 = jnp.bfloat16
NT_DIMS = (((1,), (1,)), ((), ()))
LOG2E = math.log2(math.e)
GELU_K0 = math.sqrt(2.0 / math.pi)
GELU_K1 = GELU_K0 * 0.044715


def _rms(x, g):
    ms = jnp.mean(x * x, axis=-1, keepdims=True)
    return x * lax.rsqrt(ms + EPS) * g


def _params(*sem):
    return pltpu.CompilerParams(dimension_semantics=sem, vmem_limit_bytes=VMEM_LIMIT)


def _full(shape):
    nd = len(shape)
    return pl.BlockSpec(shape, lambda *_: (0,) * nd)


def _in_proj_kernel(x_ref, g_ref, wmain_ref, wside_ref, fb_ref, qg_ref, wuq_ref, kvg_ref, wukv_ref,
                    cos_ref, sin_ref,
                    swaq_ref, swak_ref, swav_ref, foxq_ref, foxk_ref, foxv_ref,
                    mlaq_ref, mlak_ref, mlav_ref, carry_ref):
    tm = x_ref.shape[0]
    h = _rms(x_ref[...], g_ref[...]).astype(BF16)

    main = jnp.dot(h, wmain_ref[...], preferred_element_type=F32)
    scale = HEAD_DIM ** -0.5
    swaq_ref[...] = (main[:, 0:512] * (scale * LOG2E)).astype(BF16)
    swak_ref[...] = main[:, 512:640].astype(BF16)
    swav_ref[...] = main[:, 640:768].astype(BF16)
    foxv_ref[...] = main[:, 1280:1536].astype(BF16)

    side = jnp.dot(h, wside_ref[...], preferred_element_type=F32)
    cos = cos_ref[...]
    sin = sin_ref[...]

    cqn = _rms(side[:, 0:256], qg_ref[...]).astype(BF16)
    qa = jnp.dot(cqn, wuq_ref[...], preferred_element_type=F32)
    for hh in range(MLA_HEADS):
        lo = hh * LANES
        a = qa[:, lo:lo + LANES]
        b = qa[:, 512 + lo:512 + lo + LANES]
        mlaq_ref[:, lo:lo + LANES] = ((a * cos + b * sin) * (MLA_QK_DIM ** -0.5 * LOG2E)).astype(BF16)

    ckvn = _rms(side[:, 256:384], kvg_ref[...]).astype(BF16)
    kv = jnp.dot(ckvn, wukv_ref[...], preferred_element_type=F32)
    krot = side[:, 384:512] * cos + side[:, 512:640] * sin
    for hh in range(MLA_HEADS):
        lo = hh * LANES
        mlak_ref[:, lo:lo + LANES] = (kv[:, lo:lo + LANES] + krot).astype(BF16)
    mlav_ref[...] = kv[:, 512:768].astype(BF16)

    lane = lax.broadcasted_iota(jnp.int32, (tm, LANES), 1)
    row = lax.broadcasted_iota(jnp.int32, (tm, LANES), 0)
    c = jnp.where(lane < FOX_HEADS, jax.nn.log_sigmoid(side[:, 640:768] + fb_ref[...]), 0.0)
    shift = 1
    while shift < tm:
        c = c + jnp.where(row >= shift, pltpu.roll(c, shift, axis=0), 0.0)
        shift *= 2

    @pl.when(pl.program_id(1) == 0)
    def _():
        carry_ref[...] = jnp.zeros_like(carry_ref)

    c = c + carry_ref[0:1, :]
    carry_ref[...] = jnp.broadcast_to(c[tm - 1:tm, :], carry_ref.shape)

    for hd in range(FOX_HEADS):
        pair, half = hd // 2, hd % 2
        o = 0 if half else HEAD_DIM
        f = jnp.broadcast_to(jnp.sum(jnp.where(lane == hd, c, 0.0), axis=1, keepdims=True), c.shape) * LOG2E
        hi = f.astype(BF16).astype(F32)
        mid = (f - hi).astype(BF16).astype(F32)
        low = (f - hi) - mid
        one_k = jnp.where(jnp.logical_and(lane >= o + 3, lane < o + 6), 1.0, 0.0)
        one_q = jnp.where(jnp.logical_and(lane >= o, lane < o + 3), 1.0, 0.0)
        extra_k = jnp.where(lane == o, -hi, jnp.where(lane == o + 1, -mid, jnp.where(lane == o + 2, -low, one_k)))
        extra_q = jnp.where(lane == o + 3, hi, jnp.where(lane == o + 4, mid, jnp.where(lane == o + 5, low, one_q)))
        data = _half_lanes(c.shape, half)
        slab = slice(hd * LANES, (hd + 1) * LANES)
        q_pair = main[:, 768 + pair * LANES:768 + (pair + 1) * LANES] * (scale * LOG2E)
        k_pair = main[:, 1024 + pair * LANES:1024 + (pair + 1) * LANES]
        foxq_ref[:, slab] = jnp.where(data, q_pair, extra_q).astype(BF16)
        foxk_ref[:, slab] = jnp.where(data, k_pair, extra_k).astype(BF16)


def _in_proj(x2, g, wmain, wside, fb, qg, wuq, kvg, wukv, cos_t, sin_t, batch, seq):
    T = x2.shape[0]
    tm = IN_PROJ_TILE
    ns = seq // tm
    row = lambda b, s: (b * ns + s, 0)
    seqrow = lambda b, s: (s, 0)
    out_shape = [
        jax.ShapeDtypeStruct((T, 512), BF16), jax.ShapeDtypeStruct((T, 128), BF16),
        jax.ShapeDtypeStruct((T, 128), BF16), jax.ShapeDtypeStruct((T, 512), BF16),
        jax.ShapeDtypeStruct((T, 512), BF16), jax.ShapeDtypeStruct((T, 256), BF16),
        jax.ShapeDtypeStruct((T, 512), BF16), jax.ShapeDtypeStruct((T, 512), BF16),
        jax.ShapeDtypeStruct((T, 256), BF16),
    ]
    out_specs = [
        pl.BlockSpec((tm, 512), row), pl.BlockSpec((tm, 128), row), pl.BlockSpec((tm, 128), row),
        pl.BlockSpec((tm, 512), row), pl.BlockSpec((tm, 512), row), pl.BlockSpec((tm, 256), row),
        pl.BlockSpec((tm, 512), row), pl.BlockSpec((tm, 512), row), pl.BlockSpec((tm, 256), row),
    ]
    in_specs = [
        pl.BlockSpec((tm, D_MODEL), row), _full(g.shape), _full(wmain.shape), _full(wside.shape),
        _full(fb.shape), _full(qg.shape), _full(wuq.shape), _full(kvg.shape), _full(wukv.shape),
        pl.BlockSpec((tm, LANES), seqrow), pl.BlockSpec((tm, LANES), seqrow),
    ]
    return pl.pallas_call(
        _in_proj_kernel,
        grid=(batch, ns),
        in_specs=in_specs,
        out_specs=out_specs,
        out_shape=out_shape,
        scratch_shapes=[pltpu.VMEM((8, LANES), F32)],
        compiler_params=_params("arbitrary", "arbitrary"),
        name="in_proj",
    )(x2, g, wmain, wside, fb, qg, wuq, kvg, wukv, cos_t, sin_t)


def _half_lanes(shape, hh):
    lane = lax.broadcasted_iota(jnp.int32, shape, len(shape) - 1)
    return (lane < HEAD_DIM) if hh == 0 else (lane >= HEAD_DIM)


def _ones_in_other_half(v2, hh):
    return jnp.where(_half_lanes(v2.shape, hh), v2, jnp.ones_like(v2))


def _swa_kernel(q_ref, kc_ref, kp_ref, vc_ref, vp_ref, bias_ref, sink_ref, o_ref, kbuf, vbuf):
    first_tile = pl.program_id(1) == 0
    nwin = q_ref.shape[0] // WINDOW
    npair = SWA_Q_HEADS // 2
    kbuf[0:WINDOW, :] = kp_ref[...]
    kbuf[WINDOW:, :] = kc_ref[...]
    vbuf[0:WINDOW, :] = vp_ref[...]
    vbuf[WINDOW:, :] = vc_ref[...]
    kj = lax.broadcasted_iota(jnp.int32, (1, 2 * WINDOW), 1)
    no_prev = jnp.where(jnp.logical_and(kj < WINDOW, first_tile), NEG_INF, 0.0)
    low = _half_lanes((WINDOW, LANES), 0)
    for w in range(nwin):
        kw = kbuf[w * WINDOW:(w + 2) * WINDOW, :]
        vw = vbuf[w * WINDOW:(w + 2) * WINDOW, :]
        rows = slice(w * WINDOW, (w + 1) * WINDOW)
        q4 = [q_ref[rows, p * LANES:(p + 1) * LANES] for p in range(npair)]
        outs = []
        for hh in range(2):
            mine = _half_lanes((WINDOW, LANES), hh)
            qs = jnp.concatenate([jnp.where(mine, q, jnp.zeros_like(q)) for q in q4], axis=0)
            s = lax.dot_general(qs, kw, NT_DIMS, preferred_element_type=F32) + bias_ref[hh]
            if w == 0:
                s = s + no_prev
            sink = sink_ref[hh]
            s0, s1 = s[:, 0:LANES], s[:, LANES:]
            mloc = jnp.max(jnp.maximum(s0, s1), axis=1, keepdims=True)
            m = jnp.maximum(jnp.broadcast_to(mloc, sink.shape), sink)
            e = jnp.concatenate([jnp.exp2(s0 - m), jnp.exp2(s1 - m)], axis=1).astype(BF16)
            pv = jnp.dot(e, _ones_in_other_half(vw, hh), preferred_element_type=F32)
            den = pltpu.roll(pv, HEAD_DIM, axis=1) + jnp.exp2(sink - m)
            outs.append(pv / den)
        for p in range(npair):
            pr = slice(p * WINDOW, (p + 1) * WINDOW)
            o_ref[rows, p * LANES:(p + 1) * LANES] = jnp.where(low, outs[0][pr], outs[1][pr]).astype(o_ref.dtype)


def _swa(q, k, v, bias, sinks, batch, seq):
    T = q.shape[0]
    tm = ROW_TILE
    ns = seq // tm
    per = tm // WINDOW
    row = lambda b, s: (b * ns + s, 0)
    prev = lambda b, s: (jnp.maximum((b * ns + s) * per - 1, 0), 0)
    return pl.pallas_call(
        _swa_kernel,
        grid=(batch, ns),
        in_specs=[
            pl.BlockSpec((tm, SWA_WIDTH), row),
            pl.BlockSpec((tm, LANES), row), pl.BlockSpec((WINDOW, LANES), prev),
            pl.BlockSpec((tm, LANES), row), pl.BlockSpec((WINDOW, LANES), prev),
            _full(bias.shape), _full(sinks.shape),
        ],
        out_specs=pl.BlockSpec((tm, SWA_WIDTH), row),
        out_shape=jax.ShapeDtypeStruct((T, SWA_WIDTH), BF16),
        scratch_shapes=[pltpu.VMEM((tm + WINDOW, LANES), BF16), pltpu.VMEM((tm + WINDOW, LANES), BF16)],
        compiler_params=_params("parallel", "arbitrary"),
        name="swa_attn",
    )(q, k, k, v, v, bias, sinks)


def _flash_init(m_sc, acc_sc):
    m_sc[...] = jnp.full(m_sc.shape, NEG_INF, F32)
    acc_sc[...] = jnp.zeros(acc_sc.shape, F32)


def _flash_update(s_tiles, v2, head, rows, m_sc, acc_sc):
    mloc = functools.reduce(jnp.maximum, s_tiles)
    mloc = jnp.broadcast_to(jnp.max(mloc, axis=1, keepdims=True), mloc.shape)
    m_prev = m_sc[head, rows]
    m_new = jnp.maximum(m_prev, mloc)
    m_sc[head, rows] = m_new
    alpha = jnp.exp2(m_prev - m_new)
    p = jnp.concatenate([jnp.exp2(s - m_new) for s in s_tiles], axis=1).astype(BF16)
    pv = jnp.dot(p, _ones_in_other_half(v2, head % 2), preferred_element_type=F32)
    acc_sc[head, rows] = alpha * acc_sc[head, rows] + pv


def _flash_finish(o_ref, acc_sc):
    low = _half_lanes(acc_sc.shape[1:], 0)
    for pair in range(acc_sc.shape[0] // 2):
        outs = [acc_sc[2 * pair + hh] / pltpu.roll(acc_sc[2 * pair + hh], HEAD_DIM, axis=1) for hh in range(2)]
        o_ref[:, pair * LANES:(pair + 1) * LANES] = jnp.where(low, outs[0], outs[1]).astype(o_ref.dtype)


def _lane_tiles(s):
    return [s[:, j * LANES:(j + 1) * LANES] for j in range(s.shape[1] // LANES)]


def _mask_from(tiles, first):
    r = lax.broadcasted_iota(jnp.int32, tiles[0].shape, 0)
    c = lax.broadcasted_iota(jnp.int32, tiles[0].shape, 1)
    return [x if j < first else jnp.where(c + (j - first) * LANES <= r, x, NEG_INF) for j, x in enumerate(tiles)]


def _flash_kernel(qi_ref, ki_ref, q_ref, k_ref, v_ref, o_ref, m_sc, acc_sc):
    t = pl.program_id(1)
    qi = qi_ref[t]
    ki = ki_ref[t]
    tq, tk = q_ref.shape[0], k_ref.shape[0]
    half = tq // 2
    nheads = q_ref.shape[1] // LANES

    @pl.when(ki == 0)
    def _():
        _flash_init(m_sc, acc_sc)

    def scores(head, rows, keys):
        slab = slice(head * LANES, (head + 1) * LANES)
        s = lax.dot_general(q_ref[rows, slab], k_ref[keys, slab], NT_DIMS, preferred_element_type=F32)
        return _lane_tiles(s)

    def step(diagonal):
        for head in range(nheads):
            pair = slice((head // 2) * LANES, (head // 2 + 1) * LANES)
            if not diagonal:
                _flash_update(scores(head, slice(0, tq), slice(0, tk)), v_ref[:, pair], head, slice(0, tq),
                              m_sc, acc_sc)
                continue
            top, bottom = slice(0, half), slice(half, tq)
            _flash_update(_mask_from(scores(head, top, top), 0), v_ref[top, pair], head, top, m_sc, acc_sc)
            _flash_update(_mask_from(scores(head, bottom, slice(0, tk)), half // LANES), v_ref[:, pair], head,
                          bottom, m_sc, acc_sc)

    @pl.when(ki < qi)
    def _():
        step(False)

    @pl.when(ki == qi)
    def _():
        step(True)
        _flash_finish(o_ref, acc_sc)


def _tri_tables(nq):
    qi = [q for q in range(nq) for _ in range(q + 1)]
    ki = [k for q in range(nq) for k in range(q + 1)]
    return jnp.asarray(qi, jnp.int32), jnp.asarray(ki, jnp.int32)


def _flash(q, k, v, batch, seq, name):
    T = q.shape[0]
    nheads = q.shape[1] // LANES
    tq = ATTN_TILE
    nq = seq // tq
    qi_tab, ki_tab = _tri_tables(nq)
    qrow = lambda b, t, qi, ki: (b * nq + qi[t], 0)
    krow = lambda b, t, qi, ki: (b * nq + ki[t], 0)
    grid_spec = pltpu.PrefetchScalarGridSpec(
        num_scalar_prefetch=2,
        grid=(batch, qi_tab.shape[0]),
        in_specs=[
            pl.BlockSpec((tq, q.shape[1]), qrow), pl.BlockSpec((tq, k.shape[1]), krow),
            pl.BlockSpec((tq, v.shape[1]), krow),
        ],
        out_specs=pl.BlockSpec((tq, v.shape[1]), qrow),
        scratch_shapes=[pltpu.VMEM((nheads, tq, LANES), F32),
                        pltpu.VMEM((nheads, tq, LANES), F32)],
    )
    return pl.pallas_call(
        _flash_kernel,
        grid_spec=grid_spec,
        out_shape=jax.ShapeDtypeStruct((T, v.shape[1]), BF16),
        compiler_params=_params("parallel", "arbitrary"),
        name=name,
    )(qi_tab, ki_tab, q, k, v)


def _mix_ffn_kernel(oa_ref, ob_ref, oc_ref, gg_ref, wout_ref, apg_ref, x_ref,
                    g_ref, wup_ref, cw_ref, cb_ref, wdown_ref, pg_ref, o_ref,
                    hbuf, ubuf, acc_ref, carry_ref):
    tm = x_ref.shape[0]

    gg = gg_ref[...]
    a = _rms(oa_ref[...].astype(F32), gg[:, 0:SWA_WIDTH]).astype(BF16)
    b = _rms(ob_ref[...].astype(F32), gg[:, SWA_WIDTH:SWA_WIDTH + FOX_WIDTH]).astype(BF16)
    c = _rms(oc_ref[...].astype(F32), gg[:, SWA_WIDTH + FOX_WIDTH:]).astype(BF16)
    y = jnp.dot(a, wout_ref[0:SWA_WIDTH, :], preferred_element_type=F32)
    y = y + jnp.dot(b, wout_ref[SWA_WIDTH:SWA_WIDTH + FOX_WIDTH, :], preferred_element_type=F32)
    y = y + jnp.dot(c, wout_ref[SWA_WIDTH + FOX_WIDTH:, :], preferred_element_type=F32)
    x1 = x_ref[...] + _rms(y, apg_ref[...])
    o_ref[...] = x1
    hbuf[...] = _rms(x1, g_ref[...]).astype(BF16)

    @pl.when(pl.program_id(1) == 0)
    def _():
        carry_ref[...] = jnp.zeros_like(carry_ref)

    row8 = lax.broadcasted_iota(jnp.int32, (8, 2 * FF_CHUNK), 0)

    def up_proj(c, slot):
        ubuf[slot] = jnp.dot(hbuf[...], wup_ref[c], preferred_element_type=F32)

    def conv_act_down(c, slot, first=False):
        u = ubuf[slot]
        prev = carry_ref[c]
        carry_ref[c] = u[tm - 8:tm, :]
        w = cw_ref[c]
        y = cb_ref[c] + w[2:3, :] * u
        for d in (1, 2):
            r = pltpu.roll(u, d, axis=0)
            head = jnp.where(row8 < d, pltpu.roll(prev, d, axis=0), r[0:8, :])
            r = jnp.concatenate([head, r[8:, :]], axis=0)
            y = y + w[2 - d:3 - d, :] * r
        gate, half_up = y[:, 0:FF_CHUNK], y[:, FF_CHUNK:]
        t = jnp.tanh(gate * (GELU_K0 + GELU_K1 * (gate * gate)))
        act = ((gate * half_up) * (1.0 + t)).astype(BF16)
        down = jnp.dot(act, wdown_ref[c], preferred_element_type=F32)
        if first:
            acc_ref[...] = down
        else:
            acc_ref[...] += down

    up_proj(0, 0)
    up_proj(1, 1)
    conv_act_down(0, 0, first=True)

    def two_chunks(j, _):
        c = 2 * j + 1
        up_proj(c + 1, 0)
        conv_act_down(c, 1)
        up_proj(c + 2, 1)
        conv_act_down(c + 1, 0)
        return 0

    lax.fori_loop(0, (N_FF_CHUNKS - 3) // 2, two_chunks, 0)
    up_proj(N_FF_CHUNKS - 1, 0)
    conv_act_down(N_FF_CHUNKS - 2, 1)
    conv_act_down(N_FF_CHUNKS - 1, 0)
    o_ref[...] += _rms(acc_ref[...], pg_ref[...])


def _mix_ffn(oa, ob, oc, gg, wout, apg, x2, g, wup, cw, cb, wdown, pg, batch, seq):
    T = x2.shape[0]
    tm = MIX_TILE
    ns = seq // tm
    row = lambda b, s: (b * ns + s, 0)
    return pl.pallas_call(
        _mix_ffn_kernel,
        grid=(batch, ns),
        in_specs=[
            pl.BlockSpec((tm, SWA_WIDTH), row), pl.BlockSpec((tm, FOX_WIDTH), row),
            pl.BlockSpec((tm, MLA_WIDTH), row), _full(gg.shape), _full(wout.shape), _full(apg.shape),
            pl.BlockSpec((tm, D_MODEL), row), _full(g.shape), _full(wup.shape), _full(cw.shape),
            _full(cb.shape), _full(wdown.shape), _full(pg.shape),
        ],
        out_specs=pl.BlockSpec((tm, D_MODEL), row),
        out_shape=jax.ShapeDtypeStruct((T, D_MODEL), F32),
        scratch_shapes=[
            pltpu.VMEM((tm, D_MODEL), BF16),
            pltpu.VMEM((2, tm, 2 * FF_CHUNK), F32),
            pltpu.VMEM((tm, D_MODEL), F32),
            pltpu.VMEM((N_FF_CHUNKS, 8, 2 * FF_CHUNK), F32),
        ],
        compiler_params=_params("parallel", "arbitrary"),
        name="mix_ffn",
    )(oa, ob, oc, gg, wout, apg, x2, g, wup, cw, cb, wdown, pg)


def _t5_causal_bucket(dist):
    max_exact = REL_BUCKETS // 2
    d = jnp.maximum(dist, 0)
    log_ratio = jnp.log(jnp.maximum(d, 1).astype(F32) / max_exact) / math.log(REL_MAX_DIST / max_exact)
    large = max_exact + (log_ratio * (REL_BUCKETS - max_exact)).astype(jnp.int32)
    large = jnp.minimum(large, REL_BUCKETS - 1)
    return jnp.where(d < max_exact, d, large)


def _swa_bias_table(rel_bias):
    qi = jnp.arange(WINDOW, dtype=jnp.int32)[:, None] + WINDOW
    kj = jnp.arange(2 * WINDOW, dtype=jnp.int32)[None, :]
    dist = qi - kj
    in_band = (dist >= 0) & (dist < WINDOW)
    hit = _t5_causal_bucket(dist)[None, :, :] == jnp.arange(REL_BUCKETS, dtype=jnp.int32)[:, None, None]
    bias = jnp.sum(jnp.where(hit[None], rel_bias.astype(F32).T[:, :, None, None], 0.0), axis=1)
    bias = jnp.where(in_band[None], bias, NEG_INF)
    bias = bias * LOG2E
    return bias.reshape(SWA_KV_HEADS, (SWA_Q_HEADS // SWA_KV_HEADS) * WINDOW, 2 * WINDOW)


def _rope_tables(seq):
    pos = jnp.arange(seq, dtype=F32)
    inv_freq = ROPE_THETA ** (-(jnp.arange(MLA_ROPE_DIM // 2, dtype=F32) * 2.0 / MLA_ROPE_DIM))
    ang = pos[:, None] * inv_freq[None, :]
    cos, sin = jnp.cos(ang), jnp.sin(ang)
    pad = jnp.zeros((seq, LANES - MLA_QK_DIM), F32)
    cos_t = jnp.concatenate([jnp.ones((seq, MLA_NOPE_DIM), F32), cos, cos, pad], axis=1)
    sin_t = jnp.concatenate([jnp.zeros((seq, MLA_NOPE_DIM), F32), -sin, sin, pad], axis=1)
    return cos_t, sin_t


def _swap_halves(w):
    half = w.shape[1] // 2
    return jnp.concatenate([w[:, half:], w[:, :half]], axis=1)


def _layer_params(l, w_in, forget_bias, w_uq, w_ukv, group_norm, w_out, w_up, conv_w, conv_b, w_down):
    wi = w_in[l]
    d = wi.shape[0]
    order = jnp.asarray(SWA_HEAD_ORDER)
    swa_q = wi[:, 0:SWA_WIDTH].reshape(d, SWA_Q_HEADS, HEAD_DIM)[:, order].reshape(d, SWA_WIDTH)
    wmain = jnp.concatenate([swa_q, wi[:, SWA_WIDTH:SWA_COLS + 3 * FOX_WIDTH]], axis=1).astype(BF16)

    f0 = SWA_COLS + 3 * FOX_WIDTH
    m0 = SWA_COLS + FOX_COLS
    z = lambda n: jnp.zeros((d, n), F32)
    w_kr = wi[:, m0 + MLA_Q_RANK + MLA_KV_RANK:]
    rope_pad = LANES - MLA_QK_DIM
    wside = jnp.concatenate([
        wi[:, m0:m0 + MLA_Q_RANK + MLA_KV_RANK],
        z(MLA_NOPE_DIM), w_kr, z(rope_pad),
        z(MLA_NOPE_DIM), _swap_halves(w_kr), z(rope_pad),
        wi[:, f0:f0 + FOX_HEADS], z(LANES - FOX_HEADS),
    ], axis=1).astype(BF16)
    fb = jnp.pad(forget_bias[l], (0, LANES - FOX_HEADS))[None, :]

    uq = w_uq[l].reshape(MLA_Q_RANK, MLA_HEADS, MLA_QK_DIM)
    zq = jnp.zeros((MLA_Q_RANK, MLA_HEADS, 1), F32)
    plain = jnp.concatenate([uq, jnp.tile(zq, (1, 1, rope_pad))], axis=2)
    rope = uq[:, :, MLA_NOPE_DIM:]
    swapped = jnp.concatenate([jnp.tile(zq, (1, 1, MLA_NOPE_DIM)), rope[:, :, MLA_ROPE_DIM // 2:],
                               rope[:, :, :MLA_ROPE_DIM // 2], jnp.tile(zq, (1, 1, rope_pad))], axis=2)
    wuq = jnp.concatenate([plain.reshape(MLA_Q_RANK, -1), swapped.reshape(MLA_Q_RANK, -1)], axis=1).astype(BF16)

    ukv = w_ukv[l].reshape(MLA_KV_RANK, MLA_HEADS, MLA_NOPE_DIM + MLA_V_DIM)
    k_slabs = jnp.concatenate([ukv[:, :, :MLA_NOPE_DIM],
                               jnp.zeros((MLA_KV_RANK, MLA_HEADS, LANES - MLA_NOPE_DIM), F32)], axis=2)
    wukv = jnp.concatenate([k_slabs.reshape(MLA_KV_RANK, -1),
                            ukv[:, :, MLA_NOPE_DIM:].reshape(MLA_KV_RANK, -1)], axis=1).astype(BF16)

    gg = group_norm[l]
    gg = jnp.concatenate([gg[:SWA_WIDTH].reshape(SWA_Q_HEADS, HEAD_DIM)[order].reshape(-1), gg[SWA_WIDTH:]])[None, :]
    wo = w_out[l]
    wo = jnp.concatenate([wo[:SWA_WIDTH].reshape(SWA_Q_HEADS, HEAD_DIM, -1)[order].reshape(SWA_WIDTH, -1),
                          wo[SWA_WIDTH:]], axis=0).astype(BF16)

    def gate_up(a):
        lead = a.shape[:-1]
        a = a.reshape(lead + (2, N_FF_CHUNKS, FF_CHUNK))
        a = jnp.moveaxis(a, -2, 0)
        return a.reshape((N_FF_CHUNKS,) + lead + (2 * FF_CHUNK,))

    wup = gate_up(w_up[l].astype(BF16))
    halve_up = jnp.concatenate([jnp.ones((D_FF,), F32), jnp.full((D_FF,), 0.5, F32)])
    cw = gate_up(conv_w[l] * halve_up)
    cb = gate_up((conv_b[l] * halve_up)[None, :])
    wdown = w_down[l].astype(BF16).reshape(N_FF_CHUNKS, FF_CHUNK, -1)
    return wmain, wside, fb, wuq, wukv, gg, wo, wup, cw, cb, wdown


def kernel(x, attn_pre_norm, w_in, forget_bias, swa_sinks, rel_bias, q_latent_norm, w_uq, kv_latent_norm, w_ukv,
           group_norm, w_out, attn_post_norm, ffn_pre_norm, w_up, conv_w, conv_b, w_down, ffn_post_norm):
    batch, seq, d = x.shape
    assert d == D_MODEL and all(seq % t == 0 for t in (ROW_TILE, ATTN_TILE, IN_PROJ_TILE, MIX_TILE))
    depth = w_in.shape[0]
    cos_t, sin_t = _rope_tables(seq)
    bias = _swa_bias_table(rel_bias)
    x2 = x.reshape(batch * seq, d)
    for l in range(depth):
        wmain, wside, fb, wuq, wukv, gg, wo, wup, cw, cb, wdown = _layer_params(
            l, w_in, forget_bias, w_uq, w_ukv, group_norm, w_out, w_up, conv_w, conv_b, w_down)
        sinks = jnp.repeat((swa_sinks[l].astype(F32) * LOG2E).reshape(SWA_KV_HEADS, -1), WINDOW, axis=1)
        sinks = jnp.broadcast_to(sinks[:, :, None], sinks.shape + (LANES,))
        (swa_q, swa_k, swa_v, fox_q, fox_k, fox_v, mla_q, mla_k, mla_v) = _in_proj(
            x2, attn_pre_norm[l][None, :], wmain, wside, fb, q_latent_norm[l][None, :], wuq,
            kv_latent_norm[l][None, :], wukv, cos_t, sin_t, batch, seq)
        out_a = _swa(swa_q, swa_k, swa_v, bias, sinks, batch, seq)
        out_b = _flash(fox_q, fox_k, fox_v, batch, seq, "fox_attn")
        out_c = _flash(mla_q, mla_k, mla_v, batch, seq, "mla_attn")
        x2 = _mix_ffn(out_a, out_b, out_c, gg, wo, attn_post_norm[l][None, :], x2, ffn_pre_norm[l][None, :],
                      wup, cw, cb, wdown, ffn_post_norm[l][None, :], batch, seq)
    return x2.reshape(batch, seq, d)
```

```python
import functools
import math

import jax
import jax.numpy as jnp
from jax import lax
from jax.experimental import pallas as pl
from jax.experimental.pallas import tpu as pltpu

D_MODEL = 1024
HEAD_DIM = 64
SWA_Q_HEADS = 8
SWA_KV_HEADS = 2
WINDOW = 128
FOX_HEADS = 4
MLA_HEADS = 4
MLA_Q_RANK = 256
MLA_KV_RANK = 128
MLA_NOPE_DIM = 64
MLA_ROPE_DIM = 32
MLA_V_DIM = 64
MLA_QK_DIM = MLA_NOPE_DIM + MLA_ROPE_DIM
ROPE_THETA = 10000.0
REL_BUCKETS = 32
REL_MAX_DIST = 128
D_FF = 2816
EPS = 1e-6
NEG_INF = -1e30

SWA_WIDTH = SWA_Q_HEADS * HEAD_DIM
FOX_WIDTH = FOX_HEADS * HEAD_DIM
MLA_WIDTH = MLA_HEADS * MLA_V_DIM
SWA_COLS = (SWA_Q_HEADS + 2 * SWA_KV_HEADS) * HEAD_DIM
FOX_COLS = 3 * FOX_WIDTH + FOX_HEADS

LANES = 128
ROW_TILE = 512
IN_PROJ_TILE = 1024
MIX_TILE = 1024
ATTN_TILE = 1024
FF_CHUNK = 256
N_FF_CHUNKS = D_FF // FF_CHUNK
VMEM_LIMIT = 60 * 1024 * 1024

SWA_HEAD_ORDER = (0, 4, 1, 5, 2, 6, 3, 7)

F32 = jnp.float32
BF16 = jnp.bfloat16
NT_DIMS = (((1,), (1,)), ((), ()))
LOG2E = math.log2(math.e)
GELU_K0 = math.sqrt(2.0 / math.pi)
GELU_K1 = GELU_K0 * 0.044715


def _rms(x, g):
    ms = jnp.mean(x * x, axis=-1, keepdims=True)
    return x * lax.rsqrt(ms + EPS) * g


def _params(*sem):
    return pltpu.CompilerParams(dimension_semantics=sem, vmem_limit_bytes=VMEM_LIMIT)


def _full(shape):
    nd = len(shape)
    return pl.BlockSpec(shape, lambda *_: (0,) * nd)


def _in_proj_kernel(x_ref, g_ref, wmain_ref, wside_ref, fb_ref, qg_ref, wuq_ref, kvg_ref, wukv_ref,
                    cos_ref, sin_ref,
                    swaq_ref, swak_ref, swav_ref, foxq_ref, foxk_ref, foxv_ref,
                    mlaq_ref, mlak_ref, mlav_ref, carry_ref):
    tm = x_ref.shape[0]
    h = _rms(x_ref[...], g_ref[...]).astype(BF16)

    main = jnp.dot(h, wmain_ref[...], preferred_element_type=F32)
    scale = HEAD_DIM ** -0.5
    swaq_ref[...] = (main[:, 0:512] * (scale * LOG2E)).astype(BF16)
    swak_ref[...] = main[:, 512:640].astype(BF16)
    swav_ref[...] = main[:, 640:768].astype(BF16)
    foxv_ref[...] = main[:, 1280:1536].astype(BF16)

    side = jnp.dot(h, wside_ref[...], preferred_element_type=F32)
    cos = cos_ref[...]
    sin = sin_ref[...]

    cqn = _rms(side[:, 0:256], qg_ref[...]).astype(BF16)
    qa = jnp.dot(cqn, wuq_ref[...], preferred_element_type=F32)
    for hh in range(MLA_HEADS):
        lo = hh * LANES
        a = qa[:, lo:lo + LANES]
        b = qa[:, 512 + lo:512 + lo + LANES]
        mlaq_ref[:, lo:lo + LANES] = ((a * cos + b * sin) * (MLA_QK_DIM ** -0.5 * LOG2E)).astype(BF16)

    ckvn = _rms(side[:, 256:384], kvg_ref[...]).astype(BF16)
    kv = jnp.dot(ckvn, wukv_ref[...], preferred_element_type=F32)
    krot = side[:, 384:512] * cos + side[:, 512:640] * sin
    for hh in range(MLA_HEADS):
        lo = hh * LANES
        mlak_ref[:, lo:lo + LANES] = (kv[:, lo:lo + LANES] + krot).astype(BF16)
    mlav_ref[...] = kv[:, 512:768].astype(BF16)

    lane = lax.broadcasted_iota(jnp.int32, (tm, LANES), 1)
    row = lax.broadcasted_iota(jnp.int32, (tm, LANES), 0)
    c = jnp.where(lane < FOX_HEADS, jax.nn.log_sigmoid(side[:, 640:768] + fb_ref[...]), 0.0)
    shift = 1
    while shift < tm:
        c = c + jnp.where(row >= shift, pltpu.roll(c, shift, axis=0), 0.0)
        shift *= 2

    @pl.when(pl.program_id(1) == 0)
    def _():
        carry_ref[...] = jnp.zeros_like(carry_ref)

    c = c + carry_ref[0:1, :]
    carry_ref[...] = jnp.broadcast_to(c[tm - 1:tm, :], carry_ref.shape)

    for hd in range(FOX_HEADS):
        pair, half = hd // 2, hd % 2
        o = 0 if half else HEAD_DIM
        f = jnp.broadcast_to(jnp.sum(jnp.where(lane == hd, c, 0.0), axis=1, keepdims=True), c.shape) * LOG2E
        hi = f.astype(BF16).astype(F32)
        mid = (f - hi).astype(BF16).astype(F32)
        low = (f - hi) - mid
        one_k = jnp.where(jnp.logical_and(lane >= o + 3, lane < o + 6), 1.0, 0.0)
        one_q = jnp.where(jnp.logical_and(lane >= o, lane < o + 3), 1.0, 0.0)
        extra_k = jnp.where(lane == o, -hi, jnp.where(lane == o + 1, -mid, jnp.where(lane == o + 2, -low, one_k)))
        extra_q = jnp.where(lane == o + 3, hi, jnp.where(lane == o + 4, mid, jnp.where(lane == o + 5, low, one_q)))
        data = _half_lanes(c.shape, half)
        slab = slice(hd * LANES, (hd + 1) * LANES)
        q_pair = main[:, 768 + pair * LANES:768 + (pair + 1) * LANES] * (scale * LOG2E)
        k_pair = main[:, 1024 + pair * LANES:1024 + (pair + 1) * LANES]
        foxq_ref[:, slab] = jnp.where(data, q_pair, extra_q).astype(BF16)
        foxk_ref[:, slab] = jnp.where(data, k_pair, extra_k).astype(BF16)


def _in_proj(x2, g, wmain, wside, fb, qg, wuq, kvg, wukv, cos_t, sin_t, batch, seq):
    T = x2.shape[0]
    tm = IN_PROJ_TILE
    ns = seq // tm
    row = lambda b, s: (b * ns + s, 0)
    seqrow = lambda b, s: (s, 0)
    out_shape = [
        jax.ShapeDtypeStruct((T, 512), BF16), jax.ShapeDtypeStruct((T, 128), BF16),
        jax.ShapeDtypeStruct((T, 128), BF16), jax.ShapeDtypeStruct((T, 512), BF16),
        jax.ShapeDtypeStruct((T, 512), BF16), jax.ShapeDtypeStruct((T, 256), BF16),
        jax.ShapeDtypeStruct((T, 512), BF16), jax.ShapeDtypeStruct((T, 512), BF16),
        jax.ShapeDtypeStruct((T, 256), BF16),
    ]
    out_specs = [
        pl.BlockSpec((tm, 512), row), pl.BlockSpec((tm, 128), row), pl.BlockSpec((tm, 128), row),
        pl.BlockSpec((tm, 512), row), pl.BlockSpec((tm, 512), row), pl.BlockSpec((tm, 256), row),
        pl.BlockSpec((tm, 512), row), pl.BlockSpec((tm, 512), row), pl.BlockSpec((tm, 256), row),
    ]
    in_specs = [
        pl.BlockSpec((tm, D_MODEL), row), _full(g.shape), _full(wmain.shape), _full(wside.shape),
        _full(fb.shape), _full(qg.shape), _full(wuq.shape), _full(kvg.shape), _full(wukv.shape),
        pl.BlockSpec((tm, LANES), seqrow), pl.BlockSpec((tm, LANES), seqrow),
    ]
    return pl.pallas_call(
        _in_proj_kernel,
        grid=(batch, ns),
        in_specs=in_specs,
        out_specs=out_specs,
        out_shape=out_shape,
        scratch_shapes=[pltpu.VMEM((8, LANES), F32)],
        compiler_params=_params("arbitrary", "arbitrary"),
        name="in_proj",
    )(x2, g, wmain, wside, fb, qg, wuq, kvg, wukv, cos_t, sin_t)


def _half_lanes(shape, hh):
    lane = lax.broadcasted_iota(jnp.int32, shape, len(shape) - 1)
    return (lane < HEAD_DIM) if hh == 0 else (lane >= HEAD_DIM)


def _ones_in_other_half(v2, hh):
    return jnp.where(_half_lanes(v2.shape, hh), v2, jnp.ones_like(v2))


def _swa_kernel(q_ref, kc_ref, kp_ref, vc_ref, vp_ref, bias_ref, sink_ref, o_ref, kbuf, vbuf, sbuf, ebuf, mbuf):
    first_tile = pl.program_id(1) == 0
    nwin = q_ref.shape[0] // WINDOW
    npair = SWA_Q_HEADS // 2
    kbuf[0:WINDOW, :] = kp_ref[...]
    kbuf[WINDOW:, :] = kc_ref[...]
    vbuf[0:WINDOW, :] = vp_ref[...]
    vbuf[WINDOW:, :] = vc_ref[...]
    kj = lax.broadcasted_iota(jnp.int32, (1, 2 * WINDOW), 1)
    no_prev = jnp.where(jnp.logical_and(kj < WINDOW, first_tile), NEG_INF, 0.0)
    low = _half_lanes((WINDOW, LANES), 0)
    units = [(w, hh) for w in range(nwin) for hh in range(2)]

    for i, (w, hh) in enumerate(units):
        rows = slice(w * WINDOW, (w + 1) * WINDOW)
        mine = _half_lanes((WINDOW, LANES), hh)
        q4 = [q_ref[rows, p * LANES:(p + 1) * LANES] for p in range(npair)]
        qs = jnp.concatenate([jnp.where(mine, q, jnp.zeros_like(q)) for q in q4], axis=0)
        kw = kbuf[w * WINDOW:(w + 2) * WINDOW, :]
        s = lax.dot_general(qs, kw, NT_DIMS, preferred_element_type=F32) + bias_ref[hh]
        sbuf[i] = s + no_prev if w == 0 else s

    for i, (w, hh) in enumerate(units):
        s0, s1 = sbuf[i, :, 0:LANES], sbuf[i, :, LANES:]
        sink = sink_ref[hh]
        mloc = jnp.max(jnp.maximum(s0, s1), axis=1, keepdims=True)
        m = jnp.maximum(jnp.broadcast_to(mloc, sink.shape), sink)
        mbuf[i] = m
        ebuf[i] = jnp.concatenate([jnp.exp2(s0 - m), jnp.exp2(s1 - m)], axis=1).astype(BF16)

    for w in range(nwin):
        rows = slice(w * WINDOW, (w + 1) * WINDOW)
        vw = vbuf[w * WINDOW:(w + 2) * WINDOW, :]
        outs = []
        for hh in range(2):
            i = 2 * w + hh
            pv = jnp.dot(ebuf[i], _ones_in_other_half(vw, hh), preferred_element_type=F32)
            den = pltpu.roll(pv, HEAD_DIM, axis=1) + jnp.exp2(sink_ref[hh] - mbuf[i])
            outs.append(pv / den)
        for p in range(npair):
            pr = slice(p * WINDOW, (p + 1) * WINDOW)
            o_ref[rows, p * LANES:(p + 1) * LANES] = jnp.where(low, outs[0][pr], outs[1][pr]).astype(o_ref.dtype)


def _swa(q, k, v, bias, sinks, batch, seq):
    T = q.shape[0]
    tm = ROW_TILE
    ns = seq // tm
    per = tm // WINDOW
    unit_rows = (SWA_Q_HEADS // SWA_KV_HEADS) * WINDOW
    row = lambda b, s: (b * ns + s, 0)
    prev = lambda b, s: (jnp.maximum((b * ns + s) * per - 1, 0), 0)
    return pl.pallas_call(
        _swa_kernel,
        grid=(batch, ns),
        in_specs=[
            pl.BlockSpec((tm, SWA_WIDTH), row),
            pl.BlockSpec((tm, LANES), row), pl.BlockSpec((WINDOW, LANES), prev),
            pl.BlockSpec((tm, LANES), row), pl.BlockSpec((WINDOW, LANES), prev),
            _full(bias.shape), _full(sinks.shape),
        ],
        out_specs=pl.BlockSpec((tm, SWA_WIDTH), row),
        out_shape=jax.ShapeDtypeStruct((T, SWA_WIDTH), BF16),
        scratch_shapes=[
            pltpu.VMEM((tm + WINDOW, LANES), BF16), pltpu.VMEM((tm + WINDOW, LANES), BF16),
            pltpu.VMEM((2 * per, unit_rows, 2 * WINDOW), F32),
            pltpu.VMEM((2 * per, unit_rows, 2 * WINDOW), BF16),
            pltpu.VMEM((2 * per, unit_rows, LANES), F32),
        ],
        compiler_params=_params("parallel", "arbitrary"),
        name="swa_attn",
    )(q, k, k, v, v, bias, sinks)


def _flash_init(m_sc, acc_sc):
    m_sc[...] = jnp.full(m_sc.shape, NEG_INF, F32)
    acc_sc[...] = jnp.zeros(acc_sc.shape, F32)


def _flash_update(s_tiles, v2, head, rows, m_sc, acc_sc):
    mloc = functools.reduce(jnp.maximum, s_tiles)
    mloc = jnp.broadcast_to(jnp.max(mloc, axis=1, keepdims=True), mloc.shape)
    m_prev = m_sc[head, rows]
    m_new = jnp.maximum(m_prev, mloc)
    m_sc[head, rows] = m_new
    alpha = jnp.exp2(m_prev - m_new)
    p = jnp.concatenate([jnp.exp2(s - m_new) for s in s_tiles], axis=1).astype(BF16)
    pv = jnp.dot(p, _ones_in_other_half(v2, head % 2), preferred_element_type=F32)
    acc_sc[head, rows] = alpha * acc_sc[head, rows] + pv


def _flash_finish(o_ref, acc_sc):
    low = _half_lanes(acc_sc.shape[1:], 0)
    for pair in range(acc_sc.shape[0] // 2):
        outs = [acc_sc[2 * pair + hh] / pltpu.roll(acc_sc[2 * pair + hh], HEAD_DIM, axis=1) for hh in range(2)]
        o_ref[:, pair * LANES:(pair + 1) * LANES] = jnp.where(low, outs[0], outs[1]).astype(o_ref.dtype)


def _lane_tiles(s):
    return [s[:, j * LANES:(j + 1) * LANES] for j in range(s.shape[1] // LANES)]


def _mask_from(tiles, first):
    r = lax.broadcasted_iota(jnp.int32, tiles[0].shape, 0)
    c = lax.broadcasted_iota(jnp.int32, tiles[0].shape, 1)
    return [x if j < first else jnp.where(c + (j - first) * LANES <= r, x, NEG_INF) for j, x in enumerate(tiles)]


def _flash_kernel(qi_ref, ki_ref, q_ref, k_ref, v_ref, o_ref, m_sc, acc_sc):
    t = pl.program_id(1)
    qi = qi_ref[t]
    ki = ki_ref[t]
    tq, tk = q_ref.shape[0], k_ref.shape[0]
    half = tq // 2
    nheads = q_ref.shape[1] // LANES

    @pl.when(ki == 0)
    def _():
        _flash_init(m_sc, acc_sc)

    def scores(head, rows, keys):
        slab = slice(head * LANES, (head + 1) * LANES)
        s = lax.dot_general(q_ref[rows, slab], k_ref[keys, slab], NT_DIMS, preferred_element_type=F32)
        return _lane_tiles(s)

    def step(diagonal):
        for head in range(nheads):
            pair = slice((head // 2) * LANES, (head // 2 + 1) * LANES)
            if not diagonal:
                _flash_update(scores(head, slice(0, tq), slice(0, tk)), v_ref[:, pair], head, slice(0, tq),
                              m_sc, acc_sc)
                continue
            top, bottom = slice(0, half), slice(half, tq)
            _flash_update(_mask_from(scores(head, top, top), 0), v_ref[top, pair], head, top, m_sc, acc_sc)
            _flash_update(_mask_from(scores(head, bottom, slice(0, tk)), half // LANES), v_ref[:, pair], head,
                          bottom, m_sc, acc_sc)

    @pl.when(ki < qi)
    def _():
        step(False)

    @pl.when(ki == qi)
    def _():
        step(True)
        _flash_finish(o_ref, acc_sc)


def _tri_tables(nq):
    qi = [q for q in range(nq) for _ in range(q + 1)]
    ki = [k for q in range(nq) for k in range(q + 1)]
    return jnp.asarray(qi, jnp.int32), jnp.asarray(ki, jnp.int32)


def _flash(q, k, v, batch, seq, name):
    T = q.shape[0]
    nheads = q.shape[1] // LANES
    tq = ATTN_TILE
    nq = seq // tq
    qi_tab, ki_tab = _tri_tables(nq)
    qrow = lambda b, t, qi, ki: (b * nq + qi[t], 0)
    krow = lambda b, t, qi, ki: (b * nq + ki[t], 0)
    grid_spec = pltpu.PrefetchScalarGridSpec(
        num_scalar_prefetch=2,
        grid=(batch, qi_tab.shape[0]),
        in_specs=[
            pl.BlockSpec((tq, q.shape[1]), qrow), pl.BlockSpec((tq, k.shape[1]), krow),
            pl.BlockSpec((tq, v.shape[1]), krow),
        ],
        out_specs=pl.BlockSpec((tq, v.shape[1]), qrow),
        scratch_shapes=[pltpu.VMEM((nheads, tq, LANES), F32),
                        pltpu.VMEM((nheads, tq, LANES), F32)],
    )
    return pl.pallas_call(
        _flash_kernel,
        grid_spec=grid_spec,
        out_shape=jax.ShapeDtypeStruct((T, v.shape[1]), BF16),
        compiler_params=_params("parallel", "arbitrary"),
        name=name,
    )(qi_tab, ki_tab, q, k, v)


def _mix_ffn_kernel(oa_ref, ob_ref, oc_ref, gg_ref, wout_ref, apg_ref, x_ref,
                    g_ref, wup_ref, cw_ref, cb_ref, wdown_ref, pg_ref, o_ref,
                    hbuf, ubuf, acc_ref, carry_ref):
    tm = x_ref.shape[0]

    gg = gg_ref[...]
    a = _rms(oa_ref[...].astype(F32), gg[:, 0:SWA_WIDTH]).astype(BF16)
    b = _rms(ob_ref[...].astype(F32), gg[:, SWA_WIDTH:SWA_WIDTH + FOX_WIDTH]).astype(BF16)
    c = _rms(oc_ref[...].astype(F32), gg[:, SWA_WIDTH + FOX_WIDTH:]).astype(BF16)
    y = jnp.dot(a, wout_ref[0:SWA_WIDTH, :], preferred_element_type=F32)
    y = y + jnp.dot(b, wout_ref[SWA_WIDTH:SWA_WIDTH + FOX_WIDTH, :], preferred_element_type=F32)
    y = y + jnp.dot(c, wout_ref[SWA_WIDTH + FOX_WIDTH:, :], preferred_element_type=F32)
    x1 = x_ref[...] + _rms(y, apg_ref[...])
    o_ref[...] = x1
    hbuf[...] = _rms(x1, g_ref[...]).astype(BF16)

    @pl.when(pl.program_id(1) == 0)
    def _():
        carry_ref[...] = jnp.zeros_like(carry_ref)

    row8 = lax.broadcasted_iota(jnp.int32, (8, 2 * FF_CHUNK), 0)

    def up_proj(c, slot):
        ubuf[slot] = jnp.dot(hbuf[...], wup_ref[c], preferred_element_type=F32)

    def conv_act_down(c, slot, first=False):
        u = ubuf[slot]
        prev = carry_ref[c]
        carry_ref[c] = u[tm - 8:tm, :]
        w = cw_ref[c]
        y = cb_ref[c] + w[2:3, :] * u
        for d in (1, 2):
            r = pltpu.roll(u, d, axis=0)
            head = jnp.where(row8 < d, pltpu.roll(prev, d, axis=0), r[0:8, :])
            r = jnp.concatenate([head, r[8:, :]], axis=0)
            y = y + w[2 - d:3 - d, :] * r
        gate, half_up = y[:, 0:FF_CHUNK], y[:, FF_CHUNK:]
        t = jnp.tanh(gate * (GELU_K0 + GELU_K1 * (gate * gate)))
        act = ((gate * half_up) * (1.0 + t)).astype(BF16)
        down = jnp.dot(act, wdown_ref[c], preferred_element_type=F32)
        if first:
            acc_ref[...] = down
        else:
            acc_ref[...] += down

    up_proj(0, 0)
    up_proj(1, 1)
    conv_act_down(0, 0, first=True)

    def two_chunks(j, _):
        c = 2 * j + 1
        up_proj(c + 1, 0)
        conv_act_down(c, 1)
        up_proj(c + 2, 1)
        conv_act_down(c + 1, 0)
        return 0

    lax.fori_loop(0, (N_FF_CHUNKS - 3) // 2, two_chunks, 0)
    up_proj(N_FF_CHUNKS - 1, 0)
    conv_act_down(N_FF_CHUNKS - 2, 1)
    conv_act_down(N_FF_CHUNKS - 1, 0)
    o_ref[...] += _rms(acc_ref[...], pg_ref[...])


def _mix_ffn(oa, ob, oc, gg, wout, apg, x2, g, wup, cw, cb, wdown, pg, batch, seq):
    T = x2.shape[0]
    tm = MIX_TILE
    ns = seq // tm
    row = lambda b, s: (b * ns + s, 0)
    return pl.pallas_call(
        _mix_ffn_kernel,
        grid=(batch, ns),
        in_specs=[
            pl.BlockSpec((tm, SWA_WIDTH), row), pl.BlockSpec((tm, FOX_WIDTH), row),
            pl.BlockSpec((tm, MLA_WIDTH), row), _full(gg.shape), _full(wout.shape), _full(apg.shape),
            pl.BlockSpec((tm, D_MODEL), row), _full(g.shape), _full(wup.shape), _full(cw.shape),
            _full(cb.shape), _full(wdown.shape), _full(pg.shape),
        ],
        out_specs=pl.BlockSpec((tm, D_MODEL), row),
        out_shape=jax.ShapeDtypeStruct((T, D_MODEL), F32),
        scratch_shapes=[
            pltpu.VMEM((tm, D_MODEL), BF16),
            pltpu.VMEM((2, tm, 2 * FF_CHUNK), F32),
            pltpu.VMEM((tm, D_MODEL), F32),
            pltpu.VMEM((N_FF_CHUNKS, 8, 2 * FF_CHUNK), F32),
        ],
        compiler_params=_params("parallel", "arbitrary"),
        name="mix_ffn",
    )(oa, ob, oc, gg, wout, apg, x2, g, wup, cw, cb, wdown, pg)


def _t5_causal_bucket(dist):
    max_exact = REL_BUCKETS // 2
    d = jnp.maximum(dist, 0)
    log_ratio = jnp.log(jnp.maximum(d, 1).astype(F32) / max_exact) / math.log(REL_MAX_DIST / max_exact)
    large = max_exact + (log_ratio * (REL_BUCKETS - max_exact)).astype(jnp.int32)
    large = jnp.minimum(large, REL_BUCKETS - 1)
    return jnp.where(d < max_exact, d, large)


def _swa_bias_table(rel_bias):
    qi = jnp.arange(WINDOW, dtype=jnp.int32)[:, None] + WINDOW
    kj = jnp.arange(2 * WINDOW, dtype=jnp.int32)[None, :]
    dist = qi - kj
    in_band = (dist >= 0) & (dist < WINDOW)
    hit = _t5_causal_bucket(dist)[None, :, :] == jnp.arange(REL_BUCKETS, dtype=jnp.int32)[:, None, None]
    bias = jnp.sum(jnp.where(hit[None], rel_bias.astype(F32).T[:, :, None, None], 0.0), axis=1)
    bias = jnp.where(in_band[None], bias, NEG_INF)
    bias = bias * LOG2E
    return bias.reshape(SWA_KV_HEADS, (SWA_Q_HEADS // SWA_KV_HEADS) * WINDOW, 2 * WINDOW)


def _rope_tables(seq):
    pos = jnp.arange(seq, dtype=F32)
    inv_freq = ROPE_THETA ** (-(jnp.arange(MLA_ROPE_DIM // 2, dtype=F32) * 2.0 / MLA_ROPE_DIM))
    ang = pos[:, None] * inv_freq[None, :]
    cos, sin = jnp.cos(ang), jnp.sin(ang)
    pad = jnp.zeros((seq, LANES - MLA_QK_DIM), F32)
    cos_t = jnp.concatenate([jnp.ones((seq, MLA_NOPE_DIM), F32), cos, cos, pad], axis=1)
    sin_t = jnp.concatenate([jnp.zeros((seq, MLA_NOPE_DIM), F32), -sin, sin, pad], axis=1)
    return cos_t, sin_t


def _swap_halves(w):
    half = w.shape[1] // 2
    return jnp.concatenate([w[:, half:], w[:, :half]], axis=1)


def _layer_params(l, w_in, forget_bias, w_uq, w_ukv, group_norm, w_out, w_up, conv_w, conv_b, w_down):
    wi = w_in[l]
    d = wi.shape[0]
    order = jnp.asarray(SWA_HEAD_ORDER)
    swa_q = wi[:, 0:SWA_WIDTH].reshape(d, SWA_Q_HEADS, HEAD_DIM)[:, order].reshape(d, SWA_WIDTH)
    wmain = jnp.concatenate([swa_q, wi[:, SWA_WIDTH:SWA_COLS + 3 * FOX_WIDTH]], axis=1).astype(BF16)

    f0 = SWA_COLS + 3 * FOX_WIDTH
    m0 = SWA_COLS + FOX_COLS
    z = lambda n: jnp.zeros((d, n), F32)
    w_kr = wi[:, m0 + MLA_Q_RANK + MLA_KV_RANK:]
    rope_pad = LANES - MLA_QK_DIM
    wside = jnp.concatenate([
        wi[:, m0:m0 + MLA_Q_RANK + MLA_KV_RANK],
        z(MLA_NOPE_DIM), w_kr, z(rope_pad),
        z(MLA_NOPE_DIM), _swap_halves(w_kr), z(rope_pad),
        wi[:, f0:f0 + FOX_HEADS], z(LANES - FOX_HEADS),
    ], axis=1).astype(BF16)
    fb = jnp.pad(forget_bias[l], (0, LANES - FOX_HEADS))[None, :]

    uq = w_uq[l].reshape(MLA_Q_RANK, MLA_HEADS, MLA_QK_DIM)
    zq = jnp.zeros((MLA_Q_RANK, MLA_HEADS, 1), F32)
    plain = jnp.concatenate([uq, jnp.tile(zq, (1, 1, rope_pad))], axis=2)
    rope = uq[:, :, MLA_NOPE_DIM:]
    swapped = jnp.concatenate([jnp.tile(zq, (1, 1, MLA_NOPE_DIM)), rope[:, :, MLA_ROPE_DIM // 2:],
                               rope[:, :, :MLA_ROPE_DIM // 2], jnp.tile(zq, (1, 1, rope_pad))], axis=2)
    wuq = jnp.concatenate([plain.reshape(MLA_Q_RANK, -1), swapped.reshape(MLA_Q_RANK, -1)], axis=1).astype(BF16)

    ukv = w_ukv[l].reshape(MLA_KV_RANK, MLA_HEADS, MLA_NOPE_DIM + MLA_V_DIM)
    k_slabs = jnp.concatenate([ukv[:, :, :MLA_NOPE_DIM],
                               jnp.zeros((MLA_KV_RANK, MLA_HEADS, LANES - MLA_NOPE_DIM), F32)], axis=2)
    wukv = jnp.concatenate([k_slabs.reshape(MLA_KV_RANK, -1),
                            ukv[:, :, MLA_NOPE_DIM:].reshape(MLA_KV_RANK, -1)], axis=1).astype(BF16)

    gg = group_norm[l]
    gg = jnp.concatenate([gg[:SWA_WIDTH].reshape(SWA_Q_HEADS, HEAD_DIM)[order].reshape(-1), gg[SWA_WIDTH:]])[None, :]
    wo = w_out[l]
    wo = jnp.concatenate([wo[:SWA_WIDTH].reshape(SWA_Q_HEADS, HEAD_DIM, -1)[order].reshape(SWA_WIDTH, -1),
                          wo[SWA_WIDTH:]], axis=0).astype(BF16)

    def gate_up(a):
        lead = a.shape[:-1]
        a = a.reshape(lead + (2, N_FF_CHUNKS, FF_CHUNK))
        a = jnp.moveaxis(a, -2, 0)
        return a.reshape((N_FF_CHUNKS,) + lead + (2 * FF_CHUNK,))

    wup = gate_up(w_up[l].astype(BF16))
    halve_up = jnp.concatenate([jnp.ones((D_FF,), F32), jnp.full((D_FF,), 0.5, F32)])
    cw = gate_up(conv_w[l] * halve_up)
    cb = gate_up((conv_b[l] * halve_up)[None, :])
    wdown = w_down[l].astype(BF16).reshape(N_FF_CHUNKS, FF_CHUNK, -1)
    return wmain, wside, fb, wuq, wukv, gg, wo, wup, cw, cb, wdown


def kernel(x, attn_pre_norm, w_in, forget_bias, swa_sinks, rel_bias, q_latent_norm, w_uq, kv_latent_norm, w_ukv,
           group_norm, w_out, attn_post_norm, ffn_pre_norm, w_up, conv_w, conv_b, w_down, ffn_post_norm):
    batch, seq, d = x.shape
    assert d == D_MODEL and all(seq % t == 0 for t in (ROW_TILE, ATTN_TILE, IN_PROJ_TILE, MIX_TILE))
    depth = w_in.shape[0]
    cos_t, sin_t = _rope_tables(seq)
    bias = _swa_bias_table(rel_bias)
    x2 = x.reshape(batch * seq, d)
    for l in range(depth):
        wmain, wside, fb, wuq, wukv, gg, wo, wup, cw, cb, wdown = _layer_params(
            l, w_in, forget_bias, w_uq, w_ukv, group_norm, w_out, w_up, conv_w, conv_b, w_down)
        sinks = jnp.repeat((swa_sinks[l].astype(F32) * LOG2E).reshape(SWA_KV_HEADS, -1), WINDOW, axis=1)
        sinks = jnp.broadcast_to(sinks[:, :, None], sinks.shape + (LANES,))
        (swa_q, swa_k, swa_v, fox_q, fox_k, fox_v, mla_q, mla_k, mla_v) = _in_proj(
            x2, attn_pre_norm[l][None, :], wmain, wside, fb, q_latent_norm[l][None, :], wuq,
            kv_latent_norm[l][None, :], wukv, cos_t, sin_t, batch, seq)
        out_a = _swa(swa_q, swa_k, swa_v, bias, sinks, batch, seq)
        out_b = _flash(fox_q, fox_k, fox_v, batch, seq, "fox_attn")
        out_c = _flash(mla_q, mla_k, mla_v, batch, seq, "mla_attn")
        x2 = _mix_ffn(out_a, out_b, out_c, gg, wo, attn_post_norm[l][None, :], x2, ffn_pre_norm[l][None, :],
                      wup, cw, cb, wdown, ffn_post_norm[l][None, :], batch, seq)
    return x2.reshape(batch, seq, d)
```

```python
import functools
import math

import jax
import jax.numpy as jnp
from jax import lax
from jax.experimental import pallas as pl
from jax.experimental.pallas import tpu as pltpu

D_MODEL = 1024
HEAD_DIM = 64
SWA_Q_HEADS = 8
SWA_KV_HEADS = 2
WINDOW = 128
FOX_HEADS = 4
MLA_HEADS = 4
MLA_Q_RANK = 256
MLA_KV_RANK = 128
MLA_NOPE_DIM = 64
MLA_ROPE_DIM = 32
MLA_V_DIM = 64
MLA_QK_DIM = MLA_NOPE_DIM + MLA_ROPE_DIM
ROPE_THETA = 10000.0
REL_BUCKETS = 32
REL_MAX_DIST = 128
D_FF = 2816
EPS = 1e-6
NEG_INF = -1e30

SWA_WIDTH = SWA_Q_HEADS * HEAD_DIM
FOX_WIDTH = FOX_HEADS * HEAD_DIM
MLA_WIDTH = MLA_HEADS * MLA_V_DIM
SWA_COLS = (SWA_Q_HEADS + 2 * SWA_KV_HEADS) * HEAD_DIM
FOX_COLS = 3 * FOX_WIDTH + FOX_HEADS

LANES = 128
ROW_TILE = 1024
IN_PROJ_TILE = 1024
MIX_TILE = 1024
ATTN_TILE = 1024
FF_CHUNK = 256
N_FF_CHUNKS = D_FF // FF_CHUNK
VMEM_LIMIT = 60 * 1024 * 1024

SWA_HEAD_ORDER = (0, 4, 1, 5, 2, 6, 3, 7)

F32 = jnp.float32
BF16 = jnp.bfloat16
NT_DIMS = (((1,), (1,)), ((), ()))
LOG2E = math.log2(math.e)
GELU_K0 = math.sqrt(2.0 / math.pi)
GELU_K1 = GELU_K0 * 0.044715


def _rms(x, g):
    ms = jnp.mean(x * x, axis=-1, keepdims=True)
    return x * lax.rsqrt(ms + EPS) * g


def _params(*sem):
    return pltpu.CompilerParams(dimension_semantics=sem, vmem_limit_bytes=VMEM_LIMIT)


def _full(shape):
    nd = len(shape)
    return pl.BlockSpec(shape, lambda *_: (0,) * nd)


def _in_proj_kernel(x_ref, g_ref, wmain_ref, wside_ref, fb_ref, qg_ref, wuq_ref, kvg_ref, wukv_ref,
                    cos_ref, sin_ref,
                    swaq_ref, swak_ref, swav_ref, foxq_ref, foxk_ref, foxv_ref,
                    mlaq_ref, mlak_ref, mlav_ref, carry_ref):
    tm = x_ref.shape[0]
    h = _rms(x_ref[...], g_ref[...]).astype(BF16)

    side = jnp.dot(h, wside_ref[...], preferred_element_type=F32)
    cos = cos_ref[...]
    sin = sin_ref[...]

    cqn = _rms(side[:, 0:256], qg_ref[...]).astype(BF16)
    qa = jnp.dot(cqn, wuq_ref[...], preferred_element_type=F32)
    for hh in range(MLA_HEADS):
        lo = hh * LANES
        a = qa[:, lo:lo + LANES]
        b = qa[:, 512 + lo:512 + lo + LANES]
        mlaq_ref[:, lo:lo + LANES] = ((a * cos + b * sin) * (MLA_QK_DIM ** -0.5 * LOG2E)).astype(BF16)

    ckvn = _rms(side[:, 256:384], kvg_ref[...]).astype(BF16)
    kv = jnp.dot(ckvn, wukv_ref[...], preferred_element_type=F32)
    krot = side[:, 384:512] * cos + side[:, 512:640] * sin
    for hh in range(MLA_HEADS):
        lo = hh * LANES
        mlak_ref[:, lo:lo + LANES] = (kv[:, lo:lo + LANES] + krot).astype(BF16)
    mlav_ref[...] = kv[:, 512:768].astype(BF16)

    lane = lax.broadcasted_iota(jnp.int32, (tm, LANES), 1)
    row = lax.broadcasted_iota(jnp.int32, (tm, LANES), 0)
    c = jnp.where(lane < FOX_HEADS, jax.nn.log_sigmoid(side[:, 640:768] + fb_ref[...]), 0.0)
    shift = 1
    while shift < tm:
        c = c + jnp.where(row >= shift, pltpu.roll(c, shift, axis=0), 0.0)
        shift *= 2

    @pl.when(pl.program_id(1) == 0)
    def _():
        carry_ref[...] = jnp.zeros_like(carry_ref)

    c = c + carry_ref[0:1, :]
    carry_ref[...] = jnp.broadcast_to(c[tm - 1:tm, :], carry_ref.shape)

    main = jnp.dot(h, wmain_ref[...], preferred_element_type=F32)
    scale = HEAD_DIM ** -0.5
    swaq_ref[...] = (main[:, 0:512] * (scale * LOG2E)).astype(BF16)
    swak_ref[...] = main[:, 512:640].astype(BF16)
    swav_ref[...] = main[:, 640:768].astype(BF16)
    foxv_ref[...] = main[:, 1280:1536].astype(BF16)

    for hd in range(FOX_HEADS):
        pair, half = hd // 2, hd % 2
        o = 0 if half else HEAD_DIM
        f = jnp.broadcast_to(jnp.sum(jnp.where(lane == hd, c, 0.0), axis=1, keepdims=True), c.shape) * LOG2E
        hi = f.astype(BF16).astype(F32)
        mid = (f - hi).astype(BF16).astype(F32)
        low = (f - hi) - mid
        one_k = jnp.where(jnp.logical_and(lane >= o + 3, lane < o + 6), 1.0, 0.0)
        one_q = jnp.where(jnp.logical_and(lane >= o, lane < o + 3), 1.0, 0.0)
        extra_k = jnp.where(lane == o, -hi, jnp.where(lane == o + 1, -mid, jnp.where(lane == o + 2, -low, one_k)))
        extra_q = jnp.where(lane == o + 3, hi, jnp.where(lane == o + 4, mid, jnp.where(lane == o + 5, low, one_q)))
        data = _half_lanes(c.shape, half)
        slab = slice(hd * LANES, (hd + 1) * LANES)
        q_pair = main[:, 768 + pair * LANES:768 + (pair + 1) * LANES] * (scale * LOG2E)
        k_pair = main[:, 1024 + pair * LANES:1024 + (pair + 1) * LANES]
        foxq_ref[:, slab] = jnp.where(data, q_pair, extra_q).astype(BF16)
        foxk_ref[:, slab] = jnp.where(data, k_pair, extra_k).astype(BF16)


def _in_proj(x2, g, wmain, wside, fb, qg, wuq, kvg, wukv, cos_t, sin_t, batch, seq):
    T = x2.shape[0]
    tm = IN_PROJ_TILE
    ns = seq // tm
    row = lambda b, s: (b * ns + s, 0)
    seqrow = lambda b, s: (s, 0)
    out_shape = [
        jax.ShapeDtypeStruct((T, 512), BF16), jax.ShapeDtypeStruct((T, 128), BF16),
        jax.ShapeDtypeStruct((T, 128), BF16), jax.ShapeDtypeStruct((T, 512), BF16),
        jax.ShapeDtypeStruct((T, 512), BF16), jax.ShapeDtypeStruct((T, 256), BF16),
        jax.ShapeDtypeStruct((T, 512), BF16), jax.ShapeDtypeStruct((T, 512), BF16),
        jax.ShapeDtypeStruct((T, 256), BF16),
    ]
    out_specs = [
        pl.BlockSpec((tm, 512), row), pl.BlockSpec((tm, 128), row), pl.BlockSpec((tm, 128), row),
        pl.BlockSpec((tm, 512), row), pl.BlockSpec((tm, 512), row), pl.BlockSpec((tm, 256), row),
        pl.BlockSpec((tm, 512), row), pl.BlockSpec((tm, 512), row), pl.BlockSpec((tm, 256), row),
    ]
    in_specs = [
        pl.BlockSpec((tm, D_MODEL), row), _full(g.shape), _full(wmain.shape), _full(wside.shape),
        _full(fb.shape), _full(qg.shape), _full(wuq.shape), _full(kvg.shape), _full(wukv.shape),
        pl.BlockSpec((tm, LANES), seqrow), pl.BlockSpec((tm, LANES), seqrow),
    ]
    return pl.pallas_call(
        _in_proj_kernel,
        grid=(batch, ns),
        in_specs=in_specs,
        out_specs=out_specs,
        out_shape=out_shape,
        scratch_shapes=[pltpu.VMEM((8, LANES), F32)],
        compiler_params=_params("arbitrary", "arbitrary"),
        name="in_proj",
    )(x2, g, wmain, wside, fb, qg, wuq, kvg, wukv, cos_t, sin_t)


def _half_lanes(shape, hh):
    lane = lax.broadcasted_iota(jnp.int32, shape, len(shape) - 1)
    return (lane < HEAD_DIM) if hh == 0 else (lane >= HEAD_DIM)


def _ones_in_other_half(v2, hh):
    return jnp.where(_half_lanes(v2.shape, hh), v2, jnp.ones_like(v2))


def _swa_kernel(q_ref, kc_ref, kp_ref, vc_ref, vp_ref, bias_ref, sink_ref, o_ref, kbuf, vbuf, sbuf, ebuf, mbuf):
    first_tile = pl.program_id(1) == 0
    nwin = q_ref.shape[0] // WINDOW
    npair = SWA_Q_HEADS // 2
    kbuf[0:WINDOW, :] = kp_ref[...]
    kbuf[WINDOW:, :] = kc_ref[...]
    vbuf[0:WINDOW, :] = vp_ref[...]
    vbuf[WINDOW:, :] = vc_ref[...]
    kj = lax.broadcasted_iota(jnp.int32, (1, 2 * WINDOW), 1)
    no_prev = jnp.where(jnp.logical_and(kj < WINDOW, first_tile), NEG_INF, 0.0)
    low = _half_lanes((WINDOW, LANES), 0)
    units = [(w, hh) for w in range(nwin) for hh in range(2)]

    for i, (w, hh) in enumerate(units):
        rows = slice(w * WINDOW, (w + 1) * WINDOW)
        mine = _half_lanes((WINDOW, LANES), hh)
        q4 = [q_ref[rows, p * LANES:(p + 1) * LANES] for p in range(npair)]
        qs = jnp.concatenate([jnp.where(mine, q, jnp.zeros_like(q)) for q in q4], axis=0)
        kw = kbuf[w * WINDOW:(w + 2) * WINDOW, :]
        s = lax.dot_general(qs, kw, NT_DIMS, preferred_element_type=F32) + bias_ref[hh]
        sbuf[i] = s + no_prev if w == 0 else s

    for i, (w, hh) in enumerate(units):
        s0, s1 = sbuf[i, :, 0:LANES], sbuf[i, :, LANES:]
        sink = sink_ref[hh]
        mloc = jnp.max(jnp.maximum(s0, s1), axis=1, keepdims=True)
        m = jnp.maximum(jnp.broadcast_to(mloc, sink.shape), sink)
        mbuf[i] = m
        ebuf[i] = jnp.concatenate([jnp.exp2(s0 - m), jnp.exp2(s1 - m)], axis=1).astype(BF16)

    for w in range(nwin):
        rows = slice(w * WINDOW, (w + 1) * WINDOW)
        vw = vbuf[w * WINDOW:(w + 2) * WINDOW, :]
        outs = []
        for hh in range(2):
            i = 2 * w + hh
            pv = jnp.dot(ebuf[i], _ones_in_other_half(vw, hh), preferred_element_type=F32)
            den = pltpu.roll(pv, HEAD_DIM, axis=1) + jnp.exp2(sink_ref[hh] - mbuf[i])
            outs.append(pv / den)
        for p in range(npair):
            pr = slice(p * WINDOW, (p + 1) * WINDOW)
            o_ref[rows, p * LANES:(p + 1) * LANES] = jnp.where(low, outs[0][pr], outs[1][pr]).astype(o_ref.dtype)


def _swa(q, k, v, bias, sinks, batch, seq):
    T = q.shape[0]
    tm = ROW_TILE
    ns = seq // tm
    per = tm // WINDOW
    unit_rows = (SWA_Q_HEADS // SWA_KV_HEADS) * WINDOW
    row = lambda b, s: (b * ns + s, 0)
    prev = lambda b, s: (jnp.maximum((b * ns + s) * per - 1, 0), 0)
    return pl.pallas_call(
        _swa_kernel,
        grid=(batch, ns),
        in_specs=[
            pl.BlockSpec((tm, SWA_WIDTH), row),
            pl.BlockSpec((tm, LANES), row), pl.BlockSpec((WINDOW, LANES), prev),
            pl.BlockSpec((tm, LANES), row), pl.BlockSpec((WINDOW, LANES), prev),
            _full(bias.shape), _full(sinks.shape),
        ],
        out_specs=pl.BlockSpec((tm, SWA_WIDTH), row),
        out_shape=jax.ShapeDtypeStruct((T, SWA_WIDTH), BF16),
        scratch_shapes=[
            pltpu.VMEM((tm + WINDOW, LANES), BF16), pltpu.VMEM((tm + WINDOW, LANES), BF16),
            pltpu.VMEM((2 * per, unit_rows, 2 * WINDOW), F32),
            pltpu.VMEM((2 * per, unit_rows, 2 * WINDOW), BF16),
            pltpu.VMEM((2 * per, unit_rows, LANES), F32),
        ],
        compiler_params=_params("parallel", "arbitrary"),
        name="swa_attn",
    )(q, k, k, v, v, bias, sinks)


def _flash_init(m_sc, acc_sc):
    m_sc[...] = jnp.full(m_sc.shape, NEG_INF, F32)
    acc_sc[...] = jnp.zeros(acc_sc.shape, F32)


def _flash_update(s_tiles, v2, head, rows, m_sc, acc_sc):
    mloc = functools.reduce(jnp.maximum, s_tiles)
    mloc = jnp.broadcast_to(jnp.max(mloc, axis=1, keepdims=True), mloc.shape)
    m_prev = m_sc[head, rows]
    m_new = jnp.maximum(m_prev, mloc)
    m_sc[head, rows] = m_new
    alpha = jnp.exp2(m_prev - m_new)
    p = jnp.concatenate([jnp.exp2(s - m_new) for s in s_tiles], axis=1).astype(BF16)
    pv = jnp.dot(p, _ones_in_other_half(v2, head % 2), preferred_element_type=F32)
    acc_sc[head, rows] = alpha * acc_sc[head, rows] + pv


def _flash_finish(o_ref, acc_sc):
    low = _half_lanes(acc_sc.shape[1:], 0)
    for pair in range(acc_sc.shape[0] // 2):
        outs = [acc_sc[2 * pair + hh] / pltpu.roll(acc_sc[2 * pair + hh], HEAD_DIM, axis=1) for hh in range(2)]
        o_ref[:, pair * LANES:(pair + 1) * LANES] = jnp.where(low, outs[0], outs[1]).astype(o_ref.dtype)


def _lane_tiles(s):
    return [s[:, j * LANES:(j + 1) * LANES] for j in range(s.shape[1] // LANES)]


def _mask_from(tiles, first):
    r = lax.broadcasted_iota(jnp.int32, tiles[0].shape, 0)
    c = lax.broadcasted_iota(jnp.int32, tiles[0].shape, 1)
    return [x if j < first else jnp.where(c + (j - first) * LANES <= r, x, NEG_INF) for j, x in enumerate(tiles)]


def _flash_kernel(qi_ref, ki_ref, q_ref, k_ref, v_ref, o_ref, m_sc, acc_sc):
    t = pl.program_id(1)
    qi = qi_ref[t]
    ki = ki_ref[t]
    tq, tk = q_ref.shape[0], k_ref.shape[0]
    half = tq // 2
    nheads = q_ref.shape[1] // LANES

    @pl.when(ki == 0)
    def _():
        _flash_init(m_sc, acc_sc)

    def scores(head, rows, keys):
        slab = slice(head * LANES, (head + 1) * LANES)
        s = lax.dot_general(q_ref[rows, slab], k_ref[keys, slab], NT_DIMS, preferred_element_type=F32)
        return _lane_tiles(s)

    def step(diagonal):
        for head in range(nheads):
            pair = slice((head // 2) * LANES, (head // 2 + 1) * LANES)
            if not diagonal:
                _flash_update(scores(head, slice(0, tq), slice(0, tk)), v_ref[:, pair], head, slice(0, tq),
                              m_sc, acc_sc)
                continue
            top, bottom = slice(0, half), slice(half, tq)
            _flash_update(_mask_from(scores(head, top, top), 0), v_ref[top, pair], head, top, m_sc, acc_sc)
            _flash_update(_mask_from(scores(head, bottom, slice(0, tk)), half // LANES), v_ref[:, pair], head,
                          bottom, m_sc, acc_sc)

    @pl.when(ki < qi)
    def _():
        step(False)

    @pl.when(ki == qi)
    def _():
        step(True)
        _flash_finish(o_ref, acc_sc)


def _tri_tables(nq):
    qi = [q for q in range(nq) for _ in range(q + 1)]
    ki = [k for q in range(nq) for k in range(q + 1)]
    return jnp.asarray(qi, jnp.int32), jnp.asarray(ki, jnp.int32)


def _flash(q, k, v, batch, seq, name):
    T = q.shape[0]
    nheads = q.shape[1] // LANES
    tq = ATTN_TILE
    nq = seq // tq
    qi_tab, ki_tab = _tri_tables(nq)
    qrow = lambda b, t, qi, ki: (b * nq + qi[t], 0)
    krow = lambda b, t, qi, ki: (b * nq + ki[t], 0)
    grid_spec = pltpu.PrefetchScalarGridSpec(
        num_scalar_prefetch=2,
        grid=(batch, qi_tab.shape[0]),
        in_specs=[
            pl.BlockSpec((tq, q.shape[1]), qrow), pl.BlockSpec((tq, k.shape[1]), krow),
            pl.BlockSpec((tq, v.shape[1]), krow),
        ],
        out_specs=pl.BlockSpec((tq, v.shape[1]), qrow),
        scratch_shapes=[pltpu.VMEM((nheads, tq, LANES), F32),
                        pltpu.VMEM((nheads, tq, LANES), F32)],
    )
    return pl.pallas_call(
        _flash_kernel,
        grid_spec=grid_spec,
        out_shape=jax.ShapeDtypeStruct((T, v.shape[1]), BF16),
        compiler_params=_params("parallel", "arbitrary"),
        name=name,
    )(qi_tab, ki_tab, q, k, v)


def _mix_ffn_kernel(oa_ref, ob_ref, oc_ref, gg_ref, wout_ref, apg_ref, x_ref,
                    g_ref, wup_ref, cw_ref, cb_ref, wdown_ref, pg_ref, o_ref,
                    hbuf, ubuf, acc_ref, carry_ref):
    tm = x_ref.shape[0]

    gg = gg_ref[...]
    a = _rms(oa_ref[...].astype(F32), gg[:, 0:SWA_WIDTH]).astype(BF16)
    b = _rms(ob_ref[...].astype(F32), gg[:, SWA_WIDTH:SWA_WIDTH + FOX_WIDTH]).astype(BF16)
    c = _rms(oc_ref[...].astype(F32), gg[:, SWA_WIDTH + FOX_WIDTH:]).astype(BF16)
    y = jnp.dot(a, wout_ref[0:SWA_WIDTH, :], preferred_element_type=F32)
    y = y + jnp.dot(b, wout_ref[SWA_WIDTH:SWA_WIDTH + FOX_WIDTH, :], preferred_element_type=F32)
    y = y + jnp.dot(c, wout_ref[SWA_WIDTH + FOX_WIDTH:, :], preferred_element_type=F32)
    x1 = x_ref[...] + _rms(y, apg_ref[...])
    o_ref[...] = x1
    hbuf[...] = _rms(x1, g_ref[...]).astype(BF16)

    @pl.when(pl.program_id(1) == 0)
    def _():
        carry_ref[...] = jnp.zeros_like(carry_ref)

    row8 = lax.broadcasted_iota(jnp.int32, (8, 2 * FF_CHUNK), 0)

    def up_proj(c, slot):
        ubuf[slot] = jnp.dot(hbuf[...], wup_ref[c], preferred_element_type=F32)

    def conv_act_down(c, slot, first=False):
        u = ubuf[slot]
        prev = carry_ref[c]
        carry_ref[c] = u[tm - 8:tm, :]
        w = cw_ref[c]
        y = cb_ref[c] + w[2:3, :] * u
        for d in (1, 2):
            r = pltpu.roll(u, d, axis=0)
            head = jnp.where(row8 < d, pltpu.roll(prev, d, axis=0), r[0:8, :])
            r = jnp.concatenate([head, r[8:, :]], axis=0)
            y = y + w[2 - d:3 - d, :] * r
        gate, half_up = y[:, 0:FF_CHUNK], y[:, FF_CHUNK:]
        t = jnp.tanh(gate * (GELU_K0 + GELU_K1 * (gate * gate)))
        act = ((gate * half_up) * (1.0 + t)).astype(BF16)
        down = jnp.dot(act, wdown_ref[c], preferred_element_type=F32)
        if first:
            acc_ref[...] = down
        else:
            acc_ref[...] += down

    up_proj(0, 0)
    up_proj(1, 1)
    conv_act_down(0, 0, first=True)

    def two_chunks(j, _):
        c = 2 * j + 1
        up_proj(c + 1, 0)
        conv_act_down(c, 1)
        up_proj(c + 2, 1)
        conv_act_down(c + 1, 0)
        return 0

    lax.fori_loop(0, (N_FF_CHUNKS - 3) // 2, two_chunks, 0)
    up_proj(N_FF_CHUNKS - 1, 0)
    conv_act_down(N_FF_CHUNKS - 2, 1)
    conv_act_down(N_FF_CHUNKS - 1, 0)
    o_ref[...] += _rms(acc_ref[...], pg_ref[...])


def _mix_ffn(oa, ob, oc, gg, wout, apg, x2, g, wup, cw, cb, wdown, pg, batch, seq):
    T = x2.shape[0]
    tm = MIX_TILE
    ns = seq // tm
    row = lambda b, s: (b * ns + s, 0)
    return pl.pallas_call(
        _mix_ffn_kernel,
        grid=(batch, ns),
        in_specs=[
            pl.BlockSpec((tm, SWA_WIDTH), row), pl.BlockSpec((tm, FOX_WIDTH), row),
            pl.BlockSpec((tm, MLA_WIDTH), row), _full(gg.shape), _full(wout.shape), _full(apg.shape),
            pl.BlockSpec((tm, D_MODEL), row), _full(g.shape), _full(wup.shape), _full(cw.shape),
            _full(cb.shape), _full(wdown.shape), _full(pg.shape),
        ],
        out_specs=pl.BlockSpec((tm, D_MODEL), row),
        out_shape=jax.ShapeDtypeStruct((T, D_MODEL), F32),
        scratch_shapes=[
            pltpu.VMEM((tm, D_MODEL), BF16),
            pltpu.VMEM((2, tm, 2 * FF_CHUNK), F32),
            pltpu.VMEM((tm, D_MODEL), F32),
            pltpu.VMEM((N_FF_CHUNKS, 8, 2 * FF_CHUNK), F32),
        ],
        compiler_params=_params("parallel", "arbitrary"),
        name="mix_ffn",
    )(oa, ob, oc, gg, wout, apg, x2, g, wup, cw, cb, wdown, pg)


def _t5_causal_bucket(dist):
    max_exact = REL_BUCKETS // 2
    d = jnp.maximum(dist, 0)
    log_ratio = jnp.log(jnp.maximum(d, 1).astype(F32) / max_exact) / math.log(REL_MAX_DIST / max_exact)
    large = max_exact + (log_ratio * (REL_BUCKETS - max_exact)).astype(jnp.int32)
    large = jnp.minimum(large, REL_BUCKETS - 1)
    return jnp.where(d < max_exact, d, large)


def _swa_bias_table(rel_bias):
    qi = jnp.arange(WINDOW, dtype=jnp.int32)[:, None] + WINDOW
    kj = jnp.arange(2 * WINDOW, dtype=jnp.int32)[None, :]
    dist = qi - kj
    in_band = (dist >= 0) & (dist < WINDOW)
    hit = _t5_causal_bucket(dist)[None, :, :] == jnp.arange(REL_BUCKETS, dtype=jnp.int32)[:, None, None]
    bias = jnp.sum(jnp.where(hit[None], rel_bias.astype(F32).T[:, :, None, None], 0.0), axis=1)
    bias = jnp.where(in_band[None], bias, NEG_INF)
    bias = bias * LOG2E
    return bias.reshape(SWA_KV_HEADS, (SWA_Q_HEADS // SWA_KV_HEADS) * WINDOW, 2 * WINDOW)


def _rope_tables(seq):
    pos = jnp.arange(seq, dtype=F32)
    inv_freq = ROPE_THETA ** (-(jnp.arange(MLA_ROPE_DIM // 2, dtype=F32) * 2.0 / MLA_ROPE_DIM))
    ang = pos[:, None] * inv_freq[None, :]
    cos, sin = jnp.cos(ang), jnp.sin(ang)
    pad = jnp.zeros((seq, LANES - MLA_QK_DIM), F32)
    cos_t = jnp.concatenate([jnp.ones((seq, MLA_NOPE_DIM), F32), cos, cos, pad], axis=1)
    sin_t = jnp.concatenate([jnp.zeros((seq, MLA_NOPE_DIM), F32), -sin, sin, pad], axis=1)
    return cos_t, sin_t


def _swap_halves(w):
    half = w.shape[1] // 2
    return jnp.concatenate([w[:, half:], w[:, :half]], axis=1)


def _layer_params(l, w_in, forget_bias, w_uq, w_ukv, group_norm, w_out, w_up, conv_w, conv_b, w_down):
    wi = w_in[l]
    d = wi.shape[0]
    order = jnp.asarray(SWA_HEAD_ORDER)
    swa_q = wi[:, 0:SWA_WIDTH].reshape(d, SWA_Q_HEADS, HEAD_DIM)[:, order].reshape(d, SWA_WIDTH)
    wmain = jnp.concatenate([swa_q, wi[:, SWA_WIDTH:SWA_COLS + 3 * FOX_WIDTH]], axis=1).astype(BF16)

    f0 = SWA_COLS + 3 * FOX_WIDTH
    m0 = SWA_COLS + FOX_COLS
    z = lambda n: jnp.zeros((d, n), F32)
    w_kr = wi[:, m0 + MLA_Q_RANK + MLA_KV_RANK:]
    rope_pad = LANES - MLA_QK_DIM
    wside = jnp.concatenate([
        wi[:, m0:m0 + MLA_Q_RANK + MLA_KV_RANK],
        z(MLA_NOPE_DIM), w_kr, z(rope_pad),
        z(MLA_NOPE_DIM), _swap_halves(w_kr), z(rope_pad),
        wi[:, f0:f0 + FOX_HEADS], z(LANES - FOX_HEADS),
    ], axis=1).astype(BF16)
    fb = jnp.pad(forget_bias[l], (0, LANES - FOX_HEADS))[None, :]

    uq = w_uq[l].reshape(MLA_Q_RANK, MLA_HEADS, MLA_QK_DIM)
    zq = jnp.zeros((MLA_Q_RANK, MLA_HEADS, 1), F32)
    plain = jnp.concatenate([uq, jnp.tile(zq, (1, 1, rope_pad))], axis=2)
    rope = uq[:, :, MLA_NOPE_DIM:]
    swapped = jnp.concatenate([jnp.tile(zq, (1, 1, MLA_NOPE_DIM)), rope[:, :, MLA_ROPE_DIM // 2:],
                               rope[:, :, :MLA_ROPE_DIM // 2], jnp.tile(zq, (1, 1, rope_pad))], axis=2)
    wuq = jnp.concatenate([plain.reshape(MLA_Q_RANK, -1), swapped.reshape(MLA_Q_RANK, -1)], axis=1).astype(BF16)

    ukv = w_ukv[l].reshape(MLA_KV_RANK, MLA_HEADS, MLA_NOPE_DIM + MLA_V_DIM)
    k_slabs = jnp.concatenate([ukv[:, :, :MLA_NOPE_DIM],
                               jnp.zeros((MLA_KV_RANK, MLA_HEADS, LANES - MLA_NOPE_DIM), F32)], axis=2)
    wukv = jnp.concatenate([k_slabs.reshape(MLA_KV_RANK, -1),
                            ukv[:, :, MLA_NOPE_DIM:].reshape(MLA_KV_RANK, -1)], axis=1).astype(BF16)

    gg = group_norm[l]
    gg = jnp.concatenate([gg[:SWA_WIDTH].reshape(SWA_Q_HEADS, HEAD_DIM)[order].reshape(-1), gg[SWA_WIDTH:]])[None, :]
    wo = w_out[l]
    wo = jnp.concatenate([wo[:SWA_WIDTH].reshape(SWA_Q_HEADS, HEAD_DIM, -1)[order].reshape(SWA_WIDTH, -1),
                          wo[SWA_WIDTH:]], axis=0).astype(BF16)

    def gate_up(a):
        lead = a.shape[:-1]
        a = a.reshape(lead + (2, N_FF_CHUNKS, FF_CHUNK))
        a = jnp.moveaxis(a, -2, 0)
        return a.reshape((N_FF_CHUNKS,) + lead + (2 * FF_CHUNK,))

    wup = gate_up(w_up[l].astype(BF16))
    halve_up = jnp.concatenate([jnp.ones((D_FF,), F32), jnp.full((D_FF,), 0.5, F32)])
    cw = gate_up(conv_w[l] * halve_up)
    cb = gate_up((conv_b[l] * halve_up)[None, :])
    wdown = w_down[l].astype(BF16).reshape(N_FF_CHUNKS, FF_CHUNK, -1)
    return wmain, wside, fb, wuq, wukv, gg, wo, wup, cw, cb, wdown


def kernel(x, attn_pre_norm, w_in, forget_bias, swa_sinks, rel_bias, q_latent_norm, w_uq, kv_latent_norm, w_ukv,
           group_norm, w_out, attn_post_norm, ffn_pre_norm, w_up, conv_w, conv_b, w_down, ffn_post_norm):
    batch, seq, d = x.shape
    assert d == D_MODEL and all(seq % t == 0 for t in (ROW_TILE, ATTN_TILE, IN_PROJ_TILE, MIX_TILE))
    depth = w_in.shape[0]
    cos_t, sin_t = _rope_tables(seq)
    bias = _swa_bias_table(rel_bias)
    x2 = x.reshape(batch * seq, d)
    for l in range(depth):
        wmain, wside, fb, wuq, wukv, gg, wo, wup, cw, cb, wdown = _layer_params(
            l, w_in, forget_bias, w_uq, w_ukv, group_norm, w_out, w_up, conv_w, conv_b, w_down)
        sinks = jnp.repeat((swa_sinks[l].astype(F32) * LOG2E).reshape(SWA_KV_HEADS, -1), WINDOW, axis=1)
        sinks = jnp.broadcast_to(sinks[:, :, None], sinks.shape + (LANES,))
        (swa_q, swa_k, swa_v, fox_q, fox_k, fox_v, mla_q, mla_k, mla_v) = _in_proj(
            x2, attn_pre_norm[l][None, :], wmain, wside, fb, q_latent_norm[l][None, :], wuq,
            kv_latent_norm[l][None, :], wukv, cos_t, sin_t, batch, seq)
        out_a = _swa(swa_q, swa_k, swa_v, bias, sinks, batch, seq)
        out_b = _flash(fox_q, fox_k, fox_v, batch, seq, "fox_attn")
        out_c = _flash(mla_q, mla_k, mla_v, batch, seq, "mla_attn")
        x2 = _mix_ffn(out_a, out_b, out_c, gg, wo, attn_post_norm[l][None, :], x2, ffn_pre_norm[l][None, :],
                      wup, cw, cb, wdown, ffn_post_norm[l][None, :], batch, seq)
    return x2.reshape(batch, seq, d)
```

```python
import functools
import math

import jax
import jax.numpy as jnp
from jax import lax
from jax.experimental import pallas as pl
from jax.experimental.pallas import tpu as pltpu

D_MODEL = 1024
HEAD_DIM = 64
SWA_Q_HEADS = 8
SWA_KV_HEADS = 2
WINDOW = 128
FOX_HEADS = 4
MLA_HEADS = 4
MLA_Q_RANK = 256
MLA_KV_RANK = 128
MLA_NOPE_DIM = 64
MLA_ROPE_DIM = 32
MLA_V_DIM = 64
MLA_QK_DIM = MLA_NOPE_DIM + MLA_ROPE_DIM
ROPE_THETA = 10000.0
REL_BUCKETS = 32
REL_MAX_DIST = 128
D_FF = 2816
EPS = 1e-6
NEG_INF = -1e30

SWA_WIDTH = SWA_Q_HEADS * HEAD_DIM
FOX_WIDTH = FOX_HEADS * HEAD_DIM
MLA_WIDTH = MLA_HEADS * MLA_V_DIM
SWA_COLS = (SWA_Q_HEADS + 2 * SWA_KV_HEADS) * HEAD_DIM
FOX_COLS = 3 * FOX_WIDTH + FOX_HEADS

LANES = 128
ROW_TILE = 1024
IN_PROJ_TILE = 1024
MIX_TILE = 1024
ATTN_TILE = 1024
FF_CHUNK = 256
N_FF_CHUNKS = D_FF // FF_CHUNK
VMEM_LIMIT = 60 * 1024 * 1024

SWA_HEAD_ORDER = (0, 4, 1, 5, 2, 6, 3, 7)

F32 = jnp.float32
BF16 = jnp.bfloat16
NT_DIMS = (((1,), (1,)), ((), ()))
LOG2E = math.log2(math.e)
GELU_K0 = math.sqrt(2.0 / math.pi)
GELU_K1 = GELU_K0 * 0.044715


def _rms(x, g):
    ms = jnp.mean(x * x, axis=-1, keepdims=True)
    return x * lax.rsqrt(ms + EPS) * g


def _params(*sem):
    return pltpu.CompilerParams(dimension_semantics=sem, vmem_limit_bytes=VMEM_LIMIT)


def _full(shape):
    nd = len(shape)
    return pl.BlockSpec(shape, lambda *_: (0,) * nd)


def _in_proj_kernel(x_ref, g_ref, wmain_ref, wside_ref, fb_ref, qg_ref, wuq_ref, kvg_ref, wukv_ref,
                    cos_ref, sin_ref,
                    swaq_ref, swak_ref, swav_ref, foxq_ref, foxk_ref, foxv_ref,
                    mlaq_ref, mlak_ref, mlav_ref, carry_ref):
    tm = x_ref.shape[0]
    h = _rms(x_ref[...], g_ref[...]).astype(BF16)

    side = jnp.dot(h, wside_ref[...], preferred_element_type=F32)
    cos = cos_ref[...]
    sin = sin_ref[...]

    cqn = _rms(side[:, 0:256], qg_ref[...]).astype(BF16)
    qa = jnp.dot(cqn, wuq_ref[...], preferred_element_type=F32)
    for hh in range(MLA_HEADS):
        lo = hh * LANES
        a = qa[:, lo:lo + LANES]
        b = qa[:, 512 + lo:512 + lo + LANES]
        mlaq_ref[:, lo:lo + LANES] = ((a * cos + b * sin) * (MLA_QK_DIM ** -0.5 * LOG2E)).astype(BF16)

    ckvn = _rms(side[:, 256:384], kvg_ref[...]).astype(BF16)
    kv = jnp.dot(ckvn, wukv_ref[...], preferred_element_type=F32)
    krot = side[:, 384:512] * cos + side[:, 512:640] * sin
    for hh in range(MLA_HEADS):
        lo = hh * LANES
        mlak_ref[:, lo:lo + LANES] = (kv[:, lo:lo + LANES] + krot).astype(BF16)
    mlav_ref[...] = kv[:, 512:768].astype(BF16)

    lane = lax.broadcasted_iota(jnp.int32, (tm, LANES), 1)
    row = lax.broadcasted_iota(jnp.int32, (tm, LANES), 0)
    c = jnp.where(lane < FOX_HEADS, jax.nn.log_sigmoid(side[:, 640:768] + fb_ref[...]), 0.0)
    shift = 1
    while shift < tm:
        c = c + jnp.where(row >= shift, pltpu.roll(c, shift, axis=0), 0.0)
        shift *= 2

    @pl.when(pl.program_id(1) == 0)
    def _():
        carry_ref[...] = jnp.zeros_like(carry_ref)

    c = c + carry_ref[0:1, :]
    carry_ref[...] = jnp.broadcast_to(c[tm - 1:tm, :], carry_ref.shape)

    main = jnp.dot(h, wmain_ref[...], preferred_element_type=F32)
    scale = HEAD_DIM ** -0.5
    swaq_ref[...] = (main[:, 0:512] * (scale * LOG2E)).astype(BF16)
    swak_ref[...] = main[:, 512:640].astype(BF16)
    swav_ref[...] = main[:, 640:768].astype(BF16)
    foxv_ref[...] = main[:, 1280:1536].astype(BF16)

    for hd in range(FOX_HEADS):
        pair, half = hd // 2, hd % 2
        o = 0 if half else HEAD_DIM
        f = jnp.broadcast_to(jnp.sum(jnp.where(lane == hd, c, 0.0), axis=1, keepdims=True), c.shape) * LOG2E
        hi = f.astype(BF16).astype(F32)
        mid = (f - hi).astype(BF16).astype(F32)
        low = (f - hi) - mid
        one_k = jnp.where(jnp.logical_and(lane >= o + 3, lane < o + 6), 1.0, 0.0)
        one_q = jnp.where(jnp.logical_and(lane >= o, lane < o + 3), 1.0, 0.0)
        extra_k = jnp.where(lane == o, -hi, jnp.where(lane == o + 1, -mid, jnp.where(lane == o + 2, -low, one_k)))
        extra_q = jnp.where(lane == o + 3, hi, jnp.where(lane == o + 4, mid, jnp.where(lane == o + 5, low, one_q)))
        data = _half_lanes(c.shape, half)
        slab = slice(hd * LANES, (hd + 1) * LANES)
        q_pair = main[:, 768 + pair * LANES:768 + (pair + 1) * LANES] * (scale * LOG2E)
        k_pair = main[:, 1024 + pair * LANES:1024 + (pair + 1) * LANES]
        foxq_ref[:, slab] = jnp.where(data, q_pair, extra_q).astype(BF16)
        foxk_ref[:, slab] = jnp.where(data, k_pair, extra_k).astype(BF16)


def _in_proj(x2, g, wmain, wside, fb, qg, wuq, kvg, wukv, cos_t, sin_t, batch, seq):
    T = x2.shape[0]
    tm = IN_PROJ_TILE
    ns = seq // tm
    row = lambda b, s: (b * ns + s, 0)
    seqrow = lambda b, s: (s, 0)
    out_shape = [
        jax.ShapeDtypeStruct((T, 512), BF16), jax.ShapeDtypeStruct((T, 128), BF16),
        jax.ShapeDtypeStruct((T, 128), BF16), jax.ShapeDtypeStruct((T, 512), BF16),
        jax.ShapeDtypeStruct((T, 512), BF16), jax.ShapeDtypeStruct((T, 256), BF16),
        jax.ShapeDtypeStruct((T, 512), BF16), jax.ShapeDtypeStruct((T, 512), BF16),
        jax.ShapeDtypeStruct((T, 256), BF16),
    ]
    out_specs = [
        pl.BlockSpec((tm, 512), row), pl.BlockSpec((tm, 128), row), pl.BlockSpec((tm, 128), row),
        pl.BlockSpec((tm, 512), row), pl.BlockSpec((tm, 512), row), pl.BlockSpec((tm, 256), row),
        pl.BlockSpec((tm, 512), row), pl.BlockSpec((tm, 512), row), pl.BlockSpec((tm, 256), row),
    ]
    in_specs = [
        pl.BlockSpec((tm, D_MODEL), row), _full(g.shape), _full(wmain.shape), _full(wside.shape),
        _full(fb.shape), _full(qg.shape), _full(wuq.shape), _full(kvg.shape), _full(wukv.shape),
        pl.BlockSpec((tm, LANES), seqrow), pl.BlockSpec((tm, LANES), seqrow),
    ]
    return pl.pallas_call(
        _in_proj_kernel,
        grid=(batch, ns),
        in_specs=in_specs,
        out_specs=out_specs,
        out_shape=out_shape,
        scratch_shapes=[pltpu.VMEM((8, LANES), F32)],
        compiler_params=_params("arbitrary", "arbitrary"),
        name="in_proj",
    )(x2, g, wmain, wside, fb, qg, wuq, kvg, wukv, cos_t, sin_t)


def _half_lanes(shape, hh):
    lane = lax.broadcasted_iota(jnp.int32, shape, len(shape) - 1)
    return (lane < HEAD_DIM) if hh == 0 else (lane >= HEAD_DIM)


def _ones_in_other_half(v2, hh):
    return jnp.where(_half_lanes(v2.shape, hh), v2, jnp.ones_like(v2))


def _swa_kernel(q_ref, kc_ref, kp_ref, vc_ref, vp_ref, bias_ref, sink_ref, o_ref, kbuf, vbuf, sbuf, ebuf, mbuf):
    first_tile = pl.program_id(1) == 0
    nwin = q_ref.shape[0] // WINDOW
    npair = SWA_Q_HEADS // 2
    kbuf[0:WINDOW, :] = kp_ref[...]
    kbuf[WINDOW:, :] = kc_ref[...]
    vbuf[0:WINDOW, :] = vp_ref[...]
    vbuf[WINDOW:, :] = vc_ref[...]
    kj = lax.broadcasted_iota(jnp.int32, (1, 2 * WINDOW), 1)
    no_prev = jnp.where(jnp.logical_and(kj < WINDOW, first_tile), NEG_INF, 0.0)
    low = _half_lanes((WINDOW, LANES), 0)
    units = [(w, hh) for w in range(nwin) for hh in range(2)]

    for i, (w, hh) in enumerate(units):
        rows = slice(w * WINDOW, (w + 1) * WINDOW)
        mine = _half_lanes((WINDOW, LANES), hh)
        q4 = [q_ref[rows, p * LANES:(p + 1) * LANES] for p in range(npair)]
        qs = jnp.concatenate([jnp.where(mine, q, jnp.zeros_like(q)) for q in q4], axis=0)
        kw = kbuf[w * WINDOW:(w + 2) * WINDOW, :]
        s = lax.dot_general(qs, kw, NT_DIMS, preferred_element_type=F32) + bias_ref[hh]
        sbuf[i] = s + no_prev if w == 0 else s

    for i, (w, hh) in enumerate(units):
        s0, s1 = sbuf[i, :, 0:LANES], sbuf[i, :, LANES:]
        sink = sink_ref[hh]
        mloc = jnp.max(jnp.maximum(s0, s1), axis=1, keepdims=True)
        m = jnp.maximum(jnp.broadcast_to(mloc, sink.shape), sink)
        mbuf[i] = m
        ebuf[i] = jnp.concatenate([jnp.exp2(s0 - m), jnp.exp2(s1 - m)], axis=1).astype(BF16)

    for w in range(nwin):
        rows = slice(w * WINDOW, (w + 1) * WINDOW)
        vw = vbuf[w * WINDOW:(w + 2) * WINDOW, :]
        outs = []
        for hh in range(2):
            i = 2 * w + hh
            pv = jnp.dot(ebuf[i], _ones_in_other_half(vw, hh), preferred_element_type=F32)
            den = pltpu.roll(pv, HEAD_DIM, axis=1) + jnp.exp2(sink_ref[hh] - mbuf[i])
            outs.append(pv / den)
        for p in range(npair):
            pr = slice(p * WINDOW, (p + 1) * WINDOW)
            o_ref[rows, p * LANES:(p + 1) * LANES] = jnp.where(low, outs[0][pr], outs[1][pr]).astype(o_ref.dtype)


def _swa(q, k, v, bias, sinks, batch, seq):
    T = q.shape[0]
    tm = ROW_TILE
    ns = seq // tm
    per = tm // WINDOW
    unit_rows = (SWA_Q_HEADS // SWA_KV_HEADS) * WINDOW
    row = lambda b, s: (b * ns + s, 0)
    prev = lambda b, s: (jnp.maximum((b * ns + s) * per - 1, 0), 0)
    return pl.pallas_call(
        _swa_kernel,
        grid=(batch, ns),
        in_specs=[
            pl.BlockSpec((tm, SWA_WIDTH), row),
            pl.BlockSpec((tm, LANES), row), pl.BlockSpec((WINDOW, LANES), prev),
            pl.BlockSpec((tm, LANES), row), pl.BlockSpec((WINDOW, LANES), prev),
            _full(bias.shape), _full(sinks.shape),
        ],
        out_specs=pl.BlockSpec((tm, SWA_WIDTH), row),
        out_shape=jax.ShapeDtypeStruct((T, SWA_WIDTH), BF16),
        scratch_shapes=[
            pltpu.VMEM((tm + WINDOW, LANES), BF16), pltpu.VMEM((tm + WINDOW, LANES), BF16),
            pltpu.VMEM((2 * per, unit_rows, 2 * WINDOW), F32),
            pltpu.VMEM((2 * per, unit_rows, 2 * WINDOW), BF16),
            pltpu.VMEM((2 * per, unit_rows, LANES), F32),
        ],
        compiler_params=_params("parallel", "arbitrary"),
        name="swa_attn",
    )(q, k, k, v, v, bias, sinks)


def _flash_init(m_sc, acc_sc):
    m_sc[...] = jnp.full(m_sc.shape, NEG_INF, F32)
    acc_sc[...] = jnp.zeros(acc_sc.shape, F32)


def _flash_update(s_tiles, v2, head, rows, m_sc, acc_sc):
    mloc = functools.reduce(jnp.maximum, s_tiles)
    mloc = jnp.broadcast_to(jnp.max(mloc, axis=1, keepdims=True), mloc.shape)
    m_prev = m_sc[head, rows]
    m_new = jnp.maximum(m_prev, mloc)
    m_sc[head, rows] = m_new
    alpha = jnp.exp2(m_prev - m_new)
    p = jnp.concatenate([jnp.exp2(s - m_new) for s in s_tiles], axis=1).astype(BF16)
    pv = jnp.dot(p, _ones_in_other_half(v2, head % 2), preferred_element_type=F32)
    acc_sc[head, rows] = alpha * acc_sc[head, rows] + pv


def _flash_finish(o_ref, acc_sc):
    low = _half_lanes(acc_sc.shape[1:], 0)
    for pair in range(acc_sc.shape[0] // 2):
        outs = [acc_sc[2 * pair + hh] / pltpu.roll(acc_sc[2 * pair + hh], HEAD_DIM, axis=1) for hh in range(2)]
        o_ref[:, pair * LANES:(pair + 1) * LANES] = jnp.where(low, outs[0], outs[1]).astype(o_ref.dtype)


def _lane_tiles(s):
    return [s[:, j * LANES:(j + 1) * LANES] for j in range(s.shape[1] // LANES)]


def _mask_from(tiles, first):
    r = lax.broadcasted_iota(jnp.int32, tiles[0].shape, 0)
    c = lax.broadcasted_iota(jnp.int32, tiles[0].shape, 1)
    return [x if j < first else jnp.where(c + (j - first) * LANES <= r, x, NEG_INF) for j, x in enumerate(tiles)]


def _flash_kernel(qi_ref, ki_ref, q_ref, k_ref, v_ref, o_ref, m_sc, acc_sc):
    t = pl.program_id(1)
    qi = qi_ref[t]
    ki = ki_ref[t]
    tq, tk = q_ref.shape[0], k_ref.shape[0]
    half = tq // 2
    nheads = q_ref.shape[1] // LANES

    @pl.when(ki == 0)
    def _():
        _flash_init(m_sc, acc_sc)

    def scores(head, rows, keys):
        slab = slice(head * LANES, (head + 1) * LANES)
        s = lax.dot_general(q_ref[rows, slab], k_ref[keys, slab], NT_DIMS, preferred_element_type=F32)
        return _lane_tiles(s)

    def step(diagonal):
        for head in range(nheads):
            pair = slice((head // 2) * LANES, (head // 2 + 1) * LANES)
            if not diagonal:
                _flash_update(scores(head, slice(0, tq), slice(0, tk)), v_ref[:, pair], head, slice(0, tq),
                              m_sc, acc_sc)
                continue
            top, bottom = slice(0, half), slice(half, tq)
            _flash_update(_mask_from(scores(head, top, top), 0), v_ref[top, pair], head, top, m_sc, acc_sc)
            _flash_update(_mask_from(scores(head, bottom, slice(0, tk)), half // LANES), v_ref[:, pair], head,
                          bottom, m_sc, acc_sc)

    @pl.when(ki < qi)
    def _():
        step(False)

    @pl.when(ki == qi)
    def _():
        step(True)
        _flash_finish(o_ref, acc_sc)


def _tri_tables(nq):
    qi = [q for q in range(nq) for _ in range(q + 1)]
    ki = [k for q in range(nq) for k in range(q + 1)]
    return jnp.asarray(qi, jnp.int32), jnp.asarray(ki, jnp.int32)


def _flash(q, k, v, batch, seq, name):
    T = q.shape[0]
    nheads = q.shape[1] // LANES
    tq = ATTN_TILE
    nq = seq // tq
    qi_tab, ki_tab = _tri_tables(nq)
    qrow = lambda b, t, qi, ki: (b * nq + qi[t], 0)
    krow = lambda b, t, qi, ki: (b * nq + ki[t], 0)
    grid_spec = pltpu.PrefetchScalarGridSpec(
        num_scalar_prefetch=2,
        grid=(batch, qi_tab.shape[0]),
        in_specs=[
            pl.BlockSpec((tq, q.shape[1]), qrow), pl.BlockSpec((tq, k.shape[1]), krow),
            pl.BlockSpec((tq, v.shape[1]), krow),
        ],
        out_specs=pl.BlockSpec((tq, v.shape[1]), qrow),
        scratch_shapes=[pltpu.VMEM((nheads, tq, LANES), F32),
                        pltpu.VMEM((nheads, tq, LANES), F32)],
    )
    return pl.pallas_call(
        _flash_kernel,
        grid_spec=grid_spec,
        out_shape=jax.ShapeDtypeStruct((T, v.shape[1]), BF16),
        compiler_params=_params("parallel", "arbitrary"),
        name=name,
    )(qi_tab, ki_tab, q, k, v)


def _mix_ffn_kernel(oa_ref, ob_ref, oc_ref, gg_ref, wout_ref, apg_ref, x_ref,
                    g_ref, wup_ref, cw_ref, cb_ref, wdown_ref, pg_ref, o_ref,
                    hbuf, ubuf, acc_ref, carry_ref):
    tm = x_ref.shape[0]

    gg = gg_ref[...]
    for rows in (slice(0, tm // 2), slice(tm // 2, tm)):
        a = _rms(oa_ref[rows, :].astype(F32), gg[:, 0:SWA_WIDTH]).astype(BF16)
        b = _rms(ob_ref[rows, :].astype(F32), gg[:, SWA_WIDTH:SWA_WIDTH + FOX_WIDTH]).astype(BF16)
        c = _rms(oc_ref[rows, :].astype(F32), gg[:, SWA_WIDTH + FOX_WIDTH:]).astype(BF16)
        y = jnp.dot(a, wout_ref[0:SWA_WIDTH, :], preferred_element_type=F32)
        y = y + jnp.dot(b, wout_ref[SWA_WIDTH:SWA_WIDTH + FOX_WIDTH, :], preferred_element_type=F32)
        y = y + jnp.dot(c, wout_ref[SWA_WIDTH + FOX_WIDTH:, :], preferred_element_type=F32)
        x1 = x_ref[rows, :] + _rms(y, apg_ref[...])
        o_ref[rows, :] = x1
        hbuf[rows, :] = _rms(x1, g_ref[...]).astype(BF16)

    @pl.when(pl.program_id(1) == 0)
    def _():
        carry_ref[...] = jnp.zeros_like(carry_ref)

    row8 = lax.broadcasted_iota(jnp.int32, (8, 2 * FF_CHUNK), 0)

    def gate_up(ref, c):
        starts = (c * FF_CHUNK, D_FF + c * FF_CHUNK)
        if not isinstance(c, int):
            starts = [pl.multiple_of(s, FF_CHUNK) for s in starts]
        return jnp.concatenate([ref[:, pl.ds(s, FF_CHUNK)] for s in starts], axis=1)

    def up_proj(c, slot):
        ubuf[slot] = jnp.dot(hbuf[...], gate_up(wup_ref, c), preferred_element_type=F32)

    def conv_act_down(c, slot, first=False):
        u = ubuf[slot]
        prev = carry_ref[c]
        carry_ref[c] = u[tm - 8:tm, :]
        w = gate_up(cw_ref, c)
        y = gate_up(cb_ref, c) + w[2:3, :] * u
        for d in (1, 2):
            r = pltpu.roll(u, d, axis=0)
            head = jnp.where(row8 < d, pltpu.roll(prev, d, axis=0), r[0:8, :])
            r = jnp.concatenate([head, r[8:, :]], axis=0)
            y = y + w[2 - d:3 - d, :] * r
        gate, half_up = y[:, 0:FF_CHUNK], y[:, FF_CHUNK:]
        t = jnp.tanh(gate * (GELU_K0 + GELU_K1 * (gate * gate)))
        act = ((gate * half_up) * (1.0 + t)).astype(BF16)
        down = jnp.dot(act, wdown_ref[c], preferred_element_type=F32)
        if first:
            acc_ref[...] = down
        else:
            acc_ref[...] += down

    up_proj(0, 0)
    up_proj(1, 1)
    conv_act_down(0, 0, first=True)

    def two_chunks(j, _):
        c = 2 * j + 1
        up_proj(c + 1, 0)
        conv_act_down(c, 1)
        up_proj(c + 2, 1)
        conv_act_down(c + 1, 0)
        return 0

    lax.fori_loop(0, (N_FF_CHUNKS - 3) // 2, two_chunks, 0)
    up_proj(N_FF_CHUNKS - 1, 0)
    conv_act_down(N_FF_CHUNKS - 2, 1)
    conv_act_down(N_FF_CHUNKS - 1, 0)
    o_ref[...] += _rms(acc_ref[...], pg_ref[...])


def _mix_ffn(oa, ob, oc, gg, wout, apg, x2, g, wup, cw, cb, wdown, pg, batch, seq):
    T = x2.shape[0]
    tm = MIX_TILE
    ns = seq // tm
    row = lambda b, s: (b * ns + s, 0)
    return pl.pallas_call(
        _mix_ffn_kernel,
        grid=(batch, ns),
        in_specs=[
            pl.BlockSpec((tm, SWA_WIDTH), row), pl.BlockSpec((tm, FOX_WIDTH), row),
            pl.BlockSpec((tm, MLA_WIDTH), row), _full(gg.shape), _full(wout.shape), _full(apg.shape),
            pl.BlockSpec((tm, D_MODEL), row), _full(g.shape), _full(wup.shape), _full(cw.shape),
            _full(cb.shape), _full(wdown.shape), _full(pg.shape),
        ],
        out_specs=pl.BlockSpec((tm, D_MODEL), row),
        out_shape=jax.ShapeDtypeStruct((T, D_MODEL), F32),
        scratch_shapes=[
            pltpu.VMEM((tm, D_MODEL), BF16),
            pltpu.VMEM((2, tm, 2 * FF_CHUNK), F32),
            pltpu.VMEM((tm, D_MODEL), F32),
            pltpu.VMEM((N_FF_CHUNKS, 8, 2 * FF_CHUNK), F32),
        ],
        compiler_params=_params("parallel", "arbitrary"),
        name="mix_ffn",
    )(oa, ob, oc, gg, wout, apg, x2, g, wup, cw, cb, wdown, pg)


def _t5_causal_bucket(dist):
    max_exact = REL_BUCKETS // 2
    d = jnp.maximum(dist, 0)
    log_ratio = jnp.log(jnp.maximum(d, 1).astype(F32) / max_exact) / math.log(REL_MAX_DIST / max_exact)
    large = max_exact + (log_ratio * (REL_BUCKETS - max_exact)).astype(jnp.int32)
    large = jnp.minimum(large, REL_BUCKETS - 1)
    return jnp.where(d < max_exact, d, large)


def _swa_bias_table(rel_bias):
    qi = jnp.arange(WINDOW, dtype=jnp.int32)[:, None] + WINDOW
    kj = jnp.arange(2 * WINDOW, dtype=jnp.int32)[None, :]
    dist = qi - kj
    in_band = (dist >= 0) & (dist < WINDOW)
    hit = _t5_causal_bucket(dist)[None, :, :] == jnp.arange(REL_BUCKETS, dtype=jnp.int32)[:, None, None]
    bias = jnp.sum(jnp.where(hit[None], rel_bias.astype(F32).T[:, :, None, None], 0.0), axis=1)
    bias = jnp.where(in_band[None], bias, NEG_INF)
    bias = bias * LOG2E
    return bias.reshape(SWA_KV_HEADS, (SWA_Q_HEADS // SWA_KV_HEADS) * WINDOW, 2 * WINDOW)


def _rope_tables(seq):
    pos = jnp.arange(seq, dtype=F32)
    inv_freq = ROPE_THETA ** (-(jnp.arange(MLA_ROPE_DIM // 2, dtype=F32) * 2.0 / MLA_ROPE_DIM))
    ang = pos[:, None] * inv_freq[None, :]
    cos, sin = jnp.cos(ang), jnp.sin(ang)
    pad = jnp.zeros((seq, LANES - MLA_QK_DIM), F32)
    cos_t = jnp.concatenate([jnp.ones((seq, MLA_NOPE_DIM), F32), cos, cos, pad], axis=1)
    sin_t = jnp.concatenate([jnp.zeros((seq, MLA_NOPE_DIM), F32), -sin, sin, pad], axis=1)
    return cos_t, sin_t


def _swap_halves(w):
    half = w.shape[1] // 2
    return jnp.concatenate([w[:, half:], w[:, :half]], axis=1)


def _layer_params(l, w_in, forget_bias, w_uq, w_ukv, group_norm, w_out, w_up, conv_w, conv_b, w_down):
    wi = w_in[l]
    d = wi.shape[0]
    order = jnp.asarray(SWA_HEAD_ORDER)
    swa_q = wi[:, 0:SWA_WIDTH].reshape(d, SWA_Q_HEADS, HEAD_DIM)[:, order].reshape(d, SWA_WIDTH)
    wmain = jnp.concatenate([swa_q, wi[:, SWA_WIDTH:SWA_COLS + 3 * FOX_WIDTH]], axis=1).astype(BF16)

    f0 = SWA_COLS + 3 * FOX_WIDTH
    m0 = SWA_COLS + FOX_COLS
    z = lambda n: jnp.zeros((d, n), F32)
    w_kr = wi[:, m0 + MLA_Q_RANK + MLA_KV_RANK:]
    rope_pad = LANES - MLA_QK_DIM
    wside = jnp.concatenate([
        wi[:, m0:m0 + MLA_Q_RANK + MLA_KV_RANK],
        z(MLA_NOPE_DIM), w_kr, z(rope_pad),
        z(MLA_NOPE_DIM), _swap_halves(w_kr), z(rope_pad),
        wi[:, f0:f0 + FOX_HEADS], z(LANES - FOX_HEADS),
    ], axis=1).astype(BF16)
    fb = jnp.pad(forget_bias[l], (0, LANES - FOX_HEADS))[None, :]

    uq = w_uq[l].reshape(MLA_Q_RANK, MLA_HEADS, MLA_QK_DIM)
    zq = jnp.zeros((MLA_Q_RANK, MLA_HEADS, 1), F32)
    plain = jnp.concatenate([uq, jnp.tile(zq, (1, 1, rope_pad))], axis=2)
    rope = uq[:, :, MLA_NOPE_DIM:]
    swapped = jnp.concatenate([jnp.tile(zq, (1, 1, MLA_NOPE_DIM)), rope[:, :, MLA_ROPE_DIM // 2:],
                               rope[:, :, :MLA_ROPE_DIM // 2], jnp.tile(zq, (1, 1, rope_pad))], axis=2)
    wuq = jnp.concatenate([plain.reshape(MLA_Q_RANK, -1), swapped.reshape(MLA_Q_RANK, -1)], axis=1).astype(BF16)

    ukv = w_ukv[l].reshape(MLA_KV_RANK, MLA_HEADS, MLA_NOPE_DIM + MLA_V_DIM)
    k_slabs = jnp.concatenate([ukv[:, :, :MLA_NOPE_DIM],
                               jnp.zeros((MLA_KV_RANK, MLA_HEADS, LANES - MLA_NOPE_DIM), F32)], axis=2)
    wukv = jnp.concatenate([k_slabs.reshape(MLA_KV_RANK, -1),
                            ukv[:, :, MLA_NOPE_DIM:].reshape(MLA_KV_RANK, -1)], axis=1).astype(BF16)

    gg = group_norm[l]
    gg = jnp.concatenate([gg[:SWA_WIDTH].reshape(SWA_Q_HEADS, HEAD_DIM)[order].reshape(-1), gg[SWA_WIDTH:]])[None, :]
    wo = w_out[l]
    wo = jnp.concatenate([wo[:SWA_WIDTH].reshape(SWA_Q_HEADS, HEAD_DIM, -1)[order].reshape(SWA_WIDTH, -1),
                          wo[SWA_WIDTH:]], axis=0).astype(BF16)

    wup = w_up[l].astype(BF16)
    halve_up = jnp.concatenate([jnp.ones((D_FF,), F32), jnp.full((D_FF,), 0.5, F32)])
    cw = conv_w[l] * halve_up
    cb = (conv_b[l] * halve_up)[None, :]
    wdown = w_down[l].astype(BF16).reshape(N_FF_CHUNKS, FF_CHUNK, -1)
    return wmain, wside, fb, wuq, wukv, gg, wo, wup, cw, cb, wdown


def kernel(x, attn_pre_norm, w_in, forget_bias, swa_sinks, rel_bias, q_latent_norm, w_uq, kv_latent_norm, w_ukv,
           group_norm, w_out, attn_post_norm, ffn_pre_norm, w_up, conv_w, conv_b, w_down, ffn_post_norm):
    batch, seq, d = x.shape
    assert d == D_MODEL and all(seq % t == 0 for t in (ROW_TILE, ATTN_TILE, IN_PROJ_TILE, MIX_TILE))
    depth = w_in.shape[0]
    cos_t, sin_t = _rope_tables(seq)
    bias = _swa_bias_table(rel_bias)
    x2 = x.reshape(batch * seq, d)
    for l in range(depth):
        wmain, wside, fb, wuq, wukv, gg, wo, wup, cw, cb, wdown = _layer_params(
            l, w_in, forget_bias, w_uq, w_ukv, group_norm, w_out, w_up, conv_w, conv_b, w_down)
        sinks = jnp.repeat((swa_sinks[l].astype(F32) * LOG2E).reshape(SWA_KV_HEADS, -1), WINDOW, axis=1)
        sinks = jnp.broadcast_to(sinks[:, :, None], sinks.shape + (LANES,))
        (swa_q, swa_k, swa_v, fox_q, fox_k, fox_v, mla_q, mla_k, mla_v) = _in_proj(
            x2, attn_pre_norm[l][None, :], wmain, wside, fb, q_latent_norm[l][None, :], wuq,
            kv_latent_norm[l][None, :], wukv, cos_t, sin_t, batch, seq)
        out_a = _swa(swa_q, swa_k, swa_v, bias, sinks, batch, seq)
        out_b = _flash(fox_q, fox_k, fox_v, batch, seq, "fox_attn")
        out_c = _flash(mla_q, mla_k, mla_v, batch, seq, "mla_attn")
        x2 = _mix_ffn(out_a, out_b, out_c, gg, wo, attn_post_norm[l][None, :], x2, ffn_pre_norm[l][None, :],
                      wup, cw, cb, wdown, ffn_post_norm[l][None, :], batch, seq)
    return x2.reshape(batch, seq, d)
```

```python
import functools
import math

import jax
import jax.numpy as jnp
from jax import lax
from jax.experimental import pallas as pl
from jax.experimental.pallas import tpu as pltpu

D_MODEL = 1024
HEAD_DIM = 64
SWA_Q_HEADS = 8
SWA_KV_HEADS = 2
WINDOW = 128
FOX_HEADS = 4
MLA_HEADS = 4
MLA_Q_RANK = 256
MLA_KV_RANK = 128
MLA_NOPE_DIM = 64
MLA_ROPE_DIM = 32
MLA_V_DIM = 64
MLA_QK_DIM = MLA_NOPE_DIM + MLA_ROPE_DIM
ROPE_THETA = 10000.0
REL_BUCKETS = 32
REL_MAX_DIST = 128
D_FF = 2816
EPS = 1e-6
NEG_INF = -1e30

SWA_WIDTH = SWA_Q_HEADS * HEAD_DIM
FOX_WIDTH = FOX_HEADS * HEAD_DIM
MLA_WIDTH = MLA_HEADS * MLA_V_DIM
SWA_COLS = (SWA_Q_HEADS + 2 * SWA_KV_HEADS) * HEAD_DIM
FOX_COLS = 3 * FOX_WIDTH + FOX_HEADS

LANES = 128
ROW_TILE = 1024
IN_PROJ_TILE = 1024
MIX_TILE = 1024
ATTN_TILE = 1024
FF_CHUNK = 256
N_FF_CHUNKS = D_FF // FF_CHUNK
VMEM_LIMIT = 60 * 1024 * 1024

SWA_HEAD_ORDER = (0, 4, 1, 5, 2, 6, 3, 7)

F32 = jnp.float32
BF16 = jnp.bfloat16
NT_DIMS = (((1,), (1,)), ((), ()))
LOG2E = math.log2(math.e)
GELU_K0 = math.sqrt(2.0 / math.pi)
GELU_K1 = GELU_K0 * 0.044715


def _rms(x, g):
    ms = jnp.mean(x * x, axis=-1, keepdims=True)
    return x * lax.rsqrt(ms + EPS) * g


def _params(*sem):
    return pltpu.CompilerParams(dimension_semantics=sem, vmem_limit_bytes=VMEM_LIMIT)


def _full(shape):
    nd = len(shape)
    return pl.BlockSpec(shape, lambda *_: (0,) * nd)


def _in_proj_kernel(x_ref, g_ref, wmain_ref, wside_ref, fb_ref, qg_ref, wuq_ref, kvg_ref, wukv_ref,
                    cos_ref, sin_ref,
                    swaq_ref, swak_ref, swav_ref, foxq_ref, foxk_ref, foxv_ref,
                    mlaq_ref, mlak_ref, mlav_ref, carry_ref):
    tm = x_ref.shape[0]
    @pl.when(pl.program_id(1) == 0)
    def _():
        carry_ref[...] = jnp.zeros_like(carry_ref)

    h = _rms(x_ref[...], g_ref[...]).astype(BF16)

    side = jnp.dot(h, wside_ref[...], preferred_element_type=F32)
    cos = cos_ref[...]
    sin = sin_ref[...]

    cqn = _rms(side[:, 0:256], qg_ref[...]).astype(BF16)
    qa = jnp.dot(cqn, wuq_ref[...], preferred_element_type=F32)
    for hh in range(MLA_HEADS):
        lo = hh * LANES
        a = qa[:, lo:lo + LANES]
        b = qa[:, 512 + lo:512 + lo + LANES]
        mlaq_ref[:, lo:lo + LANES] = ((a * cos + b * sin) * (MLA_QK_DIM ** -0.5 * LOG2E)).astype(BF16)

    ckvn = _rms(side[:, 256:384], kvg_ref[...]).astype(BF16)
    kv = jnp.dot(ckvn, wukv_ref[...], preferred_element_type=F32)
    krot = side[:, 384:512] * cos + side[:, 512:640] * sin
    for hh in range(MLA_HEADS):
        lo = hh * LANES
        mlak_ref[:, lo:lo + LANES] = (kv[:, lo:lo + LANES] + krot).astype(BF16)
    mlav_ref[...] = kv[:, 512:768].astype(BF16)

    lane = lax.broadcasted_iota(jnp.int32, (tm, LANES), 1)
    row = lax.broadcasted_iota(jnp.int32, (tm, LANES), 0)
    c = jnp.where(lane < FOX_HEADS, jax.nn.log_sigmoid(side[:, 640:768] + fb_ref[...]), 0.0)
    shift = 1
    while shift < tm:
        c = c + jnp.where(row >= shift, pltpu.roll(c, shift, axis=0), 0.0)
        shift *= 2

    c = c + carry_ref[0:1, :]
    carry_ref[...] = jnp.broadcast_to(c[tm - 1:tm, :], carry_ref.shape)

    main = jnp.dot(h, wmain_ref[...], preferred_element_type=F32)
    scale = HEAD_DIM ** -0.5
    swaq_ref[...] = (main[:, 0:512] * (scale * LOG2E)).astype(BF16)
    swak_ref[...] = main[:, 512:640].astype(BF16)
    swav_ref[...] = main[:, 640:768].astype(BF16)
    foxv_ref[...] = main[:, 1280:1536].astype(BF16)

    for hd in range(FOX_HEADS):
        pair, half = hd // 2, hd % 2
        o = 0 if half else HEAD_DIM
        f = jnp.broadcast_to(jnp.sum(jnp.where(lane == hd, c, 0.0), axis=1, keepdims=True), c.shape) * LOG2E
        hi = f.astype(BF16).astype(F32)
        mid = (f - hi).astype(BF16).astype(F32)
        low = (f - hi) - mid
        one_k = jnp.where(jnp.logical_and(lane >= o + 3, lane < o + 6), 1.0, 0.0)
        one_q = jnp.where(jnp.logical_and(lane >= o, lane < o + 3), 1.0, 0.0)
        extra_k = jnp.where(lane == o, -hi, jnp.where(lane == o + 1, -mid, jnp.where(lane == o + 2, -low, one_k)))
        extra_q = jnp.where(lane == o + 3, hi, jnp.where(lane == o + 4, mid, jnp.where(lane == o + 5, low, one_q)))
        data = _half_lanes(c.shape, half)
        slab = slice(hd * LANES, (hd + 1) * LANES)
        q_pair = main[:, 768 + pair * LANES:768 + (pair + 1) * LANES] * (scale * LOG2E)
        k_pair = main[:, 1024 + pair * LANES:1024 + (pair + 1) * LANES]
        foxq_ref[:, slab] = jnp.where(data, q_pair, extra_q).astype(BF16)
        foxk_ref[:, slab] = jnp.where(data, k_pair, extra_k).astype(BF16)


def _in_proj(x2, g, wmain, wside, fb, qg, wuq, kvg, wukv, cos_t, sin_t, batch, seq):
    T = x2.shape[0]
    tm = IN_PROJ_TILE
    ns = seq // tm
    row = lambda b, s: (b * ns + s, 0)
    seqrow = lambda b, s: (s, 0)
    out_shape = [
        jax.ShapeDtypeStruct((T, 512), BF16), jax.ShapeDtypeStruct((T, 128), BF16),
        jax.ShapeDtypeStruct((T, 128), BF16), jax.ShapeDtypeStruct((T, 512), BF16),
        jax.ShapeDtypeStruct((T, 512), BF16), jax.ShapeDtypeStruct((T, 256), BF16),
        jax.ShapeDtypeStruct((T, 512), BF16), jax.ShapeDtypeStruct((T, 512), BF16),
        jax.ShapeDtypeStruct((T, 256), BF16),
    ]
    out_specs = [
        pl.BlockSpec((tm, 512), row), pl.BlockSpec((tm, 128), row), pl.BlockSpec((tm, 128), row),
        pl.BlockSpec((tm, 512), row), pl.BlockSpec((tm, 512), row), pl.BlockSpec((tm, 256), row),
        pl.BlockSpec((tm, 512), row), pl.BlockSpec((tm, 512), row), pl.BlockSpec((tm, 256), row),
    ]
    in_specs = [
        pl.BlockSpec((tm, D_MODEL), row), _full(g.shape), _full(wmain.shape), _full(wside.shape),
        _full(fb.shape), _full(qg.shape), _full(wuq.shape), _full(kvg.shape), _full(wukv.shape),
        pl.BlockSpec((tm, LANES), seqrow), pl.BlockSpec((tm, LANES), seqrow),
    ]
    return pl.pallas_call(
        _in_proj_kernel,
        grid=(batch, ns),
        in_specs=in_specs,
        out_specs=out_specs,
        out_shape=out_shape,
        scratch_shapes=[pltpu.VMEM((8, LANES), F32)],
        compiler_params=_params("arbitrary", "arbitrary"),
        name="in_proj",
    )(x2, g, wmain, wside, fb, qg, wuq, kvg, wukv, cos_t, sin_t)


def _half_lanes(shape, hh):
    lane = lax.broadcasted_iota(jnp.int32, shape, len(shape) - 1)
    return (lane < HEAD_DIM) if hh == 0 else (lane >= HEAD_DIM)


def _ones_in_other_half(v2, hh):
    return jnp.where(_half_lanes(v2.shape, hh), v2, jnp.ones_like(v2))


def _swa_kernel(q_ref, kc_ref, kp_ref, vc_ref, vp_ref, bias_ref, sink_ref, o_ref, kbuf, vbuf, sbuf, ebuf, mbuf):
    first_tile = pl.program_id(1) == 0
    nwin = q_ref.shape[0] // WINDOW
    npair = SWA_Q_HEADS // 2
    kbuf[0:WINDOW, :] = kp_ref[...]
    kbuf[WINDOW:, :] = kc_ref[...]
    vbuf[0:WINDOW, :] = vp_ref[...]
    vbuf[WINDOW:, :] = vc_ref[...]
    kj = lax.broadcasted_iota(jnp.int32, (1, 2 * WINDOW), 1)
    no_prev = jnp.where(jnp.logical_and(kj < WINDOW, first_tile), NEG_INF, 0.0)
    low = _half_lanes((WINDOW, LANES), 0)
    units = [(w, hh) for w in range(nwin) for hh in range(2)]

    for i, (w, hh) in enumerate(units):
        rows = slice(w * WINDOW, (w + 1) * WINDOW)
        mine = _half_lanes((WINDOW, LANES), hh)
        q4 = [q_ref[rows, p * LANES:(p + 1) * LANES] for p in range(npair)]
        qs = jnp.concatenate([jnp.where(mine, q, jnp.zeros_like(q)) for q in q4], axis=0)
        kw = kbuf[w * WINDOW:(w + 2) * WINDOW, :]
        s = lax.dot_general(qs, kw, NT_DIMS, preferred_element_type=F32) + bias_ref[hh]
        sbuf[i] = s + no_prev if w == 0 else s

    def numerators(i, hh):
        s0, s1 = sbuf[i, :, 0:LANES], sbuf[i, :, LANES:]
        sink = sink_ref[hh]
        mloc = jnp.max(jnp.maximum(s0, s1), axis=1, keepdims=True)
        m = jnp.maximum(jnp.broadcast_to(mloc, sink.shape), sink)
        mbuf[i] = m
        ebuf[i] = jnp.concatenate([jnp.exp2(s0 - m), jnp.exp2(s1 - m)], axis=1).astype(BF16)

    def weighted_values(w):
        rows = slice(w * WINDOW, (w + 1) * WINDOW)
        vw = vbuf[w * WINDOW:(w + 2) * WINDOW, :]
        outs = []
        for hh in range(2):
            i = 2 * w + hh
            pv = jnp.dot(ebuf[i], _ones_in_other_half(vw, hh), preferred_element_type=F32)
            den = pltpu.roll(pv, HEAD_DIM, axis=1) + jnp.exp2(sink_ref[hh] - mbuf[i])
            outs.append(pv / den)
        for p in range(npair):
            pr = slice(p * WINDOW, (p + 1) * WINDOW)
            o_ref[rows, p * LANES:(p + 1) * LANES] = jnp.where(low, outs[0][pr], outs[1][pr]).astype(o_ref.dtype)

    for group in (range(0, nwin // 2), range(nwin // 2, nwin)):
        for w in group:
            for hh in range(2):
                numerators(2 * w + hh, hh)
        for w in group:
            weighted_values(w)


def _swa(q, k, v, bias, sinks, batch, seq):
    T = q.shape[0]
    tm = ROW_TILE
    ns = seq // tm
    per = tm // WINDOW
    unit_rows = (SWA_Q_HEADS // SWA_KV_HEADS) * WINDOW
    row = lambda b, s: (b * ns + s, 0)
    prev = lambda b, s: (jnp.maximum((b * ns + s) * per - 1, 0), 0)
    return pl.pallas_call(
        _swa_kernel,
        grid=(batch, ns),
        in_specs=[
            pl.BlockSpec((tm, SWA_WIDTH), row),
            pl.BlockSpec((tm, LANES), row), pl.BlockSpec((WINDOW, LANES), prev),
            pl.BlockSpec((tm, LANES), row), pl.BlockSpec((WINDOW, LANES), prev),
            _full(bias.shape), _full(sinks.shape),
        ],
        out_specs=pl.BlockSpec((tm, SWA_WIDTH), row),
        out_shape=jax.ShapeDtypeStruct((T, SWA_WIDTH), BF16),
        scratch_shapes=[
            pltpu.VMEM((tm + WINDOW, LANES), BF16), pltpu.VMEM((tm + WINDOW, LANES), BF16),
            pltpu.VMEM((2 * per, unit_rows, 2 * WINDOW), F32),
            pltpu.VMEM((2 * per, unit_rows, 2 * WINDOW), BF16),
            pltpu.VMEM((2 * per, unit_rows, LANES), F32),
        ],
        compiler_params=_params("parallel", "arbitrary"),
        name="swa_attn",
    )(q, k, k, v, v, bias, sinks)


def _flash_init(m_sc, acc_sc):
    m_sc[...] = jnp.full(m_sc.shape, NEG_INF, F32)
    acc_sc[...] = jnp.zeros(acc_sc.shape, F32)


def _flash_update(s_tiles, v2, head, rows, m_sc, acc_sc):
    mloc = functools.reduce(jnp.maximum, s_tiles)
    mloc = jnp.broadcast_to(jnp.max(mloc, axis=1, keepdims=True), mloc.shape)
    m_prev = m_sc[head, rows]
    m_new = jnp.maximum(m_prev, mloc)
    m_sc[head, rows] = m_new
    alpha = jnp.exp2(m_prev - m_new)
    p = jnp.concatenate([jnp.exp2(s - m_new) for s in s_tiles], axis=1).astype(BF16)
    pv = jnp.dot(p, _ones_in_other_half(v2, head % 2), preferred_element_type=F32)
    acc_sc[head, rows] = alpha * acc_sc[head, rows] + pv


def _flash_finish(o_ref, acc_sc):
    low = _half_lanes(acc_sc.shape[1:], 0)
    for pair in range(acc_sc.shape[0] // 2):
        outs = [acc_sc[2 * pair + hh] / pltpu.roll(acc_sc[2 * pair + hh], HEAD_DIM, axis=1) for hh in range(2)]
        o_ref[:, pair * LANES:(pair + 1) * LANES] = jnp.where(low, outs[0], outs[1]).astype(o_ref.dtype)


def _lane_tiles(s):
    return [s[:, j * LANES:(j + 1) * LANES] for j in range(s.shape[1] // LANES)]


def _mask_from(tiles, first):
    r = lax.broadcasted_iota(jnp.int32, tiles[0].shape, 0)
    c = lax.broadcasted_iota(jnp.int32, tiles[0].shape, 1)
    return [x if j < first else jnp.where(c + (j - first) * LANES <= r, x, NEG_INF) for j, x in enumerate(tiles)]


def _flash_kernel(qi_ref, ki_ref, q_ref, k_ref, v_ref, o_ref, m_sc, acc_sc):
    t = pl.program_id(1)
    qi = qi_ref[t]
    ki = ki_ref[t]
    tq, tk = q_ref.shape[0], k_ref.shape[0]
    half = tq // 2
    nheads = q_ref.shape[1] // LANES

    @pl.when(ki == 0)
    def _():
        _flash_init(m_sc, acc_sc)

    def scores(head, rows, keys):
        slab = slice(head * LANES, (head + 1) * LANES)
        s = lax.dot_general(q_ref[rows, slab], k_ref[keys, slab], NT_DIMS, preferred_element_type=F32)
        return _lane_tiles(s)

    def step(diagonal):
        for head in range(nheads):
            pair = slice((head // 2) * LANES, (head // 2 + 1) * LANES)
            if not diagonal:
                _flash_update(scores(head, slice(0, tq), slice(0, tk)), v_ref[:, pair], head, slice(0, tq),
                              m_sc, acc_sc)
                continue
            rows, left = slice(0, tq), slice(0, half)
            bottom, right = slice(half, tq), slice(half, tk)
            _flash_update(_mask_from(scores(head, rows, left), 0), v_ref[left, pair], head, rows, m_sc, acc_sc)
            _flash_update(_mask_from(scores(head, bottom, right), 0), v_ref[right, pair], head, bottom,
                          m_sc, acc_sc)

    @pl.when(ki < qi)
    def _():
        step(False)

    @pl.when(ki == qi)
    def _():
        step(True)
        _flash_finish(o_ref, acc_sc)


def _tri_tables(nq):
    qi = [q for q in range(nq) for _ in range(q + 1)]
    ki = [k for q in range(nq) for k in range(q + 1)]
    return jnp.asarray(qi, jnp.int32), jnp.asarray(ki, jnp.int32)


def _flash(q, k, v, batch, seq, name):
    T = q.shape[0]
    nheads = q.shape[1] // LANES
    tq = ATTN_TILE
    nq = seq // tq
    qi_tab, ki_tab = _tri_tables(nq)
    qrow = lambda b, t, qi, ki: (b * nq + qi[t], 0)
    krow = lambda b, t, qi, ki: (b * nq + ki[t], 0)
    grid_spec = pltpu.PrefetchScalarGridSpec(
        num_scalar_prefetch=2,
        grid=(batch, qi_tab.shape[0]),
        in_specs=[
            pl.BlockSpec((tq, q.shape[1]), qrow), pl.BlockSpec((tq, k.shape[1]), krow),
            pl.BlockSpec((tq, v.shape[1]), krow),
        ],
        out_specs=pl.BlockSpec((tq, v.shape[1]), qrow),
        scratch_shapes=[pltpu.VMEM((nheads, tq, LANES), F32),
                        pltpu.VMEM((nheads, tq, LANES), F32)],
    )
    return pl.pallas_call(
        _flash_kernel,
        grid_spec=grid_spec,
        out_shape=jax.ShapeDtypeStruct((T, v.shape[1]), BF16),
        compiler_params=_params("parallel", "arbitrary"),
        name=name,
    )(qi_tab, ki_tab, q, k, v)


def _mix_ffn_kernel(oa_ref, ob_ref, oc_ref, gg_ref, wout_ref, apg_ref, x_ref,
                    g_ref, wup_ref, cw_ref, cb_ref, wdown_ref, pg_ref, o_ref,
                    hbuf, ubuf, acc_ref, carry_ref):
    tm = x_ref.shape[0]

    @pl.when(pl.program_id(1) == 0)
    def _():
        carry_ref[...] = jnp.zeros_like(carry_ref)

    gg = gg_ref[...]
    for rows in (slice(0, tm // 2), slice(tm // 2, tm)):
        a = _rms(oa_ref[rows, :].astype(F32), gg[:, 0:SWA_WIDTH]).astype(BF16)
        b = _rms(ob_ref[rows, :].astype(F32), gg[:, SWA_WIDTH:SWA_WIDTH + FOX_WIDTH]).astype(BF16)
        c = _rms(oc_ref[rows, :].astype(F32), gg[:, SWA_WIDTH + FOX_WIDTH:]).astype(BF16)
        y = jnp.dot(a, wout_ref[0:SWA_WIDTH, :], preferred_element_type=F32)
        y = y + jnp.dot(b, wout_ref[SWA_WIDTH:SWA_WIDTH + FOX_WIDTH, :], preferred_element_type=F32)
        y = y + jnp.dot(c, wout_ref[SWA_WIDTH + FOX_WIDTH:, :], preferred_element_type=F32)
        x1 = x_ref[rows, :] + _rms(y, apg_ref[...])
        o_ref[rows, :] = x1
        hbuf[rows, :] = _rms(x1, g_ref[...]).astype(BF16)

    row8 = lax.broadcasted_iota(jnp.int32, (8, 2 * FF_CHUNK), 0)

    def gate_up(ref, c):
        starts = (c * FF_CHUNK, D_FF + c * FF_CHUNK)
        if not isinstance(c, int):
            starts = [pl.multiple_of(s, FF_CHUNK) for s in starts]
        return jnp.concatenate([ref[:, pl.ds(s, FF_CHUNK)] for s in starts], axis=1)

    def up_proj(c, slot):
        ubuf[slot] = jnp.dot(hbuf[...], gate_up(wup_ref, c), preferred_element_type=F32)

    def conv_act_down(c, slot, first=False):
        u = ubuf[slot]
        prev = carry_ref[c]
        carry_ref[c] = u[tm - 8:tm, :]
        w = gate_up(cw_ref, c)
        y = gate_up(cb_ref, c) + w[2:3, :] * u
        for d in (1, 2):
            r = pltpu.roll(u, d, axis=0)
            head = jnp.where(row8 < d, pltpu.roll(prev, d, axis=0), r[0:8, :])
            r = jnp.concatenate([head, r[8:, :]], axis=0)
            y = y + w[2 - d:3 - d, :] * r
        gate, half_up = y[:, 0:FF_CHUNK], y[:, FF_CHUNK:]
        t = jnp.tanh(gate * (GELU_K0 + GELU_K1 * (gate * gate)))
        act = ((gate * half_up) * (1.0 + t)).astype(BF16)
        down = jnp.dot(act, wdown_ref[c], preferred_element_type=F32)
        if first:
            acc_ref[...] = down
        else:
            acc_ref[...] += down

    up_proj(0, 0)
    up_proj(1, 1)
    conv_act_down(0, 0, first=True)

    def two_chunks(j, _):
        c = 2 * j + 1
        up_proj(c + 1, 0)
        conv_act_down(c, 1)
        up_proj(c + 2, 1)
        conv_act_down(c + 1, 0)
        return 0

    lax.fori_loop(0, (N_FF_CHUNKS - 3) // 2, two_chunks, 0)
    up_proj(N_FF_CHUNKS - 1, 0)
    conv_act_down(N_FF_CHUNKS - 2, 1)
    conv_act_down(N_FF_CHUNKS - 1, 0)
    o_ref[...] += _rms(acc_ref[...], pg_ref[...])


def _mix_ffn(oa, ob, oc, gg, wout, apg, x2, g, wup, cw, cb, wdown, pg, batch, seq):
    T = x2.shape[0]
    tm = MIX_TILE
    ns = seq // tm
    row = lambda b, s: (b * ns + s, 0)
    return pl.pallas_call(
        _mix_ffn_kernel,
        grid=(batch, ns),
        in_specs=[
            pl.BlockSpec((tm, SWA_WIDTH), row), pl.BlockSpec((tm, FOX_WIDTH), row),
            pl.BlockSpec((tm, MLA_WIDTH), row), _full(gg.shape), _full(wout.shape), _full(apg.shape),
            pl.BlockSpec((tm, D_MODEL), row), _full(g.shape), _full(wup.shape), _full(cw.shape),
            _full(cb.shape), _full(wdown.shape), _full(pg.shape),
        ],
        out_specs=pl.BlockSpec((tm, D_MODEL), row),
        out_shape=jax.ShapeDtypeStruct((T, D_MODEL), F32),
        scratch_shapes=[
            pltpu.VMEM((tm, D_MODEL), BF16),
            pltpu.VMEM((2, tm, 2 * FF_CHUNK), F32),
            pltpu.VMEM((tm, D_MODEL), F32),
            pltpu.VMEM((N_FF_CHUNKS, 8, 2 * FF_CHUNK), F32),
        ],
        compiler_params=_params("parallel", "arbitrary"),
        name="mix_ffn",
    )(oa, ob, oc, gg, wout, apg, x2, g, wup, cw, cb, wdown, pg)


def _t5_causal_bucket(dist):
    max_exact = REL_BUCKETS // 2
    d = jnp.maximum(dist, 0)
    log_ratio = jnp.log(jnp.maximum(d, 1).astype(F32) / max_exact) / math.log(REL_MAX_DIST / max_exact)
    large = max_exact + (log_ratio * (REL_BUCKETS - max_exact)).astype(jnp.int32)
    large = jnp.minimum(large, REL_BUCKETS - 1)
    return jnp.where(d < max_exact, d, large)


def _swa_bias_table(rel_bias):
    qi = jnp.arange(WINDOW, dtype=jnp.int32)[:, None] + WINDOW
    kj = jnp.arange(2 * WINDOW, dtype=jnp.int32)[None, :]
    dist = qi - kj
    in_band = (dist >= 0) & (dist < WINDOW)
    hit = _t5_causal_bucket(dist)[None, :, :] == jnp.arange(REL_BUCKETS, dtype=jnp.int32)[:, None, None]
    bias = jnp.sum(jnp.where(hit[None], rel_bias.astype(F32).T[:, :, None, None], 0.0), axis=1)
    bias = jnp.where(in_band[None], bias, NEG_INF)
    bias = bias * LOG2E
    return bias.reshape(SWA_KV_HEADS, (SWA_Q_HEADS // SWA_KV_HEADS) * WINDOW, 2 * WINDOW)


def _rope_tables(seq):
    pos = jnp.arange(seq, dtype=F32)
    inv_freq = ROPE_THETA ** (-(jnp.arange(MLA_ROPE_DIM // 2, dtype=F32) * 2.0 / MLA_ROPE_DIM))
    ang = pos[:, None] * inv_freq[None, :]
    cos, sin = jnp.cos(ang), jnp.sin(ang)
    pad = jnp.zeros((seq, LANES - MLA_QK_DIM), F32)
    cos_t = jnp.concatenate([jnp.ones((seq, MLA_NOPE_DIM), F32), cos, cos, pad], axis=1)
    sin_t = jnp.concatenate([jnp.zeros((seq, MLA_NOPE_DIM), F32), -sin, sin, pad], axis=1)
    return cos_t, sin_t


def _swap_halves(w):
    half = w.shape[1] // 2
    return jnp.concatenate([w[:, half:], w[:, :half]], axis=1)


def _layer_params(l, w_in, forget_bias, w_uq, w_ukv, group_norm, w_out, w_up, conv_w, conv_b, w_down):
    wi = w_in[l]
    d = wi.shape[0]
    order = jnp.asarray(SWA_HEAD_ORDER)
    swa_q = wi[:, 0:SWA_WIDTH].reshape(d, SWA_Q_HEADS, HEAD_DIM)[:, order].reshape(d, SWA_WIDTH)
    wmain = jnp.concatenate([swa_q, wi[:, SWA_WIDTH:SWA_COLS + 3 * FOX_WIDTH]], axis=1).astype(BF16)

    f0 = SWA_COLS + 3 * FOX_WIDTH
    m0 = SWA_COLS + FOX_COLS
    z = lambda n: jnp.zeros((d, n), F32)
    w_kr = wi[:, m0 + MLA_Q_RANK + MLA_KV_RANK:]
    rope_pad = LANES - MLA_QK_DIM
    wside = jnp.concatenate([
        wi[:, m0:m0 + MLA_Q_RANK + MLA_KV_RANK],
        z(MLA_NOPE_DIM), w_kr, z(rope_pad),
        z(MLA_NOPE_DIM), _swap_halves(w_kr), z(rope_pad),
        wi[:, f0:f0 + FOX_HEADS], z(LANES - FOX_HEADS),
    ], axis=1).astype(BF16)
    fb = jnp.pad(forget_bias[l], (0, LANES - FOX_HEADS))[None, :]

    uq = w_uq[l].reshape(MLA_Q_RANK, MLA_HEADS, MLA_QK_DIM)
    zq = jnp.zeros((MLA_Q_RANK, MLA_HEADS, 1), F32)
    plain = jnp.concatenate([uq, jnp.tile(zq, (1, 1, rope_pad))], axis=2)
    rope = uq[:, :, MLA_NOPE_DIM:]
    swapped = jnp.concatenate([jnp.tile(zq, (1, 1, MLA_NOPE_DIM)), rope[:, :, MLA_ROPE_DIM // 2:],
                               rope[:, :, :MLA_ROPE_DIM // 2], jnp.tile(zq, (1, 1, rope_pad))], axis=2)
    wuq = jnp.concatenate([plain.reshape(MLA_Q_RANK, -1), swapped.reshape(MLA_Q_RANK, -1)], axis=1).astype(BF16)

    ukv = w_ukv[l].reshape(MLA_KV_RANK, MLA_HEADS, MLA_NOPE_DIM + MLA_V_DIM)
    k_slabs = jnp.concatenate([ukv[:, :, :MLA_NOPE_DIM],
                               jnp.zeros((MLA_KV_RANK, MLA_HEADS, LANES - MLA_NOPE_DIM), F32)], axis=2)
    wukv = jnp.concatenate([k_slabs.reshape(MLA_KV_RANK, -1),
                            ukv[:, :, MLA_NOPE_DIM:].reshape(MLA_KV_RANK, -1)], axis=1).astype(BF16)

    gg = group_norm[l]
    gg = jnp.concatenate([gg[:SWA_WIDTH].reshape(SWA_Q_HEADS, HEAD_DIM)[order].reshape(-1), gg[SWA_WIDTH:]])[None, :]
    wo = w_out[l]
    wo = jnp.concatenate([wo[:SWA_WIDTH].reshape(SWA_Q_HEADS, HEAD_DIM, -1)[order].reshape(SWA_WIDTH, -1),
                          wo[SWA_WIDTH:]], axis=0).astype(BF16)

    wup = w_up[l].astype(BF16)
    halve_up = jnp.concatenate([jnp.ones((D_FF,), F32), jnp.full((D_FF,), 0.5, F32)])
    cw = conv_w[l] * halve_up
    cb = (conv_b[l] * halve_up)[None, :]
    wdown = w_down[l].astype(BF16).reshape(N_FF_CHUNKS, FF_CHUNK, -1)
    return wmain, wside, fb, wuq, wukv, gg, wo, wup, cw, cb, wdown


def kernel(x, attn_pre_norm, w_in, forget_bias, swa_sinks, rel_bias, q_latent_norm, w_uq, kv_latent_norm, w_ukv,
           group_norm, w_out, attn_post_norm, ffn_pre_norm, w_up, conv_w, conv_b, w_down, ffn_post_norm):
    batch, seq, d = x.shape
    assert d == D_MODEL and all(seq % t == 0 for t in (ROW_TILE, ATTN_TILE, IN_PROJ_TILE, MIX_TILE))
    depth = w_in.shape[0]
    cos_t, sin_t = _rope_tables(seq)
    bias = _swa_bias_table(rel_bias)
    x2 = x.reshape(batch * seq, d)
    for l in range(depth):
        wmain, wside, fb, wuq, wukv, gg, wo, wup, cw, cb, wdown = _layer_params(
            l, w_in, forget_bias, w_uq, w_ukv, group_norm, w_out, w_up, conv_w, conv_b, w_down)
        sinks = jnp.repeat((swa_sinks[l].astype(F32) * LOG2E).reshape(SWA_KV_HEADS, -1), WINDOW, axis=1)
        sinks = jnp.broadcast_to(sinks[:, :, None], sinks.shape + (LANES,))
        (swa_q, swa_k, swa_v, fox_q, fox_k, fox_v, mla_q, mla_k, mla_v) = _in_proj(
            x2, attn_pre_norm[l][None, :], wmain, wside, fb, q_latent_norm[l][None, :], wuq,
            kv_latent_norm[l][None, :], wukv, cos_t, sin_t, batch, seq)
        out_a = _swa(swa_q, swa_k, swa_v, bias, sinks, batch, seq)
        out_b = _flash(fox_q, fox_k, fox_v, batch, seq, "fox_attn")
        out_c = _flash(mla_q, mla_k, mla_v, batch, seq, "mla_attn")
        x2 = _mix_ffn(out_a, out_b, out_c, gg, wo, attn_post_norm[l][None, :], x2, ffn_pre_norm[l][None, :],
                      wup, cw, cb, wdown, ffn_post_norm[l][None, :], batch, seq)
    return x2.reshape(batch, seq, d)
```

```python
import functools
import math

import jax
import jax.numpy as jnp
from jax import lax
from jax.experimental import pallas as pl
from jax.experimental.pallas import tpu as pltpu

D_MODEL = 1024
HEAD_DIM = 64
SWA_Q_HEADS = 8
SWA_KV_HEADS = 2
WINDOW = 128
FOX_HEADS = 4
MLA_HEADS = 4
MLA_Q_RANK = 256
MLA_KV_RANK = 128
MLA_NOPE_DIM = 64
MLA_ROPE_DIM = 32
MLA_V_DIM = 64
MLA_QK_DIM = MLA_NOPE_DIM + MLA_ROPE_DIM
ROPE_THETA = 10000.0
REL_BUCKETS = 32
REL_MAX_DIST = 128
D_FF = 2816
EPS = 1e-6
NEG_INF = -1e30

SWA_WIDTH = SWA_Q_HEADS * HEAD_DIM
FOX_WIDTH = FOX_HEADS * HEAD_DIM
MLA_WIDTH = MLA_HEADS * MLA_V_DIM
SWA_COLS = (SWA_Q_HEADS + 2 * SWA_KV_HEADS) * HEAD_DIM
FOX_COLS = 3 * FOX_WIDTH + FOX_HEADS

LANES = 128
ROW_TILE = 1024
IN_PROJ_TILE = 1024
MIX_TILE = 1024
ATTN_TILE = 1024
FF_CHUNK = 256
N_FF_CHUNKS = D_FF // FF_CHUNK
VMEM_LIMIT = 60 * 1024 * 1024

SWA_HEAD_ORDER = (0, 4, 1, 5, 2, 6, 3, 7)

F32 = jnp.float32
BF16 = jnp.bfloat16
NT_DIMS = (((1,), (1,)), ((), ()))
LOG2E = math.log2(math.e)
GELU_K0 = math.sqrt(2.0 / math.pi)
GELU_K1 = GELU_K0 * 0.044715


def _rms(x, g):
    ms = jnp.mean(x * x, axis=-1, keepdims=True)
    return x * lax.rsqrt(ms + EPS) * g


def _params(*sem):
    return pltpu.CompilerParams(dimension_semantics=sem, vmem_limit_bytes=VMEM_LIMIT)


def _full(shape):
    nd = len(shape)
    return pl.BlockSpec(shape, lambda *_: (0,) * nd)


def _in_proj_kernel(x_ref, g_ref, wmain_ref, wside_ref, fb_ref, qg_ref, wuq_ref, kvg_ref, wukv_ref,
                    cos_ref, sin_ref,
                    swaq_ref, swak_ref, swav_ref, foxq_ref, foxk_ref, foxv_ref,
                    mlaq_ref, mlak_ref, mlav_ref, carry_ref):
    tm = x_ref.shape[0]
    @pl.when(pl.program_id(1) == 0)
    def _():
        carry_ref[...] = jnp.zeros_like(carry_ref)

    h = _rms(x_ref[...], g_ref[...]).astype(BF16)

    side = jnp.dot(h, wside_ref[...], preferred_element_type=F32)
    cos = cos_ref[...]
    sin = sin_ref[...]

    cqn = _rms(side[:, 0:256], qg_ref[...]).astype(BF16)
    qa = jnp.dot(cqn, wuq_ref[...], preferred_element_type=F32)
    for hh in range(MLA_HEADS):
        lo = hh * LANES
        a = qa[:, lo:lo + LANES]
        b = qa[:, 512 + lo:512 + lo + LANES]
        mlaq_ref[:, lo:lo + LANES] = ((a * cos + b * sin) * (MLA_QK_DIM ** -0.5 * LOG2E)).astype(BF16)

    ckvn = _rms(side[:, 256:384], kvg_ref[...]).astype(BF16)
    kv = jnp.dot(ckvn, wukv_ref[...], preferred_element_type=F32)
    krot = side[:, 384:512] * cos + side[:, 512:640] * sin
    for hh in range(MLA_HEADS):
        lo = hh * LANES
        mlak_ref[:, lo:lo + LANES] = (kv[:, lo:lo + LANES] + krot).astype(BF16)
    mlav_ref[...] = kv[:, 512:768].astype(BF16)

    lane = lax.broadcasted_iota(jnp.int32, (tm, LANES), 1)
    row = lax.broadcasted_iota(jnp.int32, (tm, LANES), 0)
    c = jnp.where(lane < FOX_HEADS, jax.nn.log_sigmoid(side[:, 640:768] + fb_ref[...]), 0.0)
    shift = 1
    while shift < tm:
        c = c + jnp.where(row >= shift, pltpu.roll(c, shift, axis=0), 0.0)
        shift *= 2

    c = c + carry_ref[0:1, :]
    carry_ref[...] = jnp.broadcast_to(c[tm - 1:tm, :], carry_ref.shape)

    main = jnp.dot(h, wmain_ref[...], preferred_element_type=F32)
    scale = HEAD_DIM ** -0.5
    swaq_ref[...] = (main[:, 0:512] * (scale * LOG2E)).astype(BF16)
    swak_ref[...] = main[:, 512:640].astype(BF16)
    swav_ref[...] = main[:, 640:768].astype(BF16)
    foxv_ref[...] = main[:, 1280:1536].astype(BF16)

    for hd in range(FOX_HEADS):
        pair, half = hd // 2, hd % 2
        o = 0 if half else HEAD_DIM
        f = jnp.broadcast_to(jnp.sum(jnp.where(lane == hd, c, 0.0), axis=1, keepdims=True), c.shape) * LOG2E
        hi = f.astype(BF16).astype(F32)
        mid = (f - hi).astype(BF16).astype(F32)
        low = (f - hi) - mid
        one_k = jnp.where(jnp.logical_and(lane >= o + 3, lane < o + 6), 1.0, 0.0)
        one_q = jnp.where(jnp.logical_and(lane >= o, lane < o + 3), 1.0, 0.0)
        extra_k = jnp.where(lane == o, -hi, jnp.where(lane == o + 1, -mid, jnp.where(lane == o + 2, -low, one_k)))
        extra_q = jnp.where(lane == o + 3, hi, jnp.where(lane == o + 4, mid, jnp.where(lane == o + 5, low, one_q)))
        data = _half_lanes(c.shape, half)
        slab = slice(hd * LANES, (hd + 1) * LANES)
        q_pair = main[:, 768 + pair * LANES:768 + (pair + 1) * LANES] * (scale * LOG2E)
        k_pair = main[:, 1024 + pair * LANES:1024 + (pair + 1) * LANES]
        foxq_ref[:, slab] = jnp.where(data, q_pair, extra_q).astype(BF16)
        foxk_ref[:, slab] = jnp.where(data, k_pair, extra_k).astype(BF16)


def _in_proj(x2, g, wmain, wside, fb, qg, wuq, kvg, wukv, cos_t, sin_t, batch, seq):
    T = x2.shape[0]
    tm = IN_PROJ_TILE
    ns = seq // tm
    row = lambda b, s: (b * ns + s, 0)
    seqrow = lambda b, s: (s, 0)
    out_shape = [
        jax.ShapeDtypeStruct((T, 512), BF16), jax.ShapeDtypeStruct((T, 128), BF16),
        jax.ShapeDtypeStruct((T, 128), BF16), jax.ShapeDtypeStruct((T, 512), BF16),
        jax.ShapeDtypeStruct((T, 512), BF16), jax.ShapeDtypeStruct((T, 256), BF16),
        jax.ShapeDtypeStruct((T, 512), BF16), jax.ShapeDtypeStruct((T, 512), BF16),
        jax.ShapeDtypeStruct((T, 256), BF16),
    ]
    out_specs = [
        pl.BlockSpec((tm, 512), row), pl.BlockSpec((tm, 128), row), pl.BlockSpec((tm, 128), row),
        pl.BlockSpec((tm, 512), row), pl.BlockSpec((tm, 512), row), pl.BlockSpec((tm, 256), row),
        pl.BlockSpec((tm, 512), row), pl.BlockSpec((tm, 512), row), pl.BlockSpec((tm, 256), row),
    ]
    in_specs = [
        pl.BlockSpec((tm, D_MODEL), row), _full(g.shape), _full(wmain.shape), _full(wside.shape),
        _full(fb.shape), _full(qg.shape), _full(wuq.shape), _full(kvg.shape), _full(wukv.shape),
        pl.BlockSpec((tm, LANES), seqrow), pl.BlockSpec((tm, LANES), seqrow),
    ]
    return pl.pallas_call(
        _in_proj_kernel,
        grid=(batch, ns),
        in_specs=in_specs,
        out_specs=out_specs,
        out_shape=out_shape,
        scratch_shapes=[pltpu.VMEM((8, LANES), F32)],
        compiler_params=_params("arbitrary", "arbitrary"),
        name="in_proj",
    )(x2, g, wmain, wside, fb, qg, wuq, kvg, wukv, cos_t, sin_t)


def _half_lanes(shape, hh):
    lane = lax.broadcasted_iota(jnp.int32, shape, len(shape) - 1)
    return (lane < HEAD_DIM) if hh == 0 else (lane >= HEAD_DIM)


def _ones_in_other_half(v2, hh):
    return jnp.where(_half_lanes(v2.shape, hh), v2, jnp.ones_like(v2))


def _swa_kernel(q_ref, kc_ref, kp_ref, vc_ref, vp_ref, bias_ref, sink_ref, o_ref, kbuf, vbuf, sbuf, ebuf, mbuf):
    first_tile = pl.program_id(1) == 0
    nwin = q_ref.shape[0] // WINDOW
    npair = SWA_Q_HEADS // 2
    kbuf[0:WINDOW, :] = kp_ref[...]
    kbuf[WINDOW:, :] = kc_ref[...]
    vbuf[0:WINDOW, :] = vp_ref[...]
    vbuf[WINDOW:, :] = vc_ref[...]
    kj = lax.broadcasted_iota(jnp.int32, (1, 2 * WINDOW), 1)
    no_prev = jnp.where(jnp.logical_and(kj < WINDOW, first_tile), NEG_INF, 0.0)
    low = _half_lanes((WINDOW, LANES), 0)
    units = [(w, hh) for w in range(nwin) for hh in range(2)]

    for i, (w, hh) in enumerate(units):
        rows = slice(w * WINDOW, (w + 1) * WINDOW)
        mine = _half_lanes((WINDOW, LANES), hh)
        q4 = [q_ref[rows, p * LANES:(p + 1) * LANES] for p in range(npair)]
        qs = jnp.concatenate([jnp.where(mine, q, jnp.zeros_like(q)) for q in q4], axis=0)
        kw = kbuf[w * WINDOW:(w + 2) * WINDOW, :]
        s = lax.dot_general(qs, kw, NT_DIMS, preferred_element_type=F32) + bias_ref[hh]
        sbuf[i] = s + no_prev if w == 0 else s

    def numerators(i, hh):
        s0, s1 = sbuf[i, :, 0:LANES], sbuf[i, :, LANES:]
        sink = sink_ref[hh]
        mloc = jnp.max(jnp.maximum(s0, s1), axis=1, keepdims=True)
        m = jnp.maximum(jnp.broadcast_to(mloc, sink.shape), sink)
        mbuf[i] = m
        ebuf[i] = jnp.concatenate([jnp.exp2(s0 - m), jnp.exp2(s1 - m)], axis=1).astype(BF16)

    def weighted_values(w):
        rows = slice(w * WINDOW, (w + 1) * WINDOW)
        vw = vbuf[w * WINDOW:(w + 2) * WINDOW, :]
        outs = []
        for hh in range(2):
            i = 2 * w + hh
            pv = jnp.dot(ebuf[i], _ones_in_other_half(vw, hh), preferred_element_type=F32)
            den = pltpu.roll(pv, HEAD_DIM, axis=1) + jnp.exp2(sink_ref[hh] - mbuf[i])
            outs.append(pv / den)
        for p in range(npair):
            pr = slice(p * WINDOW, (p + 1) * WINDOW)
            o_ref[rows, p * LANES:(p + 1) * LANES] = jnp.where(low, outs[0][pr], outs[1][pr]).astype(o_ref.dtype)

    for i, (w, hh) in enumerate(units):
        numerators(i, hh)
    for w in range(nwin):
        weighted_values(w)


def _swa(q, k, v, bias, sinks, batch, seq):
    T = q.shape[0]
    tm = ROW_TILE
    ns = seq // tm
    per = tm // WINDOW
    unit_rows = (SWA_Q_HEADS // SWA_KV_HEADS) * WINDOW
    row = lambda b, s: (b * ns + s, 0)
    prev = lambda b, s: (jnp.maximum((b * ns + s) * per - 1, 0), 0)
    return pl.pallas_call(
        _swa_kernel,
        grid=(batch, ns),
        in_specs=[
            pl.BlockSpec((tm, SWA_WIDTH), row),
            pl.BlockSpec((tm, LANES), row), pl.BlockSpec((WINDOW, LANES), prev),
            pl.BlockSpec((tm, LANES), row), pl.BlockSpec((WINDOW, LANES), prev),
            _full(bias.shape), _full(sinks.shape),
        ],
        out_specs=pl.BlockSpec((tm, SWA_WIDTH), row),
        out_shape=jax.ShapeDtypeStruct((T, SWA_WIDTH), BF16),
        scratch_shapes=[
            pltpu.VMEM((tm + WINDOW, LANES), BF16), pltpu.VMEM((tm + WINDOW, LANES), BF16),
            pltpu.VMEM((2 * per, unit_rows, 2 * WINDOW), F32),
            pltpu.VMEM((2 * per, unit_rows, 2 * WINDOW), BF16),
            pltpu.VMEM((2 * per, unit_rows, LANES), F32),
        ],
        compiler_params=_params("parallel", "arbitrary"),
        name="swa_attn",
    )(q, k, k, v, v, bias, sinks)


def _flash_init(m_sc, acc_sc):
    m_sc[...] = jnp.full(m_sc.shape, NEG_INF, F32)
    acc_sc[...] = jnp.zeros(acc_sc.shape, F32)


def _flash_update(s_tiles, v2, head, rows, m_sc, acc_sc):
    mloc = functools.reduce(jnp.maximum, s_tiles)
    mloc = jnp.broadcast_to(jnp.max(mloc, axis=1, keepdims=True), mloc.shape)
    m_prev = m_sc[head, rows]
    m_new = jnp.maximum(m_prev, mloc)
    m_sc[head, rows] = m_new
    alpha = jnp.exp2(m_prev - m_new)
    p = jnp.concatenate([jnp.exp2(s - m_new) for s in s_tiles], axis=1).astype(BF16)
    pv = jnp.dot(p, _ones_in_other_half(v2, head % 2), preferred_element_type=F32)
    acc_sc[head, rows] = alpha * acc_sc[head, rows] + pv


def _flash_finish(o_ref, acc_sc):
    low = _half_lanes(acc_sc.shape[1:], 0)
    for pair in range(acc_sc.shape[0] // 2):
        outs = [acc_sc[2 * pair + hh] / pltpu.roll(acc_sc[2 * pair + hh], HEAD_DIM, axis=1) for hh in range(2)]
        o_ref[:, pair * LANES:(pair + 1) * LANES] = jnp.where(low, outs[0], outs[1]).astype(o_ref.dtype)


def _lane_tiles(s):
    return [s[:, j * LANES:(j + 1) * LANES] for j in range(s.shape[1] // LANES)]


def _mask_from(tiles, first):
    r = lax.broadcasted_iota(jnp.int32, tiles[0].shape, 0)
    c = lax.broadcasted_iota(jnp.int32, tiles[0].shape, 1)
    return [x if j < first else jnp.where(c + (j - first) * LANES <= r, x, NEG_INF) for j, x in enumerate(tiles)]


def _flash_kernel(qi_ref, ki_ref, q_ref, k_ref, v_ref, o_ref, m_sc, acc_sc):
    t = pl.program_id(1)
    qi = qi_ref[t]
    ki = ki_ref[t]
    tq, tk = q_ref.shape[0], k_ref.shape[0]
    half = tq // 2
    nheads = q_ref.shape[1] // LANES

    @pl.when(ki == 0)
    def _():
        _flash_init(m_sc, acc_sc)

    def scores(head, rows, keys):
        slab = slice(head * LANES, (head + 1) * LANES)
        s = lax.dot_general(q_ref[rows, slab], k_ref[keys, slab], NT_DIMS, preferred_element_type=F32)
        return _lane_tiles(s)

    def step(diagonal):
        for head in range(nheads):
            pair = slice((head // 2) * LANES, (head // 2 + 1) * LANES)
            if not diagonal:
                _flash_update(scores(head, slice(0, tq), slice(0, tk)), v_ref[:, pair], head, slice(0, tq),
                              m_sc, acc_sc)
                continue
            rows, left = slice(0, tq), slice(0, half)
            bottom, right = slice(half, tq), slice(half, tk)
            _flash_update(_mask_from(scores(head, rows, left), 0), v_ref[left, pair], head, rows, m_sc, acc_sc)
            _flash_update(_mask_from(scores(head, bottom, right), 0), v_ref[right, pair], head, bottom,
                          m_sc, acc_sc)

    @pl.when(ki < qi)
    def _():
        step(False)

    @pl.when(ki == qi)
    def _():
        step(True)
        _flash_finish(o_ref, acc_sc)


def _tri_tables(nq):
    qi = [q for q in range(nq) for _ in range(q + 1)]
    ki = [k for q in range(nq) for k in range(q + 1)]
    return jnp.asarray(qi, jnp.int32), jnp.asarray(ki, jnp.int32)


def _flash(q, k, v, batch, seq, name):
    T = q.shape[0]
    nheads = q.shape[1] // LANES
    tq = ATTN_TILE
    nq = seq // tq
    qi_tab, ki_tab = _tri_tables(nq)
    qrow = lambda b, t, qi, ki: (b * nq + qi[t], 0)
    krow = lambda b, t, qi, ki: (b * nq + ki[t], 0)
    grid_spec = pltpu.PrefetchScalarGridSpec(
        num_scalar_prefetch=2,
        grid=(batch, qi_tab.shape[0]),
        in_specs=[
            pl.BlockSpec((tq, q.shape[1]), qrow), pl.BlockSpec((tq, k.shape[1]), krow),
            pl.BlockSpec((tq, v.shape[1]), krow),
        ],
        out_specs=pl.BlockSpec((tq, v.shape[1]), qrow),
        scratch_shapes=[pltpu.VMEM((nheads, tq, LANES), F32),
                        pltpu.VMEM((nheads, tq, LANES), F32)],
    )
    return pl.pallas_call(
        _flash_kernel,
        grid_spec=grid_spec,
        out_shape=jax.ShapeDtypeStruct((T, v.shape[1]), BF16),
        compiler_params=_params("parallel", "arbitrary"),
        name=name,
    )(qi_tab, ki_tab, q, k, v)


def _mix_ffn_kernel(oa_ref, ob_ref, oc_ref, gg_ref, wout_ref, apg_ref, x_ref,
                    g_ref, wup_ref, cw_ref, cb_ref, wdown_ref, pg_ref, o_ref,
                    hbuf, ubuf, acc_ref, carry_ref):
    tm = x_ref.shape[0]

    @pl.when(pl.program_id(1) == 0)
    def _():
        carry_ref[...] = jnp.zeros_like(carry_ref)

    gg = gg_ref[...]
    for rows in (slice(0, tm // 2), slice(tm // 2, tm)):
        a = _rms(oa_ref[rows, :].astype(F32), gg[:, 0:SWA_WIDTH]).astype(BF16)
        b = _rms(ob_ref[rows, :].astype(F32), gg[:, SWA_WIDTH:SWA_WIDTH + FOX_WIDTH]).astype(BF16)
        c = _rms(oc_ref[rows, :].astype(F32), gg[:, SWA_WIDTH + FOX_WIDTH:]).astype(BF16)
        y = jnp.dot(a, wout_ref[0:SWA_WIDTH, :], preferred_element_type=F32)
        y = y + jnp.dot(b, wout_ref[SWA_WIDTH:SWA_WIDTH + FOX_WIDTH, :], preferred_element_type=F32)
        y = y + jnp.dot(c, wout_ref[SWA_WIDTH + FOX_WIDTH:, :], preferred_element_type=F32)
        x1 = x_ref[rows, :] + _rms(y, apg_ref[...])
        o_ref[rows, :] = x1
        hbuf[rows, :] = _rms(x1, g_ref[...]).astype(BF16)

    row8 = lax.broadcasted_iota(jnp.int32, (8, 2 * FF_CHUNK), 0)

    def gate_up(ref, c):
        starts = (c * FF_CHUNK, D_FF + c * FF_CHUNK)
        if not isinstance(c, int):
            starts = [pl.multiple_of(s, FF_CHUNK) for s in starts]
        return jnp.concatenate([ref[:, pl.ds(s, FF_CHUNK)] for s in starts], axis=1)

    def up_proj(c, slot):
        ubuf[slot] = jnp.dot(hbuf[...], gate_up(wup_ref, c), preferred_element_type=F32)

    def conv_act_down(c, slot, first=False):
        u = ubuf[slot]
        prev = carry_ref[c]
        carry_ref[c] = u[tm - 8:tm, :]
        w = gate_up(cw_ref, c)
        y = gate_up(cb_ref, c) + w[2:3, :] * u
        for d in (1, 2):
            r = pltpu.roll(u, d, axis=0)
            head = jnp.where(row8 < d, pltpu.roll(prev, d, axis=0), r[0:8, :])
            r = jnp.concatenate([head, r[8:, :]], axis=0)
            y = y + w[2 - d:3 - d, :] * r
        gate, half_up = y[:, 0:FF_CHUNK], y[:, FF_CHUNK:]
        t = jnp.tanh(gate * (GELU_K0 + GELU_K1 * (gate * gate)))
        act = ((gate * half_up) * (1.0 + t)).astype(BF16)
        down = jnp.dot(act, wdown_ref[c], preferred_element_type=F32)
        if first:
            acc_ref[...] = down
        else:
            acc_ref[...] += down

    up_proj(0, 0)
    up_proj(1, 1)
    conv_act_down(0, 0, first=True)

    def two_chunks(j, _):
        c = 2 * j + 1
        up_proj(c + 1, 0)
        conv_act_down(c, 1)
        up_proj(c + 2, 1)
        conv_act_down(c + 1, 0)
        return 0

    lax.fori_loop(0, (N_FF_CHUNKS - 3) // 2, two_chunks, 0)
    up_proj(N_FF_CHUNKS - 1, 0)
    conv_act_down(N_FF_CHUNKS - 2, 1)
    conv_act_down(N_FF_CHUNKS - 1, 0)
    o_ref[...] += _rms(acc_ref[...], pg_ref[...])


def _mix_ffn(oa, ob, oc, gg, wout, apg, x2, g, wup, cw, cb, wdown, pg, batch, seq):
    T = x2.shape[0]
    tm = MIX_TILE
    ns = seq // tm
    row = lambda b, s: (b * ns + s, 0)
    return pl.pallas_call(
        _mix_ffn_kernel,
        grid=(batch, ns),
        in_specs=[
            pl.BlockSpec((tm, SWA_WIDTH), row), pl.BlockSpec((tm, FOX_WIDTH), row),
            pl.BlockSpec((tm, MLA_WIDTH), row), _full(gg.shape), _full(wout.shape), _full(apg.shape),
            pl.BlockSpec((tm, D_MODEL), row), _full(g.shape), _full(wup.shape), _full(cw.shape),
            _full(cb.shape), _full(wdown.shape), _full(pg.shape),
        ],
        out_specs=pl.BlockSpec((tm, D_MODEL), row),
        out_shape=jax.ShapeDtypeStruct((T, D_MODEL), F32),
        scratch_shapes=[
            pltpu.VMEM((tm, D_MODEL), BF16),
            pltpu.VMEM((2, tm, 2 * FF_CHUNK), F32),
            pltpu.VMEM((tm, D_MODEL), F32),
            pltpu.VMEM((N_FF_CHUNKS, 8, 2 * FF_CHUNK), F32),
        ],
        compiler_params=_params("parallel", "arbitrary"),
        name="mix_ffn",
    )(oa, ob, oc, gg, wout, apg, x2, g, wup, cw, cb, wdown, pg)


def _t5_causal_bucket(dist):
    max_exact = REL_BUCKETS // 2
    d = jnp.maximum(dist, 0)
    log_ratio = jnp.log(jnp.maximum(d, 1).astype(F32) / max_exact) / math.log(REL_MAX_DIST / max_exact)
    large = max_exact + (log_ratio * (REL_BUCKETS - max_exact)).astype(jnp.int32)
    large = jnp.minimum(large, REL_BUCKETS - 1)
    return jnp.where(d < max_exact, d, large)


def _swa_bias_table(rel_bias):
    qi = jnp.arange(WINDOW, dtype=jnp.int32)[:, None] + WINDOW
    kj = jnp.arange(2 * WINDOW, dtype=jnp.int32)[None, :]
    dist = qi - kj
    in_band = (dist >= 0) & (dist < WINDOW)
    hit = _t5_causal_bucket(dist)[None, :, :] == jnp.arange(REL_BUCKETS, dtype=jnp.int32)[:, None, None]
    bias = jnp.sum(jnp.where(hit[None], rel_bias.astype(F32).T[:, :, None, None], 0.0), axis=1)
    bias = jnp.where(in_band[None], bias, NEG_INF)
    bias = bias * LOG2E
    return bias.reshape(SWA_KV_HEADS, (SWA_Q_HEADS // SWA_KV_HEADS) * WINDOW, 2 * WINDOW)


def _rope_tables(seq):
    pos = jnp.arange(seq, dtype=F32)
    inv_freq = ROPE_THETA ** (-(jnp.arange(MLA_ROPE_DIM // 2, dtype=F32) * 2.0 / MLA_ROPE_DIM))
    ang = pos[:, None] * inv_freq[None, :]
    cos, sin = jnp.cos(ang), jnp.sin(ang)
    pad = jnp.zeros((seq, LANES - MLA_QK_DIM), F32)
    cos_t = jnp.concatenate([jnp.ones((seq, MLA_NOPE_DIM), F32), cos, cos, pad], axis=1)
    sin_t = jnp.concatenate([jnp.zeros((seq, MLA_NOPE_DIM), F32), -sin, sin, pad], axis=1)
    return cos_t, sin_t


def _swap_halves(w):
    half = w.shape[1] // 2
    return jnp.concatenate([w[:, half:], w[:, :half]], axis=1)


def _layer_params(l, w_in, forget_bias, w_uq, w_ukv, group_norm, w_out, w_up, conv_w, conv_b, w_down):
    wi = w_in[l]
    d = wi.shape[0]
    order = jnp.asarray(SWA_HEAD_ORDER)
    swa_q = wi[:, 0:SWA_WIDTH].reshape(d, SWA_Q_HEADS, HEAD_DIM)[:, order].reshape(d, SWA_WIDTH)
    wmain = jnp.concatenate([swa_q, wi[:, SWA_WIDTH:SWA_COLS + 3 * FOX_WIDTH]], axis=1).astype(BF16)

    f0 = SWA_COLS + 3 * FOX_WIDTH
    m0 = SWA_COLS + FOX_COLS
    z = lambda n: jnp.zeros((d, n), F32)
    w_kr = wi[:, m0 + MLA_Q_RANK + MLA_KV_RANK:]
    rope_pad = LANES - MLA_QK_DIM
    wside = jnp.concatenate([
        wi[:, m0:m0 + MLA_Q_RANK + MLA_KV_RANK],
        z(MLA_NOPE_DIM), w_kr, z(rope_pad),
        z(MLA_NOPE_DIM), _swap_halves(w_kr), z(rope_pad),
        wi[:, f0:f0 + FOX_HEADS], z(LANES - FOX_HEADS),
    ], axis=1).astype(BF16)
    fb = jnp.pad(forget_bias[l], (0, LANES - FOX_HEADS))[None, :]

    uq = w_uq[l].reshape(MLA_Q_RANK, MLA_HEADS, MLA_QK_DIM)
    zq = jnp.zeros((MLA_Q_RANK, MLA_HEADS, 1), F32)
    plain = jnp.concatenate([uq, jnp.tile(zq, (1, 1, rope_pad))], axis=2)
    rope = uq[:, :, MLA_NOPE_DIM:]
    swapped = jnp.concatenate([jnp.tile(zq, (1, 1, MLA_NOPE_DIM)), rope[:, :, MLA_ROPE_DIM // 2:],
                               rope[:, :, :MLA_ROPE_DIM // 2], jnp.tile(zq, (1, 1, rope_pad))], axis=2)
    wuq = jnp.concatenate([plain.reshape(MLA_Q_RANK, -1), swapped.reshape(MLA_Q_RANK, -1)], axis=1).astype(BF16)

    ukv = w_ukv[l].reshape(MLA_KV_RANK, MLA_HEADS, MLA_NOPE_DIM + MLA_V_DIM)
    k_slabs = jnp.concatenate([ukv[:, :, :MLA_NOPE_DIM],
                               jnp.zeros((MLA_KV_RANK, MLA_HEADS, LANES - MLA_NOPE_DIM), F32)], axis=2)
    wukv = jnp.concatenate([k_slabs.reshape(MLA_KV_RANK, -1),
                            ukv[:, :, MLA_NOPE_DIM:].reshape(MLA_KV_RANK, -1)], axis=1).astype(BF16)

    gg = group_norm[l]
    gg = jnp.concatenate([gg[:SWA_WIDTH].reshape(SWA_Q_HEADS, HEAD_DIM)[order].reshape(-1), gg[SWA_WIDTH:]])[None, :]
    wo = w_out[l]
    wo = jnp.concatenate([wo[:SWA_WIDTH].reshape(SWA_Q_HEADS, HEAD_DIM, -1)[order].reshape(SWA_WIDTH, -1),
                          wo[SWA_WIDTH:]], axis=0).astype(BF16)

    wup = w_up[l].astype(BF16)
    halve_up = jnp.concatenate([jnp.ones((D_FF,), F32), jnp.full((D_FF,), 0.5, F32)])
    cw = conv_w[l] * halve_up
    cb = (conv_b[l] * halve_up)[None, :]
    wdown = w_down[l].astype(BF16).reshape(N_FF_CHUNKS, FF_CHUNK, -1)
    return wmain, wside, fb, wuq, wukv, gg, wo, wup, cw, cb, wdown


def kernel(x, attn_pre_norm, w_in, forget_bias, swa_sinks, rel_bias, q_latent_norm, w_uq, kv_latent_norm, w_ukv,
           group_norm, w_out, attn_post_norm, ffn_pre_norm, w_up, conv_w, conv_b, w_down, ffn_post_norm):
    batch, seq, d = x.shape
    assert d == D_MODEL and all(seq % t == 0 for t in (ROW_TILE, ATTN_TILE, IN_PROJ_TILE, MIX_TILE))
    depth = w_in.shape[0]
    cos_t, sin_t = _rope_tables(seq)
    bias = _swa_bias_table(rel_bias)
    x2 = x.reshape(batch * seq, d)
    for l in range(depth):
        wmain, wside, fb, wuq, wukv, gg, wo, wup, cw, cb, wdown = _layer_params(
            l, w_in, forget_bias, w_uq, w_ukv, group_norm, w_out, w_up, conv_w, conv_b, w_down)
        sinks = jnp.repeat((swa_sinks[l].astype(F32) * LOG2E).reshape(SWA_KV_HEADS, -1), WINDOW, axis=1)
        sinks = jnp.broadcast_to(sinks[:, :, None], sinks.shape + (LANES,))
        (swa_q, swa_k, swa_v, fox_q, fox_k, fox_v, mla_q, mla_k, mla_v) = _in_proj(
            x2, attn_pre_norm[l][None, :], wmain, wside, fb, q_latent_norm[l][None, :], wuq,
            kv_latent_norm[l][None, :], wukv, cos_t, sin_t, batch, seq)
        out_a = _swa(swa_q, swa_k, swa_v, bias, sinks, batch, seq)
        out_b = _flash(fox_q, fox_k, fox_v, batch, seq, "fox_attn")
        out_c = _flash(mla_q, mla_k, mla_v, batch, seq, "mla_attn")
        x2 = _mix_ffn(out_a, out_b, out_c, gg, wo, attn_post_norm[l][None, :], x2, ffn_pre_norm[l][None, :],
                      wup, cw, cb, wdown, ffn_post_norm[l][None, :], batch, seq)
    return x2.reshape(batch, seq, d)
```
